```python
import jax, jax.numpy as jnp
from jax import lax
import numpy as np

D_MODEL = 1024
BATCH = 4
SEQ = 4096
DEPTH = 2
DEC_BATCH = 128
DEC_SEQ = 1
PAST_LEN = 16384
PAGE_SIZE = 128

N_A_LAYERS = DEPTH // 2
N_B_LAYERS = DEPTH - N_A_LAYERS
EXPAND = 2
MIX_WIDTH = EXPAND * D_MODEL
HGRN_HEAD_DIM = 128
HGRN_HEADS = MIX_WIDTH // HGRN_HEAD_DIM
CHUNK = 16
HEAD_DIM = 64
N_Q_HEADS = MIX_WIDTH // HEAD_DIM
KV_HEADS = N_Q_HEADS // 8
Q_PER_KV = N_Q_HEADS // KV_HEADS
WINDOW = 128
ROT_DIM = HEAD_DIM // 4
ROPE_THETA = 500000.0
PLE_DIM = 256
EPS = 1e-6
ATTN_SCALE = HEAD_DIM ** -0.5

kernel_name = "yoco_hgrn2_swa_sink_step"


def rmsnorm(x, g):
    xf = x.astype(jnp.float32)
    y = xf * lax.rsqrt(jnp.mean(xf * xf, axis=-1, keepdims=True) + EPS)
    return (y * g.astype(jnp.float32)).astype(x.dtype)


def rope_partial(x, pos):
    inv = ROPE_THETA ** (-jnp.arange(0, ROT_DIM, 2, dtype=jnp.float32) / ROT_DIM)
    ang = pos.astype(jnp.float32)[:, None] * inv[None, :]
    cos = jnp.cos(ang)[None, :, None, :]
    sin = jnp.sin(ang)[None, :, None, :]
    xf = x.astype(jnp.float32)
    x1 = xf[..., :ROT_DIM // 2]
    x2 = xf[..., ROT_DIM // 2:ROT_DIM]
    out = jnp.concatenate([x1 * cos - x2 * sin, x2 * cos + x1 * sin, xf[..., ROT_DIM:]], axis=-1)
    return out.astype(x.dtype)


def hgrn2_chunked(q, g, k, v, S0):
    B, L, H, D = q.shape
    c = min(CHUNK, L)
    nc = -(-L // c)
    pad = nc * c - L

    def blocks(t):
        t = jnp.pad(t, ((0, 0), (0, pad), (0, 0), (0, 0)))
        return t.reshape(B, nc, c, H, D).transpose(1, 0, 3, 2, 4)

    causal = jnp.tril(jnp.ones((c, c), dtype=bool))

    def step(S, xs):
        qc, gc, kc, vc = xs
        b = jnp.cumsum(gc, axis=2)
        o_inter = jnp.einsum('bhtk,bhkv->bhtv', qc * jnp.exp(b), S)
        diff = b[:, :, :, None, :] - b[:, :, None, :, :]
        decay = jnp.exp(jnp.where(causal[:, :, None], diff, -jnp.inf))
        A = jnp.einsum('bhtk,bhsk,bhtsk->bhts', qc, kc, decay)
        o = o_inter + jnp.einsum('bhts,bhsv->bhtv', A, vc)
        b_last = b[:, :, -1]
        S_new = jnp.exp(b_last)[..., None] * S + jnp.einsum(
            'bhsk,bhsv->bhkv', kc * jnp.exp(b_last[:, :, None, :] - b), vc)
        return S_new, o

    S, o = lax.scan(step, S0, (blocks(q), blocks(g), blocks(k), blocks(v)))
    o = o.transpose(1, 0, 3, 2, 4).reshape(B, nc * c, H, D)[:, :L]
    return o, S


def hgrn2_mixer(xn, w_in, lb, onorm_g, w_out, S0):
    B, L, _ = xn.shape
    u = xn @ w_in
    q, fpre, i, z = jnp.split(u, 4, axis=-1)
    q = jax.nn.silu(q.astype(jnp.float32))
    f = lb + (1.0 - lb) * jax.nn.sigmoid(fpre.astype(jnp.float32))
    g = jnp.log(f)
    k = 1.0 - f
    heads = lambda t: t.reshape(B, L, HGRN_HEADS, HGRN_HEAD_DIM)
    o, S = hgrn2_chunked(heads(q), heads(g), heads(k), heads(i.astype(jnp.float32)), S0)
    o = rmsnorm(o, onorm_g.reshape(HGRN_HEADS, HGRN_HEAD_DIM))
    o = o.reshape(B, L, MIX_WIDTH).astype(xn.dtype) * jax.nn.silu(z)
    return o @ w_out, S


def sink_softmax(s, sinks):
    sink = sinks.astype(jnp.float32).reshape(KV_HEADS, Q_PER_KV)[:, :, None, None]
    sink = jnp.broadcast_to(sink, s.shape[:-1] + (1,))
    pr = jax.nn.softmax(jnp.concatenate([s, sink], axis=-1), axis=-1)
    return pr[..., :-1]


def window_attn_prompt(q, k, v, sinks):
    B, L = q.shape[0], q.shape[1]
    nb = L // WINDOW
    qb = q.reshape(B, nb, WINDOW, KV_HEADS, Q_PER_KV, HEAD_DIM).transpose(1, 0, 2, 3, 4, 5)

    def band(t):
        tp = jnp.pad(t, ((0, 0), (WINDOW, 0), (0, 0), (0, 0)))
        prev = tp[:, :L].reshape(B, nb, WINDOW, KV_HEADS, HEAD_DIM)
        cur = t.reshape(B, nb, WINDOW, KV_HEADS, HEAD_DIM)
        return jnp.concatenate([prev, cur], axis=2).transpose(1, 0, 2, 3, 4)

    kb, vb = band(k), band(v)
    a = jnp.arange(WINDOW)[:, None]
    cidx = jnp.arange(2 * WINDOW)[None, :]
    rel = a + WINDOW - cidx
    in_band = (rel >= 0) & (rel < WINDOW)

    def one_block(args):
        qn, kn, vn, n = args
        mask = in_band & ((cidx >= WINDOW) | (n > 0))
        s = jnp.einsum('bqhgd,bkhd->bhgqk', qn.astype(jnp.float32), kn.astype(jnp.float32)) * ATTN_SCALE
        s = jnp.where(mask, s, -jnp.inf)
        pr = sink_softmax(s, sinks)
        return jnp.einsum('bhgqk,bkhd->bqhgd', pr, vn.astype(jnp.float32))

    o = lax.map(one_block, (qb, kb, vb, jnp.arange(nb)))
    return o.transpose(1, 0, 2, 3, 4, 5).reshape(B, L, N_Q_HEADS, HEAD_DIM).astype(q.dtype)


def window_attn_sample(q, k_all, v_all, qpos, kpos, sinks):
    B, T = q.shape[0], q.shape[1]
    qg = q.reshape(B, T, KV_HEADS, Q_PER_KV, HEAD_DIM).astype(jnp.float32)
    rel = qpos[:, None] - kpos[None, :]
    mask = (rel >= 0) & (rel < WINDOW)
    s = jnp.einsum('bqhgd,bkhd->bhgqk', qg, k_all.astype(jnp.float32)) * ATTN_SCALE
    s = jnp.where(mask, s, -jnp.inf)
    pr = sink_softmax(s, sinks)
    o = jnp.einsum('bhgqk,bkhd->bqhgd', pr, v_all.astype(jnp.float32))
    return o.reshape(B, T, N_Q_HEADS, HEAD_DIM).astype(q.dtype)


def trunk(x, p, pos, hgrn_state, k_past, v_past,
          pre_norm_g, post_norm_g, w_in_a, lb_logits, onorm_a, w_out_a,
          kv_norm_g, w_kv, w_in_b, sinks, w_out_b, w_pe, w_pg):
    B, L, _ = x.shape
    lb_all = jnp.cumsum(jax.nn.softmax(lb_logits.astype(jnp.float32), axis=0), axis=0)
    h = x
    states = []
    k_sh = v_sh = None
    for l in range(DEPTH):
        xn = rmsnorm(h, pre_norm_g[l])
        if l < N_A_LAYERS:
            if hgrn_state is None:
                S0 = jnp.zeros((B, HGRN_HEADS, HGRN_HEAD_DIM, HGRN_HEAD_DIM), jnp.float32)
            else:
                S0 = hgrn_state[l].astype(jnp.float32)
            mix, S = hgrn2_mixer(xn, w_in_a[l], lb_all[l], onorm_a[l], w_out_a[l], S0)
            states.append(S)
        else:
            if l == N_A_LAYERS:
                kv = rmsnorm(h, kv_norm_g) @ w_kv
                k_sh, v_sh = jnp.split(kv, 2, axis=-1)
                k_sh = rope_partial(k_sh.reshape(B, L, KV_HEADS, HEAD_DIM), pos)
                v_sh = v_sh.reshape(B, L, KV_HEADS, HEAD_DIM)
            j = l - N_A_LAYERS
            u = xn @ w_in_b[j]
            q, z = jnp.split(u, 2, axis=-1)
            q = rope_partial(q.reshape(B, L, N_Q_HEADS, HEAD_DIM), pos)
            if k_past is None:
                o = window_attn_prompt(q, k_sh, v_sh, sinks[j])
            else:
                wb = k_past.shape[1]
                kpos = jnp.concatenate([PAST_LEN - wb + jnp.arange(wb), pos])
                k_all = jnp.concatenate([k_past.astype(k_sh.dtype), k_sh], axis=1)
                v_all = jnp.concatenate([v_past.astype(v_sh.dtype), v_sh], axis=1)
                o = window_attn_sample(q, k_all, v_all, pos, kpos, sinks[j])
            mix = (o.reshape(B, L, MIX_WIDTH) * jax.nn.silu(z)) @ w_out_b[j]
        h = h + rmsnorm(mix, post_norm_g[l])
        h = h + (p[l] @ w_pe[l]) * jax.nn.sigmoid(h @ w_pg[l])
    return h, jnp.stack(states), k_sh, v_sh


def setup_inputs(seed: int = 0) -> dict:
    key = jax.random.key(seed)
    ks = jax.random.split(key, 20)
    f32 = jnp.float32
    w_buf = min(WINDOW, PAST_LEN)

    def nrm(k, shape, scale=1.0):
        return jax.random.normal(k, shape, f32) * scale

    return {
        "x_prompt": nrm(ks[0], (BATCH, SEQ, D_MODEL)),
        "x_sample": nrm(ks[1], (DEC_BATCH, DEC_SEQ, D_MODEL)),
        "p_prompt": nrm(ks[2], (DEPTH, BATCH, SEQ, PLE_DIM)),
        "p_sample": nrm(ks[3], (DEPTH, DEC_BATCH, DEC_SEQ, PLE_DIM)),
        "state_hgrn": nrm(ks[4], (N_A_LAYERS, DEC_BATCH, HGRN_HEADS, HGRN_HEAD_DIM, HGRN_HEAD_DIM), 0.5),
        "cache_k": nrm(ks[5], (DEC_BATCH, w_buf, KV_HEADS, HEAD_DIM)),
        "cache_v": nrm(ks[6], (DEC_BATCH, w_buf, KV_HEADS, HEAD_DIM)),
        "pre_norm_g": 1.0 + nrm(ks[7], (DEPTH, D_MODEL), 0.02),
        "post_norm_g": 1.0 + nrm(ks[8], (DEPTH, D_MODEL), 0.02),
        "w_in_a": nrm(ks[9], (N_A_LAYERS, D_MODEL, 4 * MIX_WIDTH), D_MODEL ** -0.5),
        "lb_logits": nrm(ks[10], (N_A_LAYERS + 1, MIX_WIDTH)),
        "onorm_a": 1.0 + nrm(ks[11], (N_A_LAYERS, MIX_WIDTH), 0.02),
        "w_out_a": nrm(ks[12], (N_A_LAYERS, MIX_WIDTH, D_MODEL), MIX_WIDTH ** -0.5),
        "kv_norm_g": 1.0 + nrm(ks[13], (D_MODEL,), 0.02),
        "w_kv": nrm(ks[14], (D_MODEL, 2 * KV_HEADS * HEAD_DIM), D_MODEL ** -0.5),
        "w_in_b": nrm(ks[15], (N_B_LAYERS, D_MODEL, 2 * MIX_WIDTH), D_MODEL ** -0.5),
        "sinks": nrm(ks[16], (N_B_LAYERS, N_Q_HEADS), 0.5),
        "w_out_b": nrm(ks[17], (N_B_LAYERS, MIX_WIDTH, D_MODEL), MIX_WIDTH ** -0.5),
        "w_pe": nrm(ks[18], (DEPTH, PLE_DIM, D_MODEL), PLE_DIM ** -0.5),
        "w_pg": nrm(ks[19], (DEPTH, D_MODEL, D_MODEL), D_MODEL ** -0.5),
    }


def reference(x_prompt, x_sample, p_prompt, p_sample, state_hgrn, cache_k, cache_v,
              pre_norm_g, post_norm_g, w_in_a, lb_logits, onorm_a, w_out_a,
              kv_norm_g, w_kv, w_in_b, sinks, w_out_b, w_pe, w_pg):
    L_p = x_prompt.shape[1]
    T_s = x_sample.shape[1]
    pos_p = jnp.arange(L_p)
    pos_s = PAST_LEN + jnp.arange(T_s)
    y_prompt, st_p, k_p, v_p = trunk(
        x_prompt, p_prompt, pos_p, None, None, None,
        pre_norm_g, post_norm_g, w_in_a, lb_logits, onorm_a, w_out_a,
        kv_norm_g, w_kv, w_in_b, sinks, w_out_b, w_pe, w_pg)
    y_sample, st_s, k_s, v_s = trunk(
        x_sample, p_sample, pos_s, state_hgrn, cache_k, cache_v,
        pre_norm_g, post_norm_g, w_in_a, lb_logits, onorm_a, w_out_a,
        kv_norm_g, w_kv, w_in_b, sinks, w_out_b, w_pe, w_pg)
    keep = min(WINDOW, L_p)
    k_prompt_rows = k_p[:, L_p - keep:]
    v_prompt_rows = v_p[:, L_p - keep:]
    return (y_prompt, y_sample, st_p, st_s, k_prompt_rows, v_prompt_rows, k_s, v_s)
```

```python
import functools

import jax
import jax.numpy as jnp
from jax import lax
from jax.experimental import pallas as pl
from jax.experimental.pallas import tpu as pltpu

F32 = jnp.float32
BF16 = jnp.bfloat16

EPS = 1e-6
HGRN_HEAD_DIM = 128
HEAD_DIM = 64
Q_PER_KV = 8
WINDOW = 128
ROT_DIM = 16
ROPE_THETA = 500000.0
PAST_LEN = 16384
ATTN_SCALE = HEAD_DIM ** -0.5

CHUNK = 128
LEVEL_HALVES = (64, 32, 16, 8, 4, 2, 1)
MASKED = -1e30
VMEM_LIMIT_BYTES = 56 * 1024 * 1024


def _sigmoid(x):
    return 1.0 / (1.0 + jnp.exp(-x))


def _silu(x):
    return x * _sigmoid(x)


def _rms(x):
    return x * lax.rsqrt(jnp.mean(x * x, axis=-1, keepdims=True) + EPS)


def _dot(a, b):
    return jnp.dot(a, b, preferred_element_type=F32)


def _dot_nt(a, b):
    return lax.dot_general(a, b, (((1,), (1,)), ((), ())), preferred_element_type=F32)


def _dot_tn(a, b):
    return lax.dot_general(a, b, (((0,), (0,)), ((), ())), preferred_element_type=F32)


def _block_diag2(a):
    c = a.shape[1] // 2
    lane = lax.broadcasted_iota(jnp.int32, a.shape, 1)
    zero = jnp.zeros_like(a)
    return jnp.concatenate([jnp.where(lane < c, a, zero), jnp.where(lane >= c, a, zero)], axis=0)


def _rope(x, cos, sin_lo, sin_hi):
    n = x.shape[-1]
    half = ROT_DIM // 2
    return x * cos + pltpu.roll(x, n - half, 1) * sin_lo + pltpu.roll(x, half, 1) * sin_hi


def _rope_tables(pos, reps):
    half = ROT_DIM // 2
    inv = ROPE_THETA ** (-jnp.arange(0, ROT_DIM, 2, dtype=F32) / ROT_DIM)
    ang = pos.astype(F32)[:, None] * inv[None, :]
    cos, sin = jnp.cos(ang), jnp.sin(ang)
    n = pos.shape[0]
    pad = jnp.zeros((n, HEAD_DIM - ROT_DIM), F32)
    zero = jnp.zeros((n, half), F32)
    c = jnp.concatenate([cos, cos, pad + 1.0], axis=1)
    s_lo = jnp.concatenate([-sin, zero, pad], axis=1)
    s_hi = jnp.concatenate([zero, sin, pad], axis=1)
    return tuple(jnp.tile(t, (1, reps)) for t in (c, s_lo, s_hi))


def _lower_bound(lbl):
    m = jnp.max(lbl, axis=0, keepdims=True)
    e = jnp.exp(lbl - m)
    return e[0:1] / jnp.sum(e, axis=0, keepdims=True)


def _level_ref(b, half, row):
    if half >= 8:
        size = 2 * half
        pieces = [jnp.broadcast_to(b[s + half - 1:s + half, :], (size, b.shape[1]))
                  for s in range(0, b.shape[0], size)]
        return pieces[0] if len(pieces) == 1 else jnp.concatenate(pieces, axis=0)
    n = b.shape[0]
    up = lambda x, d: pltpu.roll(x, n - d, 0)
    down = lambda x, d: pltpu.roll(x, d, 0)
    if half == 1:
        return jnp.where(row % 2 == 1, down(b, 1), b)
    y1 = jnp.where(row % 2 == 0, up(b, 1), b)
    if half == 2:
        return jnp.where(row % 4 >= 2, down(y1, 2), y1)
    y2 = jnp.where(row % 4 < 2, up(y1, 2), y1)
    return jnp.where(row % 8 >= 4, down(y2, 4), y2)


def _hgrn_chunk(q, f, v, st_a, st_b, tri):
    c, w = q.shape
    d = w // 2
    g = jnp.log(f)
    k = 1.0 - f
    g_hi = g.astype(BF16)
    r1 = g - g_hi.astype(F32)
    g_mid = r1.astype(BF16)
    g_lo = (r1 - g_mid.astype(F32)).astype(BF16)
    b = _dot(tri, g_hi) + _dot(tri, g_mid) + _dot(tri, g_lo)

    row = lax.broadcasted_iota(jnp.int32, (c, w), 0)
    col_tok = lax.broadcasted_iota(jnp.int32, (c, w), 1) % d
    zero = jnp.zeros((c, w), F32)

    qk = q * k
    diag = jnp.concatenate(
        [jnp.broadcast_to(jnp.sum(qk[:, :d], axis=-1, keepdims=True), (c, d)),
         jnp.broadcast_to(jnp.sum(qk[:, d:], axis=-1, keepdims=True), (c, d))], axis=1)
    a = jnp.where(row == col_tok, diag, zero)
    for half in LEVEL_HALVES:
        upper = row % (2 * half) >= half
        ref = _level_ref(b, half, row)
        e = jnp.exp(jnp.where(upper, b - ref, ref - b))
        qe = jnp.where(upper, q * e, zero).astype(BF16)
        ke = jnp.where(upper, zero, k * e).astype(BF16)
        p = _dot_nt(qe, _block_diag2(ke))
        a = a + jnp.where(row // (2 * half) == col_tok // (2 * half), p, zero)

    st_bd = _block_diag2(jnp.concatenate([st_a, st_b], axis=1).astype(BF16))
    o = _dot_nt((q * jnp.exp(b)).astype(BF16), st_bd) + _dot(a.astype(BF16), _block_diag2(v))

    b_last = b[c - 1:c, :]
    k_hat = (k * jnp.exp(b_last - b)).astype(BF16)
    ds = _dot_tn(v, k_hat)
    decay = jnp.exp(b_last)
    st_a = st_a * decay[:, :d] + ds[:d, :d]
    st_b = st_b * decay[:, d:] + ds[d:, d:]
    return o, st_a, st_b


def _l0_prompt_kernel(x_ref, pg_ref, w_ref, lbl_ref, on_ref, og_ref, st_ref,
                      s_scr, q_scr, f_scr, v_scr, z_scr):
    l = pl.program_id(1)
    tm = x_ref.shape[1]
    width = q_scr.shape[1]
    d = HGRN_HEAD_DIM
    n_heads = width // d

    @pl.when(l == 0)
    def _():
        s_scr[...] = jnp.zeros_like(s_scr)

    xn = (_rms(x_ref[0]) * pg_ref[...]).astype(BF16)
    lb = _lower_bound(lbl_ref[...])
    q_scr[...] = _silu(_dot(xn, w_ref[:, 0:width]))
    f_scr[...] = lb + (1.0 - lb) * _sigmoid(_dot(xn, w_ref[:, width:2 * width]))
    v_scr[...] = _dot(xn, w_ref[:, 2 * width:3 * width]).astype(BF16)
    z_scr[...] = _silu(_dot(xn, w_ref[:, 3 * width:4 * width]))

    ti = lax.broadcasted_iota(jnp.int32, (CHUNK, CHUNK), 0)
    tj = lax.broadcasted_iota(jnp.int32, (CHUNK, CHUNK), 1)
    tri = jnp.where(tj <= ti, 1.0, 0.0).astype(BF16)

    def pair_body(pi, carry):
        col = pl.multiple_of(pi * 2 * d, 2 * d)
        st_a = s_scr[2 * pi]
        st_b = s_scr[2 * pi + 1]
        gain = on_ref[:, pl.ds(col, 2 * d)]
        for c in range(tm // CHUNK):
            rows = slice(c * CHUNK, (c + 1) * CHUNK)
            o, st_a, st_b = _hgrn_chunk(q_scr[rows, pl.ds(col, 2 * d)], f_scr[rows, pl.ds(col, 2 * d)],
                                        v_scr[rows, pl.ds(col, 2 * d)], st_a, st_b, tri)
            on = jnp.concatenate([_rms(o[:, :d]), _rms(o[:, d:])], axis=1) * gain
            og_ref[0, rows, pl.ds(col, 2 * d)] = (on * z_scr[rows, pl.ds(col, 2 * d)]).astype(og_ref.dtype)
        s_scr[2 * pi] = st_a
        s_scr[2 * pi + 1] = st_b
        return carry

    lax.fori_loop(0, n_heads // 2, pair_body, 0)

    @pl.when(l == pl.num_programs(1) - 1)
    def _():
        for h in range(n_heads):
            st_ref[0, h] = s_scr[h].T


def _l0_prompt(x, pre_g, w_in, lb_logits, onorm, tm=256):
    bsz, seq, dm = x.shape
    width = w_in.shape[1] // 4
    n_heads = width // HGRN_HEAD_DIM
    d = HGRN_HEAD_DIM
    const = lambda *shape: pl.BlockSpec(shape, lambda b, l: (0,) * len(shape))
    return pl.pallas_call(
        _l0_prompt_kernel,
        grid=(bsz, seq // tm),
        in_specs=[
            pl.BlockSpec((1, tm, dm), lambda b, l: (b, l, 0)),
            const(1, dm),
            pl.BlockSpec((dm, 4 * width), lambda b, l: (0, 0), pipeline_mode=pl.Buffered(1)),
            const(*lb_logits.shape),
            const(1, width),
        ],
        out_specs=[
            pl.BlockSpec((1, tm, width), lambda b, l: (b, l, 0)),
            pl.BlockSpec((1, n_heads, d, d), lambda b, l: (b, 0, 0, 0)),
        ],
        out_shape=[
            jax.ShapeDtypeStruct((bsz, seq, width), BF16),
            jax.ShapeDtypeStruct((bsz, n_heads, d, d), F32),
        ],
        scratch_shapes=[
            pltpu.VMEM((n_heads, d, d), F32),
            pltpu.VMEM((tm, width), F32),
            pltpu.VMEM((tm, width), F32),
            pltpu.VMEM((tm, width), BF16),
            pltpu.VMEM((tm, width), F32),
        ],
        compiler_params=pltpu.CompilerParams(
            dimension_semantics=("arbitrary", "arbitrary"), vmem_limit_bytes=VMEM_LIMIT_BYTES),
        name="l0_prompt",
    )(x, pre_g, w_in, lb_logits, onorm)


def _epilogue_kernel(og_ref, h_ref, p_ref, wo_ref, pg_ref, wpe_ref, wpg_ref, out_ref):
    mix = _dot(og_ref[...].astype(BF16), wo_ref[...])
    hn = h_ref[...] + _rms(mix) * pg_ref[...]
    pe = _dot(p_ref[...].astype(BF16), wpe_ref[...])
    gate = _sigmoid(_dot(hn.astype(BF16), wpg_ref[...]))
    out_ref[...] = hn + pe * gate


def _epilogue(og, h, p, w_out, post_g, w_pe, w_pg, tm):
    t, dm = h.shape
    width = og.shape[1]
    ple = p.shape[1]
    tm = min(tm, t)
    const = lambda *shape: pl.BlockSpec(shape, lambda i: (0,) * len(shape))
    return pl.pallas_call(
        _epilogue_kernel,
        grid=(t // tm,),
        in_specs=[
            pl.BlockSpec((tm, width), lambda i: (i, 0)),
            pl.BlockSpec((tm, dm), lambda i: (i, 0)),
            pl.BlockSpec((tm, ple), lambda i: (i, 0)),
            const(width, dm), const(1, dm), const(ple, dm), const(dm, dm),
        ],
        out_specs=pl.BlockSpec((tm, dm), lambda i: (i, 0)),
        out_shape=jax.ShapeDtypeStruct((t, dm), F32),
        compiler_params=pltpu.CompilerParams(
            dimension_semantics=("arbitrary",), vmem_limit_bytes=VMEM_LIMIT_BYTES),
        name="epilogue",
    )(og, h, p, w_out, post_g, w_pe, w_pg)


def _l1_prompt_kernel(sinks_ref, h_ref, g1_ref, kvg_ref, wkv_ref, win_ref, cos_ref, slo_ref, shi_ref,
                      og_ref, kc_ref, vc_ref, kx_scr, vx_scr, q_scr, z_scr):
    l = pl.program_id(1)
    tm = h_ref.shape[1]
    width = z_scr.shape[1]
    kvw = kx_scr.shape[1]
    pair_w = 2 * HEAD_DIM
    n_pairs = width // pair_w
    pairs_per_kv = Q_PER_KV // 2

    @pl.when(l == 0)
    def _():
        kx_scr[0:WINDOW, :] = jnp.zeros((WINDOW, kvw), kx_scr.dtype)
        vx_scr[0:WINDOW, :] = jnp.zeros((WINDOW, kvw), vx_scr.dtype)

    @pl.when(l > 0)
    def _():
        kx_scr[0:WINDOW, :] = kx_scr[tm:tm + WINDOW, :]
        vx_scr[0:WINDOW, :] = vx_scr[tm:tm + WINDOW, :]

    h0 = _rms(h_ref[0])
    xn = (h0 * g1_ref[...]).astype(BF16)
    hk = (h0 * kvg_ref[...]).astype(BF16)
    cos, slo, shi = cos_ref[...], slo_ref[...], shi_ref[...]
    tile = lambda t, n: jnp.concatenate([t] * n, axis=1)

    kv = _dot(hk, wkv_ref[...])
    k_rot = _rope(kv[:, :kvw], tile(cos, kvw // pair_w), tile(slo, kvw // pair_w), tile(shi, kvw // pair_w))
    v_new = kv[:, kvw:]
    kx_scr[WINDOW:WINDOW + tm, :] = k_rot.astype(kx_scr.dtype)
    vx_scr[WINDOW:WINDOW + tm, :] = v_new.astype(vx_scr.dtype)

    @pl.when(l == pl.num_programs(1) - 1)
    def _():
        kc_ref[0] = k_rot[tm - WINDOW:, :]
        vc_ref[0] = v_new[tm - WINDOW:, :]

    q = _rope(_dot(xn, win_ref[:, 0:width]), tile(cos, n_pairs), tile(slo, n_pairs), tile(shi, n_pairs))
    q_scr[...] = (q * ATTN_SCALE).astype(q_scr.dtype)
    z_scr[...] = _silu(_dot(xn, win_ref[:, width:2 * width]))

    qi = lax.broadcasted_iota(jnp.int32, (WINDOW, 2 * WINDOW), 0)
    ki = lax.broadcasted_iota(jnp.int32, (WINDOW, 2 * WINDOW), 1)
    band = (ki > qi) & (ki <= qi + WINDOW)
    lane = lax.broadcasted_iota(jnp.int32, (WINDOW, pair_w), 1)

    def pair_body(pi, carry):
        col = pl.multiple_of(pi * pair_w, pair_w)
        kcol = pl.multiple_of((pi // pairs_per_kv) * pair_w, pair_w)
        sink_a = sinks_ref[2 * pi]
        sink_b = sinks_ref[2 * pi + 1]
        for c in range(tm // WINDOW):
            rows = slice(c * WINDOW, (c + 1) * WINDOW)
            keys = slice(c * WINDOW, (c + 2) * WINDOW)
            k_bd = _block_diag2(kx_scr[keys, pl.ds(kcol, pair_w)])
            v_bd = _block_diag2(vx_scr[keys, pl.ds(kcol, pair_w)])
            s2 = _dot_nt(q_scr[rows, pl.ds(col, pair_w)], k_bd)
            mask = band & (ki >= jnp.where(l == 0, WINDOW, 0)) if c == 0 else band
            probs, inv = [], []
            for s, sink in ((s2[:, :2 * WINDOW], sink_a), (s2[:, 2 * WINDOW:], sink_b)):
                s = jnp.where(mask, s, MASKED)
                mx = jnp.maximum(jnp.max(s, axis=-1, keepdims=True), sink)
                e = jnp.exp(s - mx)
                inv.append(1.0 / (jnp.sum(e, axis=-1, keepdims=True) + jnp.exp(sink - mx)))
                probs.append(e.astype(BF16))
            o2 = _dot(jnp.concatenate(probs, axis=1), v_bd)
            o2 = o2 * jnp.where(lane < HEAD_DIM, inv[0], inv[1])
            og_ref[0, rows, pl.ds(col, pair_w)] = (o2 * z_scr[rows, pl.ds(col, pair_w)]).astype(og_ref.dtype)
        return carry

    lax.fori_loop(0, n_pairs, pair_body, 0)


def _l1_prompt(h, g1, kvg, w_kv_dup, w_in, sinks, tables, tm=256):
    bsz, seq, dm = h.shape
    width = w_in.shape[1] // 2
    kvw = w_kv_dup.shape[1] // 2
    pair_w = 2 * HEAD_DIM
    const = lambda *shape: pl.BlockSpec(shape, lambda b, l: (0,) * len(shape))
    table = pl.BlockSpec((tm, pair_w), lambda b, l: (l, 0))
    return pl.pallas_call(
        _l1_prompt_kernel,
        grid=(bsz, seq // tm),
        in_specs=[
            pl.BlockSpec(memory_space=pltpu.SMEM),
            pl.BlockSpec((1, tm, dm), lambda b, l: (b, l, 0)),
            const(1, dm), const(1, dm), const(dm, 2 * kvw), const(dm, 2 * width),
            table, table, table,
        ],
        out_specs=[
            pl.BlockSpec((1, tm, width), lambda b, l: (b, l, 0)),
            pl.BlockSpec((1, WINDOW, kvw), lambda b, l: (b, 0, 0)),
            pl.BlockSpec((1, WINDOW, kvw), lambda b, l: (b, 0, 0)),
        ],
        out_shape=[
            jax.ShapeDtypeStruct((bsz, seq, width), BF16),
            jax.ShapeDtypeStruct((bsz, WINDOW, kvw), F32),
            jax.ShapeDtypeStruct((bsz, WINDOW, kvw), F32),
        ],
        scratch_shapes=[
            pltpu.VMEM((WINDOW + tm, kvw), BF16),
            pltpu.VMEM((WINDOW + tm, kvw), BF16),
            pltpu.VMEM((tm, width), BF16),
            pltpu.VMEM((tm, width), F32),
        ],
        compiler_params=pltpu.CompilerParams(
            dimension_semantics=("arbitrary", "arbitrary"), vmem_limit_bytes=VMEM_LIMIT_BYTES),
        name="l1_prompt",
    )(sinks, h, g1, kvg, w_kv_dup, w_in, *tables)


def _norm_matmul_kernel(x_ref, g_ref, w_ref, o_ref):
    xn = (_rms(x_ref[...]) * g_ref[...]).astype(BF16)
    o_ref[...] = _dot(xn, w_ref[...])


def _norm_matmul(x, g, w, tn=2048):
    t, dm = x.shape
    n = w.shape[1]
    tn = min(tn, n)
    return pl.pallas_call(
        _norm_matmul_kernel,
        grid=(n // tn,),
        in_specs=[
            pl.BlockSpec((t, dm), lambda j: (0, 0)),
            pl.BlockSpec((1, dm), lambda j: (0, 0)),
            pl.BlockSpec((dm, tn), lambda j: (0, j)),
        ],
        out_specs=pl.BlockSpec((t, tn), lambda j: (0, j)),
        out_shape=jax.ShapeDtypeStruct((t, n), F32),
        compiler_params=pltpu.CompilerParams(
            dimension_semantics=("arbitrary",), vmem_limit_bytes=VMEM_LIMIT_BYTES),
        name="norm_matmul",
    )(x, g, w)


def _l0_sample_kernel(u_ref, lbl_ref, on_ref, s0_ref, s1_ref, og_ref):
    bt = u_ref.shape[0]
    width = og_ref.shape[1]
    d = HGRN_HEAD_DIM
    lb_all = _lower_bound(lbl_ref[...])

    for h in range(width // d):
        cols = slice(h * d, (h + 1) * d)
        lb = lb_all[:, cols]
        q = _silu(u_ref[:, h * d:(h + 1) * d])
        f = lb + (1.0 - lb) * _sigmoid(u_ref[:, width + h * d:width + (h + 1) * d])
        v = u_ref[:, 2 * width + h * d:2 * width + (h + 1) * d]
        z = _silu(u_ref[:, 3 * width + h * d:3 * width + (h + 1) * d])
        f_t = f.T
        q_t = q.T
        outs = []
        for j in range(bt):
            f_col = f_t[:, j:j + 1]
            s_new = f_col * s0_ref[j, h] + (1.0 - f_col) * v[j:j + 1, :]
            s1_ref[j, h] = s_new
            outs.append(jnp.sum(q_t[:, j:j + 1] * s_new, axis=0, keepdims=True))
        o = jnp.concatenate(outs, axis=0)
        og_ref[:, cols] = _rms(o) * on_ref[:, cols] * z


def _l0_sample(u, lb_logits, onorm, state, bt=8):
    n, heads, d, _ = state.shape
    width = heads * d
    return pl.pallas_call(
        _l0_sample_kernel,
        grid=(n // bt,),
        in_specs=[
            pl.BlockSpec((bt, 4 * width), lambda i: (i, 0)),
            pl.BlockSpec(lb_logits.shape, lambda i: (0, 0)),
            pl.BlockSpec((1, width), lambda i: (0, 0)),
            pl.BlockSpec((bt, heads, d, d), lambda i: (i, 0, 0, 0)),
        ],
        out_specs=[
            pl.BlockSpec((bt, heads, d, d), lambda i: (i, 0, 0, 0)),
            pl.BlockSpec((bt, width), lambda i: (i, 0)),
        ],
        out_shape=[
            jax.ShapeDtypeStruct(state.shape, F32),
            jax.ShapeDtypeStruct((n, width), F32),
        ],
        compiler_params=pltpu.CompilerParams(
            dimension_semantics=("arbitrary",), vmem_limit_bytes=VMEM_LIMIT_BYTES),
        name="l0_sample",
    )(u, lb_logits, onorm, state)


def _l1_sample_front_kernel(h_ref, g1_ref, kvg_ref, win_ref, wkv_ref, cos_ref, slo_ref, shi_ref,
                            q_ref, z_ref, k_ref, v_ref):
    width = q_ref.shape[1]
    kvw = k_ref.shape[1]
    h0 = _rms(h_ref[...])
    xn = (h0 * g1_ref[...]).astype(BF16)
    hk = (h0 * kvg_ref[...]).astype(BF16)
    cos, slo, shi = cos_ref[...], slo_ref[...], shi_ref[...]
    q = _rope(_dot(xn, win_ref[:, 0:width]), cos, slo, shi)
    q_ref[...] = q * ATTN_SCALE
    z_ref[...] = _silu(_dot(xn, win_ref[:, width:2 * width]))
    kv = _dot(hk, wkv_ref[...])
    k_ref[...] = _rope(kv[:, :kvw], cos[:, :kvw], slo[:, :kvw], shi[:, :kvw])
    v_ref[...] = kv[:, kvw:]


def _l1_sample_front(h, g1, kvg, w_in, w_kv, tables):
    t, dm = h.shape
    width = w_in.shape[1] // 2
    kvw = w_kv.shape[1] // 2
    full = lambda a: pl.BlockSpec(a.shape, lambda i: (0,) * a.ndim)
    args = (h, g1, kvg, w_in, w_kv, *tables)
    return pl.pallas_call(
        _l1_sample_front_kernel,
        grid=(1,),
        in_specs=[full(a) for a in args],
        out_specs=[pl.BlockSpec((t, width), lambda i: (0, 0)), pl.BlockSpec((t, width), lambda i: (0, 0)),
                   pl.BlockSpec((t, kvw), lambda i: (0, 0)), pl.BlockSpec((t, kvw), lambda i: (0, 0))],
        out_shape=[jax.ShapeDtypeStruct((t, width), F32), jax.ShapeDtypeStruct((t, width), F32),
                   jax.ShapeDtypeStruct((t, kvw), F32), jax.ShapeDtypeStruct((t, kvw), F32)],
        compiler_params=pltpu.CompilerParams(
            dimension_semantics=("arbitrary",), vmem_limit_bytes=VMEM_LIMIT_BYTES),
        name="l1_sample_front",
    )(*args)


def _l1_sample_attn_kernel(q_ref, z_ref, kn_ref, vn_ref, ck_ref, cv_ref, sink_ref, og_ref):
    bt, wb = ck_ref.shape[0], ck_ref.shape[1]
    kv_heads = kn_ref.shape[1]
    valid = lax.broadcasted_iota(jnp.int32, (Q_PER_KV, wb), 1) > wb - WINDOW
    for j in range(bt):
        for g in range(kv_heads):
            heads = slice(g * Q_PER_KV, (g + 1) * Q_PER_KV)
            lanes = slice(g * HEAD_DIM, (g + 1) * HEAD_DIM)
            q = q_ref[j, heads, :]
            sink = sink_ref[heads, :]
            s = jnp.where(valid, _dot_nt(q.astype(BF16), ck_ref[j, :, lanes].astype(BF16)), MASKED)
            s_new = jnp.sum(q * kn_ref[j, g:g + 1, :], axis=-1, keepdims=True)
            mx = jnp.maximum(jnp.maximum(jnp.max(s, axis=-1, keepdims=True), s_new), sink)
            e = jnp.exp(s - mx)
            e_new = jnp.exp(s_new - mx)
            den = jnp.sum(e, axis=-1, keepdims=True) + e_new + jnp.exp(sink - mx)
            o = _dot(e.astype(BF16), cv_ref[j, :, lanes].astype(BF16)) + e_new * vn_ref[j, g:g + 1, :]
            og_ref[j, heads, :] = o / den * z_ref[j, heads, :]


def _l1_sample_attn(q3, z3, k_new, v_new, cache_k, cache_v, sinks_col, bt=8):
    n, heads, hd = q3.shape
    wb, kvw = cache_k.shape[1], cache_k.shape[2]
    kv_heads = k_new.shape[1]
    qspec = pl.BlockSpec((bt, heads, hd), lambda i: (i, 0, 0))
    nspec = pl.BlockSpec((bt, kv_heads, hd), lambda i: (i, 0, 0))
    cspec = pl.BlockSpec((bt, wb, kvw), lambda i: (i, 0, 0))
    return pl.pallas_call(
        _l1_sample_attn_kernel,
        grid=(n // bt,),
        in_specs=[qspec, qspec, nspec, nspec, cspec, cspec, pl.BlockSpec((heads, 1), lambda i: (0, 0))],
        out_specs=qspec,
        out_shape=jax.ShapeDtypeStruct((n, heads, hd), F32),
        compiler_params=pltpu.CompilerParams(
            dimension_semantics=("arbitrary",), vmem_limit_bytes=VMEM_LIMIT_BYTES),
        name="l1_sample_attn",
    )(q3, z3, k_new, v_new, cache_k, cache_v, sinks_col)


def kernel(x_prompt, x_sample, p_prompt, p_sample, state_hgrn, cache_k, cache_v, pre_norm_g, post_norm_g,
           w_in_a, lb_logits, onorm_a, w_out_a, kv_norm_g, w_kv, w_in_b, sinks, w_out_b, w_pe, w_pg):
    bsz, seq, dm = x_prompt.shape
    n_s = x_sample.shape[0]
    assert x_sample.shape[1] == 1 and w_in_a.shape[0] == 1 and w_in_b.shape[0] == 1
    assert seq % 256 == 0 and n_s % 8 == 0
    kv_heads = w_kv.shape[1] // (2 * HEAD_DIM)
    n_q = w_in_b.shape[2] // (2 * HEAD_DIM)
    assert n_q == kv_heads * Q_PER_KV

    row = lambda a: a.reshape(1, -1)
    bf = lambda a: a.astype(BF16)
    w_in_a0, w_out_a0 = bf(w_in_a[0]), bf(w_out_a[0])
    w_in_b0, w_out_b0 = bf(w_in_b[0]), bf(w_out_b[0])
    w_pe_b, w_pg_b = bf(w_pe), bf(w_pg)
    w_kv_b = bf(w_kv)
    dup = lambda w: jnp.concatenate([w.reshape(dm, kv_heads, 1, HEAD_DIM)] * 2, axis=2).reshape(dm, -1)
    kvw = kv_heads * HEAD_DIM
    w_kv_dup = jnp.concatenate([dup(w_kv_b[:, :kvw]), dup(w_kv_b[:, kvw:])], axis=1)
    undup = lambda a: a.reshape(a.shape[0], a.shape[1], kv_heads, 2, HEAD_DIM)[:, :, :, 0, :]

    og0, st_p = _l0_prompt(x_prompt, row(pre_norm_g[0]), w_in_a0, lb_logits, row(onorm_a[0]))
    h1 = _epilogue(og0.reshape(bsz * seq, -1), x_prompt.reshape(bsz * seq, dm), p_prompt[0].reshape(bsz * seq, -1),
                   w_out_a0, row(post_norm_g[0]), w_pe_b[0], w_pg_b[0], tm=512)
    tables_p = _rope_tables(jnp.arange(seq), 2)
    og1, k_p, v_p = _l1_prompt(h1.reshape(bsz, seq, dm), row(pre_norm_g[1]), row(kv_norm_g), w_kv_dup, w_in_b0,
                               sinks[0], tables_p)
    y_p = _epilogue(og1.reshape(bsz * seq, -1), h1, p_prompt[1].reshape(bsz * seq, -1),
                    w_out_b0, row(post_norm_g[1]), w_pe_b[1], w_pg_b[1], tm=512)

    xs = x_sample.reshape(n_s, dm)
    u0 = _norm_matmul(xs, row(pre_norm_g[0]), w_in_a0)
    st_s, og0s = _l0_sample(u0, lb_logits, row(onorm_a[0]), state_hgrn[0])
    h1s = _epilogue(og0s, xs, p_sample[0].reshape(n_s, -1),
                    w_out_a0, row(post_norm_g[0]), w_pe_b[0], w_pg_b[0], tm=n_s)
    tables_s = _rope_tables(PAST_LEN + jnp.arange(1), n_q)
    q_s, z_s, k_s, v_s = _l1_sample_front(h1s, row(pre_norm_g[1]), row(kv_norm_g), w_in_b0, w_kv_b, tables_s)
    wb = cache_k.shape[1]
    og1s = _l1_sample_attn(q_s.reshape(n_s, n_q, HEAD_DIM), z_s.reshape(n_s, n_q, HEAD_DIM),
                           k_s.reshape(n_s, kv_heads, HEAD_DIM), v_s.reshape(n_s, kv_heads, HEAD_DIM),
                           cache_k.reshape(n_s, wb, kvw), cache_v.reshape(n_s, wb, kvw),
                           sinks[0].reshape(n_q, 1))
    y_s = _epilogue(og1s.reshape(n_s, -1), h1s, p_sample[1].reshape(n_s, -1),
                    w_out_b0, row(post_norm_g[1]), w_pe_b[1], w_pg_b[1], tm=n_s)

    return (y_p.reshape(bsz, seq, dm), y_s.reshape(n_s, 1, dm),
            st_p[None], st_s[None],
            undup(k_p), undup(v_p),
            k_s.reshape(n_s, 1, kv_heads, HEAD_DIM), v_s.reshape(n_s, 1, kv_heads, HEAD_DIM))
```

```python
import functools

import jax
import jax.numpy as jnp
from jax import lax
from jax.experimental import pallas as pl
from jax.experimental.pallas import tpu as pltpu

F32 = jnp.float32
BF16 = jnp.bfloat16

EPS = 1e-6
HGRN_HEAD_DIM = 128
HEAD_DIM = 64
Q_PER_KV = 8
WINDOW = 128
ROT_DIM = 16
ROPE_THETA = 500000.0
PAST_LEN = 16384
ATTN_SCALE = HEAD_DIM ** -0.5

CHUNK = 128
LEVEL_HALVES = (64, 32, 16, 8, 4, 2, 1)
MASKED = -1e30
VMEM_LIMIT_BYTES = 56 * 1024 * 1024


def _sigmoid(x):
    return 1.0 / (1.0 + jnp.exp(-x))


def _silu(x):
    return x * _sigmoid(x)


def _rms(x):
    return x * lax.rsqrt(jnp.mean(x * x, axis=-1, keepdims=True) + EPS)


def _dot(a, b):
    return jnp.dot(a, b, preferred_element_type=F32)


def _dot_nt(a, b):
    return lax.dot_general(a, b, (((1,), (1,)), ((), ())), preferred_element_type=F32)


def _dot_tn(a, b):
    return lax.dot_general(a, b, (((0,), (0,)), ((), ())), preferred_element_type=F32)


def _block_diag2(a):
    c = a.shape[1] // 2
    lane = lax.broadcasted_iota(jnp.int32, a.shape, 1)
    zero = jnp.zeros_like(a)
    return jnp.concatenate([jnp.where(lane < c, a, zero), jnp.where(lane >= c, a, zero)], axis=0)


def _rope(x, cos, sin_lo, sin_hi):
    n = x.shape[-1]
    half = ROT_DIM // 2
    return x * cos + pltpu.roll(x, n - half, 1) * sin_lo + pltpu.roll(x, half, 1) * sin_hi


def _rope_tables(pos, reps):
    half = ROT_DIM // 2
    inv = ROPE_THETA ** (-jnp.arange(0, ROT_DIM, 2, dtype=F32) / ROT_DIM)
    ang = pos.astype(F32)[:, None] * inv[None, :]
    cos, sin = jnp.cos(ang), jnp.sin(ang)
    n = pos.shape[0]
    pad = jnp.zeros((n, HEAD_DIM - ROT_DIM), F32)
    zero = jnp.zeros((n, half), F32)
    c = jnp.concatenate([cos, cos, pad + 1.0], axis=1)
    s_lo = jnp.concatenate([-sin, zero, pad], axis=1)
    s_hi = jnp.concatenate([zero, sin, pad], axis=1)
    return tuple(jnp.tile(t, (1, reps)) for t in (c, s_lo, s_hi))


def _lower_bound(lbl):
    m = jnp.max(lbl, axis=0, keepdims=True)
    e = jnp.exp(lbl - m)
    return e[0:1] / jnp.sum(e, axis=0, keepdims=True)


def _level_ref(b, half, row):
    if half >= 8:
        size = 2 * half
        pieces = [jnp.broadcast_to(b[s + half - 1:s + half, :], (size, b.shape[1]))
                  for s in range(0, b.shape[0], size)]
        return pieces[0] if len(pieces) == 1 else jnp.concatenate(pieces, axis=0)
    n = b.shape[0]
    up = lambda x, d: pltpu.roll(x, n - d, 0)
    down = lambda x, d: pltpu.roll(x, d, 0)
    if half == 1:
        return jnp.where(row % 2 == 1, down(b, 1), b)
    y1 = jnp.where(row % 2 == 0, up(b, 1), b)
    if half == 2:
        return jnp.where(row % 4 >= 2, down(y1, 2), y1)
    y2 = jnp.where(row % 4 < 2, up(y1, 2), y1)
    return jnp.where(row % 8 >= 4, down(y2, 4), y2)


def _hgrn_chunk(q, f, v, st_a, st_b, tri):
    c, w = q.shape
    d = w // 2
    g = jnp.log(f)
    k = 1.0 - f
    g_hi = g.astype(BF16)
    r1 = g - g_hi.astype(F32)
    g_mid = r1.astype(BF16)
    g_lo = (r1 - g_mid.astype(F32)).astype(BF16)
    b = _dot(tri, g_hi) + _dot(tri, g_mid) + _dot(tri, g_lo)

    row = lax.broadcasted_iota(jnp.int32, (c, w), 0)
    col_tok = lax.broadcasted_iota(jnp.int32, (c, w), 1) % d
    zero = jnp.zeros((c, w), F32)

    qk = q * k
    diag = jnp.concatenate(
        [jnp.broadcast_to(jnp.sum(qk[:, :d], axis=-1, keepdims=True), (c, d)),
         jnp.broadcast_to(jnp.sum(qk[:, d:], axis=-1, keepdims=True), (c, d))], axis=1)
    a = jnp.where(row == col_tok, diag, zero)
    for half in LEVEL_HALVES:
        upper = row % (2 * half) >= half
        ref = _level_ref(b, half, row)
        e = jnp.exp(jnp.where(upper, b - ref, ref - b))
        qe = jnp.where(upper, q * e, zero).astype(BF16)
        ke = jnp.where(upper, zero, k * e).astype(BF16)
        p = _dot_nt(qe, _block_diag2(ke))
        a = a + jnp.where(row // (2 * half) == col_tok // (2 * half), p, zero)

    st_bd = _block_diag2(jnp.concatenate([st_a, st_b], axis=1).astype(BF16))
    o = _dot_nt((q * jnp.exp(b)).astype(BF16), st_bd) + _dot(a.astype(BF16), _block_diag2(v))

    b_last = b[c - 1:c, :]
    k_hat = (k * jnp.exp(b_last - b)).astype(BF16)
    ds = _dot_tn(v, k_hat)
    decay = jnp.exp(b_last)
    st_a = st_a * decay[:, :d] + ds[:d, :d]
    st_b = st_b * decay[:, d:] + ds[d:, d:]
    return o, st_a, st_b


def _l0_prompt_kernel(x_ref, pg_ref, w_ref, lbl_ref, on_ref, og_ref, st_ref,
                      s_scr, q_scr, f_scr, v_scr, z_scr):
    l = pl.program_id(1)
    tm = x_ref.shape[1]
    width = q_scr.shape[1]
    d = HGRN_HEAD_DIM
    n_heads = width // d

    @pl.when(l == 0)
    def _():
        s_scr[...] = jnp.zeros_like(s_scr)

    xn = (_rms(x_ref[0]) * pg_ref[...]).astype(BF16)
    lb = _lower_bound(lbl_ref[...])
    q_scr[...] = _silu(_dot(xn, w_ref[:, 0:width]))
    f_scr[...] = lb + (1.0 - lb) * _sigmoid(_dot(xn, w_ref[:, width:2 * width]))
    v_scr[...] = _dot(xn, w_ref[:, 2 * width:3 * width]).astype(BF16)
    z_scr[...] = _silu(_dot(xn, w_ref[:, 3 * width:4 * width]))

    ti = lax.broadcasted_iota(jnp.int32, (CHUNK, CHUNK), 0)
    tj = lax.broadcasted_iota(jnp.int32, (CHUNK, CHUNK), 1)
    tri = jnp.where(tj <= ti, 1.0, 0.0).astype(BF16)

    def pair_body(pi, carry):
        col = pl.multiple_of(pi * 2 * d, 2 * d)
        st_a = s_scr[2 * pi]
        st_b = s_scr[2 * pi + 1]
        gain = on_ref[:, pl.ds(col, 2 * d)]
        for c in range(tm // CHUNK):
            rows = slice(c * CHUNK, (c + 1) * CHUNK)
            o, st_a, st_b = _hgrn_chunk(q_scr[rows, pl.ds(col, 2 * d)], f_scr[rows, pl.ds(col, 2 * d)],
                                        v_scr[rows, pl.ds(col, 2 * d)], st_a, st_b, tri)
            on = jnp.concatenate([_rms(o[:, :d]), _rms(o[:, d:])], axis=1) * gain
            og_ref[0, rows, pl.ds(col, 2 * d)] = (on * z_scr[rows, pl.ds(col, 2 * d)]).astype(og_ref.dtype)
        s_scr[2 * pi] = st_a
        s_scr[2 * pi + 1] = st_b
        return carry

    lax.fori_loop(0, n_heads // 2, pair_body, 0)

    @pl.when(l == pl.num_programs(1) - 1)
    def _():
        for h in range(n_heads):
            st_ref[0, h] = s_scr[h].T


def _l0_prompt(x, pre_g, w_in, lb_logits, onorm, tm=256):
    bsz, seq, dm = x.shape
    width = w_in.shape[1] // 4
    n_heads = width // HGRN_HEAD_DIM
    d = HGRN_HEAD_DIM
    const = lambda *shape: pl.BlockSpec(shape, lambda b, l: (0,) * len(shape))
    return pl.pallas_call(
        _l0_prompt_kernel,
        grid=(bsz, seq // tm),
        in_specs=[
            pl.BlockSpec((1, tm, dm), lambda b, l: (b, l, 0)),
            const(1, dm),
            pl.BlockSpec((dm, 4 * width), lambda b, l: (0, 0), pipeline_mode=pl.Buffered(1)),
            const(*lb_logits.shape),
            const(1, width),
        ],
        out_specs=[
            pl.BlockSpec((1, tm, width), lambda b, l: (b, l, 0)),
            pl.BlockSpec((1, n_heads, d, d), lambda b, l: (b, 0, 0, 0)),
        ],
        out_shape=[
            jax.ShapeDtypeStruct((bsz, seq, width), BF16),
            jax.ShapeDtypeStruct((bsz, n_heads, d, d), F32),
        ],
        scratch_shapes=[
            pltpu.VMEM((n_heads, d, d), F32),
            pltpu.VMEM((tm, width), F32),
            pltpu.VMEM((tm, width), F32),
            pltpu.VMEM((tm, width), BF16),
            pltpu.VMEM((tm, width), F32),
        ],
        compiler_params=pltpu.CompilerParams(
            dimension_semantics=("arbitrary", "arbitrary"), vmem_limit_bytes=VMEM_LIMIT_BYTES),
        name="l0_prompt",
    )(x, pre_g, w_in, lb_logits, onorm)


def _epilogue_kernel(og_ref, h_ref, p_ref, wo_ref, pg_ref, wpe_ref, wpg_ref, out_ref):
    mix = _dot(og_ref[...].astype(BF16), wo_ref[...])
    hn = h_ref[...] + _rms(mix) * pg_ref[...]
    pe = _dot(p_ref[...].astype(BF16), wpe_ref[...])
    gate = _sigmoid(_dot(hn.astype(BF16), wpg_ref[...]))
    out_ref[...] = hn + pe * gate


def _epilogue(og, h, p, w_out, post_g, w_pe, w_pg, tm):
    t, dm = h.shape
    width = og.shape[1]
    ple = p.shape[1]
    tm = min(tm, t)
    const = lambda *shape: pl.BlockSpec(shape, lambda i: (0,) * len(shape))
    return pl.pallas_call(
        _epilogue_kernel,
        grid=(t // tm,),
        in_specs=[
            pl.BlockSpec((tm, width), lambda i: (i, 0)),
            pl.BlockSpec((tm, dm), lambda i: (i, 0)),
            pl.BlockSpec((tm, ple), lambda i: (i, 0)),
            const(width, dm), const(1, dm), const(ple, dm), const(dm, dm),
        ],
        out_specs=pl.BlockSpec((tm, dm), lambda i: (i, 0)),
        out_shape=jax.ShapeDtypeStruct((t, dm), F32),
        compiler_params=pltpu.CompilerParams(
            dimension_semantics=("arbitrary",), vmem_limit_bytes=VMEM_LIMIT_BYTES),
        name="epilogue",
    )(og, h, p, w_out, post_g, w_pe, w_pg)


def _l1_prompt_kernel(sinks_ref, h_ref, g1_ref, kvg_ref, wkv_ref, win_ref, cos_ref, slo_ref, shi_ref,
                      og_ref, kc_ref, vc_ref, kx_scr, vx_scr, q_scr, z_scr):
    l = pl.program_id(1)
    tm = h_ref.shape[1]
    width = z_scr.shape[1]
    kvw = kx_scr.shape[1]
    pair_w = 2 * HEAD_DIM
    n_pairs = width // pair_w
    pairs_per_kv = Q_PER_KV // 2

    @pl.when(l == 0)
    def _():
        kx_scr[0:WINDOW, :] = jnp.zeros((WINDOW, kvw), kx_scr.dtype)
        vx_scr[0:WINDOW, :] = jnp.zeros((WINDOW, kvw), vx_scr.dtype)

    @pl.when(l > 0)
    def _():
        kx_scr[0:WINDOW, :] = kx_scr[tm:tm + WINDOW, :]
        vx_scr[0:WINDOW, :] = vx_scr[tm:tm + WINDOW, :]

    h0 = _rms(h_ref[0])
    xn = (h0 * g1_ref[...]).astype(BF16)
    hk = (h0 * kvg_ref[...]).astype(BF16)
    cos, slo, shi = cos_ref[...], slo_ref[...], shi_ref[...]
    tile = lambda t, n: jnp.concatenate([t] * n, axis=1)

    kv = _dot(hk, wkv_ref[...])
    k_rot = _rope(kv[:, :kvw], tile(cos, kvw // pair_w), tile(slo, kvw // pair_w), tile(shi, kvw // pair_w))
    v_new = kv[:, kvw:]
    kx_scr[WINDOW:WINDOW + tm, :] = k_rot.astype(kx_scr.dtype)
    vx_scr[WINDOW:WINDOW + tm, :] = v_new.astype(vx_scr.dtype)

    @pl.when(l == pl.num_programs(1) - 1)
    def _():
        kc_ref[0] = k_rot[tm - WINDOW:, :]
        vc_ref[0] = v_new[tm - WINDOW:, :]

    q = _rope(_dot(xn, win_ref[:, 0:width]), tile(cos, n_pairs), tile(slo, n_pairs), tile(shi, n_pairs))
    q_scr[...] = (q * ATTN_SCALE).astype(q_scr.dtype)
    z_scr[...] = _silu(_dot(xn, win_ref[:, width:2 * width]))

    qi = lax.broadcasted_iota(jnp.int32, (WINDOW, 2 * WINDOW), 0)
    ki = lax.broadcasted_iota(jnp.int32, (WINDOW, 2 * WINDOW), 1)
    band = (ki > qi) & (ki <= qi + WINDOW)
    lane = lax.broadcasted_iota(jnp.int32, (WINDOW, pair_w), 1)

    def group_body(g, carry):
        kcol = pl.multiple_of(g * pair_w, pair_w)
        cols = [pl.multiple_of((g * pairs_per_kv + p) * pair_w, pair_w) for p in range(pairs_per_kv)]
        for c in range(tm // WINDOW):
            rows = slice(c * WINDOW, (c + 1) * WINDOW)
            keys = slice(c * WINDOW, (c + 2) * WINDOW)
            k_bd = _block_diag2(kx_scr[keys, pl.ds(kcol, pair_w)])
            v_bd = _block_diag2(vx_scr[keys, pl.ds(kcol, pair_w)])
            q_all = jnp.concatenate([q_scr[rows, pl.ds(col, pair_w)] for col in cols], axis=0)
            s_all = _dot_nt(q_all, k_bd)
            mask = band & (ki >= jnp.where(l == 0, WINDOW, 0)) if c == 0 else band
            probs, inv = [], []
            for p in range(pairs_per_kv):
                pair_probs = []
                for hh in range(2):
                    sink = sinks_ref[(g * pairs_per_kv + p) * 2 + hh]
                    s = s_all[p * WINDOW:(p + 1) * WINDOW, hh * 2 * WINDOW:(hh + 1) * 2 * WINDOW]
                    s = jnp.where(mask, s, MASKED)
                    mx = jnp.maximum(jnp.max(s, axis=-1, keepdims=True), sink)
                    e = jnp.exp(s - mx)
                    inv.append(1.0 / (jnp.sum(e, axis=-1, keepdims=True) + jnp.exp(sink - mx)))
                    pair_probs.append(e.astype(BF16))
                probs.append(jnp.concatenate(pair_probs, axis=1))
            o_all = _dot(jnp.concatenate(probs, axis=0), v_bd)
            for p, col in enumerate(cols):
                o2 = o_all[p * WINDOW:(p + 1) * WINDOW] * jnp.where(lane < HEAD_DIM, inv[2 * p], inv[2 * p + 1])
                og_ref[0, rows, pl.ds(col, pair_w)] = (o2 * z_scr[rows, pl.ds(col, pair_w)]).astype(og_ref.dtype)
        return carry

    lax.fori_loop(0, n_pairs // pairs_per_kv, group_body, 0)


def _l1_prompt(h, g1, kvg, w_kv_dup, w_in, sinks, tables, tm=256):
    bsz, seq, dm = h.shape
    width = w_in.shape[1] // 2
    kvw = w_kv_dup.shape[1] // 2
    pair_w = 2 * HEAD_DIM
    const = lambda *shape: pl.BlockSpec(shape, lambda b, l: (0,) * len(shape))
    table = pl.BlockSpec((tm, pair_w), lambda b, l: (l, 0))
    return pl.pallas_call(
        _l1_prompt_kernel,
        grid=(bsz, seq // tm),
        in_specs=[
            pl.BlockSpec(memory_space=pltpu.SMEM),
            pl.BlockSpec((1, tm, dm), lambda b, l: (b, l, 0)),
            const(1, dm), const(1, dm), const(dm, 2 * kvw), const(dm, 2 * width),
            table, table, table,
        ],
        out_specs=[
            pl.BlockSpec((1, tm, width), lambda b, l: (b, l, 0)),
            pl.BlockSpec((1, WINDOW, kvw), lambda b, l: (b, 0, 0)),
            pl.BlockSpec((1, WINDOW, kvw), lambda b, l: (b, 0, 0)),
        ],
        out_shape=[
            jax.ShapeDtypeStruct((bsz, seq, width), BF16),
            jax.ShapeDtypeStruct((bsz, WINDOW, kvw), F32),
            jax.ShapeDtypeStruct((bsz, WINDOW, kvw), F32),
        ],
        scratch_shapes=[
            pltpu.VMEM((WINDOW + tm, kvw), BF16),
            pltpu.VMEM((WINDOW + tm, kvw), BF16),
            pltpu.VMEM((tm, width), BF16),
            pltpu.VMEM((tm, width), F32),
        ],
        compiler_params=pltpu.CompilerParams(
            dimension_semantics=("arbitrary", "arbitrary"), vmem_limit_bytes=VMEM_LIMIT_BYTES),
        name="l1_prompt",
    )(sinks, h, g1, kvg, w_kv_dup, w_in, *tables)


def _norm_matmul_kernel(x_ref, g_ref, w_ref, o_ref):
    xn = (_rms(x_ref[...]) * g_ref[...]).astype(BF16)
    o_ref[...] = _dot(xn, w_ref[...])


def _norm_matmul(x, g, w, tn=2048):
    t, dm = x.shape
    n = w.shape[1]
    tn = min(tn, n)
    return pl.pallas_call(
        _norm_matmul_kernel,
        grid=(n // tn,),
        in_specs=[
            pl.BlockSpec((t, dm), lambda j: (0, 0)),
            pl.BlockSpec((1, dm), lambda j: (0, 0)),
            pl.BlockSpec((dm, tn), lambda j: (0, j)),
        ],
        out_specs=pl.BlockSpec((t, tn), lambda j: (0, j)),
        out_shape=jax.ShapeDtypeStruct((t, n), F32),
        compiler_params=pltpu.CompilerParams(
            dimension_semantics=("arbitrary",), vmem_limit_bytes=VMEM_LIMIT_BYTES),
        name="norm_matmul",
    )(x, g, w)


def _l0_sample_kernel(u_ref, lbl_ref, on_ref, s0_ref, s1_ref, og_ref):
    bt = u_ref.shape[0]
    width = og_ref.shape[1]
    d = HGRN_HEAD_DIM
    lb_all = _lower_bound(lbl_ref[...])

    for h in range(width // d):
        cols = slice(h * d, (h + 1) * d)
        lb = lb_all[:, cols]
        q = _silu(u_ref[:, h * d:(h + 1) * d])
        f = lb + (1.0 - lb) * _sigmoid(u_ref[:, width + h * d:width + (h + 1) * d])
        v = u_ref[:, 2 * width + h * d:2 * width + (h + 1) * d]
        z = _silu(u_ref[:, 3 * width + h * d:3 * width + (h + 1) * d])
        f_t = f.T
        q_t = q.T
        outs = []
        for j in range(bt):
            f_col = f_t[:, j:j + 1]
            s_new = f_col * s0_ref[j, h] + (1.0 - f_col) * v[j:j + 1, :]
            s1_ref[j, h] = s_new
            outs.append(jnp.sum(q_t[:, j:j + 1] * s_new, axis=0, keepdims=True))
        o = jnp.concatenate(outs, axis=0)
        og_ref[:, cols] = _rms(o) * on_ref[:, cols] * z


def _l0_sample(u, lb_logits, onorm, state, bt=8):
    n, heads, d, _ = state.shape
    width = heads * d
    return pl.pallas_call(
        _l0_sample_kernel,
        grid=(n // bt,),
        in_specs=[
            pl.BlockSpec((bt, 4 * width), lambda i: (i, 0)),
            pl.BlockSpec(lb_logits.shape, lambda i: (0, 0)),
            pl.BlockSpec((1, width), lambda i: (0, 0)),
            pl.BlockSpec((bt, heads, d, d), lambda i: (i, 0, 0, 0)),
        ],
        out_specs=[
            pl.BlockSpec((bt, heads, d, d), lambda i: (i, 0, 0, 0)),
            pl.BlockSpec((bt, width), lambda i: (i, 0)),
        ],
        out_shape=[
            jax.ShapeDtypeStruct(state.shape, F32),
            jax.ShapeDtypeStruct((n, width), F32),
        ],
        compiler_params=pltpu.CompilerParams(
            dimension_semantics=("arbitrary",), vmem_limit_bytes=VMEM_LIMIT_BYTES),
        name="l0_sample",
    )(u, lb_logits, onorm, state)


def _l1_sample_front_kernel(h_ref, g1_ref, kvg_ref, win_ref, wkv_ref, cos_ref, slo_ref, shi_ref,
                            q_ref, z_ref, k_ref, v_ref):
    width = q_ref.shape[1]
    kvw = k_ref.shape[1]
    h0 = _rms(h_ref[...])
    xn = (h0 * g1_ref[...]).astype(BF16)
    hk = (h0 * kvg_ref[...]).astype(BF16)
    cos, slo, shi = cos_ref[...], slo_ref[...], shi_ref[...]
    q = _rope(_dot(xn, win_ref[:, 0:width]), cos, slo, shi)
    q_ref[...] = q * ATTN_SCALE
    z_ref[...] = _silu(_dot(xn, win_ref[:, width:2 * width]))
    kv = _dot(hk, wkv_ref[...])
    k_ref[...] = _rope(kv[:, :kvw], cos[:, :kvw], slo[:, :kvw], shi[:, :kvw])
    v_ref[...] = kv[:, kvw:]


def _l1_sample_front(h, g1, kvg, w_in, w_kv, tables):
    t, dm = h.shape
    width = w_in.shape[1] // 2
    kvw = w_kv.shape[1] // 2
    full = lambda a: pl.BlockSpec(a.shape, lambda i: (0,) * a.ndim)
    args = (h, g1, kvg, w_in, w_kv, *tables)
    return pl.pallas_call(
        _l1_sample_front_kernel,
        grid=(1,),
        in_specs=[full(a) for a in args],
        out_specs=[pl.BlockSpec((t, width), lambda i: (0, 0)), pl.BlockSpec((t, width), lambda i: (0, 0)),
                   pl.BlockSpec((t, kvw), lambda i: (0, 0)), pl.BlockSpec((t, kvw), lambda i: (0, 0))],
        out_shape=[jax.ShapeDtypeStruct((t, width), F32), jax.ShapeDtypeStruct((t, width), F32),
                   jax.ShapeDtypeStruct((t, kvw), F32), jax.ShapeDtypeStruct((t, kvw), F32)],
        compiler_params=pltpu.CompilerParams(
            dimension_semantics=("arbitrary",), vmem_limit_bytes=VMEM_LIMIT_BYTES),
        name="l1_sample_front",
    )(*args)


def _l1_sample_attn_kernel(q_ref, z_ref, kn_ref, vn_ref, ck_ref, cv_ref, sink_ref, og_ref):
    bt, wb = ck_ref.shape[0], ck_ref.shape[1]
    kv_heads = kn_ref.shape[1]
    valid = lax.broadcasted_iota(jnp.int32, (Q_PER_KV, wb), 1) > wb - WINDOW
    for j in range(bt):
        for g in range(kv_heads):
            heads = slice(g * Q_PER_KV, (g + 1) * Q_PER_KV)
            lanes = slice(g * HEAD_DIM, (g + 1) * HEAD_DIM)
            q = q_ref[j, heads, :]
            sink = sink_ref[heads, :]
            s = jnp.where(valid, _dot_nt(q.astype(BF16), ck_ref[j, :, lanes].astype(BF16)), MASKED)
            s_new = jnp.sum(q * kn_ref[j, g:g + 1, :], axis=-1, keepdims=True)
            mx = jnp.maximum(jnp.maximum(jnp.max(s, axis=-1, keepdims=True), s_new), sink)
            e = jnp.exp(s - mx)
            e_new = jnp.exp(s_new - mx)
            den = jnp.sum(e, axis=-1, keepdims=True) + e_new + jnp.exp(sink - mx)
            o = _dot(e.astype(BF16), cv_ref[j, :, lanes].astype(BF16)) + e_new * vn_ref[j, g:g + 1, :]
            og_ref[j, heads, :] = o / den * z_ref[j, heads, :]


def _l1_sample_attn(q3, z3, k_new, v_new, cache_k, cache_v, sinks_col, bt=8):
    n, heads, hd = q3.shape
    wb, kvw = cache_k.shape[1], cache_k.shape[2]
    kv_heads = k_new.shape[1]
    qspec = pl.BlockSpec((bt, heads, hd), lambda i: (i, 0, 0))
    nspec = pl.BlockSpec((bt, kv_heads, hd), lambda i: (i, 0, 0))
    cspec = pl.BlockSpec((bt, wb, kvw), lambda i: (i, 0, 0))
    return pl.pallas_call(
        _l1_sample_attn_kernel,
        grid=(n // bt,),
        in_specs=[qspec, qspec, nspec, nspec, cspec, cspec, pl.BlockSpec((heads, 1), lambda i: (0, 0))],
        out_specs=qspec,
        out_shape=jax.ShapeDtypeStruct((n, heads, hd), F32),
        compiler_params=pltpu.CompilerParams(
            dimension_semantics=("arbitrary",), vmem_limit_bytes=VMEM_LIMIT_BYTES),
        name="l1_sample_attn",
    )(q3, z3, k_new, v_new, cache_k, cache_v, sinks_col)


def kernel(x_prompt, x_sample, p_prompt, p_sample, state_hgrn, cache_k, cache_v, pre_norm_g, post_norm_g,
           w_in_a, lb_logits, onorm_a, w_out_a, kv_norm_g, w_kv, w_in_b, sinks, w_out_b, w_pe, w_pg):
    bsz, seq, dm = x_prompt.shape
    n_s = x_sample.shape[0]
    assert x_sample.shape[1] == 1 and w_in_a.shape[0] == 1 and w_in_b.shape[0] == 1
    assert seq % 256 == 0 and n_s % 8 == 0
    kv_heads = w_kv.shape[1] // (2 * HEAD_DIM)
    n_q = w_in_b.shape[2] // (2 * HEAD_DIM)
    assert n_q == kv_heads * Q_PER_KV

    row = lambda a: a.reshape(1, -1)
    bf = lambda a: a.astype(BF16)
    w_in_a0, w_out_a0 = bf(w_in_a[0]), bf(w_out_a[0])
    w_in_b0, w_out_b0 = bf(w_in_b[0]), bf(w_out_b[0])
    w_pe_b, w_pg_b = bf(w_pe), bf(w_pg)
    w_kv_b = bf(w_kv)
    dup = lambda w: jnp.concatenate([w.reshape(dm, kv_heads, 1, HEAD_DIM)] * 2, axis=2).reshape(dm, -1)
    kvw = kv_heads * HEAD_DIM
    w_kv_dup = jnp.concatenate([dup(w_kv_b[:, :kvw]), dup(w_kv_b[:, kvw:])], axis=1)
    undup = lambda a: a.reshape(a.shape[0], a.shape[1], kv_heads, 2, HEAD_DIM)[:, :, :, 0, :]

    og0, st_p = _l0_prompt(x_prompt, row(pre_norm_g[0]), w_in_a0, lb_logits, row(onorm_a[0]))
    h1 = _epilogue(og0.reshape(bsz * seq, -1), x_prompt.reshape(bsz * seq, dm), p_prompt[0].reshape(bsz * seq, -1),
                   w_out_a0, row(post_norm_g[0]), w_pe_b[0], w_pg_b[0], tm=512)
    tables_p = _rope_tables(jnp.arange(seq), 2)
    og1, k_p, v_p = _l1_prompt(h1.reshape(bsz, seq, dm), row(pre_norm_g[1]), row(kv_norm_g), w_kv_dup, w_in_b0,
                               sinks[0], tables_p)
    y_p = _epilogue(og1.reshape(bsz * seq, -1), h1, p_prompt[1].reshape(bsz * seq, -1),
                    w_out_b0, row(post_norm_g[1]), w_pe_b[1], w_pg_b[1], tm=512)

    xs = x_sample.reshape(n_s, dm)
    u0 = _norm_matmul(xs, row(pre_norm_g[0]), w_in_a0)
    st_s, og0s = _l0_sample(u0, lb_logits, row(onorm_a[0]), state_hgrn[0])
    h1s = _epilogue(og0s, xs, p_sample[0].reshape(n_s, -1),
                    w_out_a0, row(post_norm_g[0]), w_pe_b[0], w_pg_b[0], tm=n_s)
    tables_s = _rope_tables(PAST_LEN + jnp.arange(1), n_q)
    q_s, z_s, k_s, v_s = _l1_sample_front(h1s, row(pre_norm_g[1]), row(kv_norm_g), w_in_b0, w_kv_b, tables_s)
    wb = cache_k.shape[1]
    og1s = _l1_sample_attn(q_s.reshape(n_s, n_q, HEAD_DIM), z_s.reshape(n_s, n_q, HEAD_DIM),
                           k_s.reshape(n_s, kv_heads, HEAD_DIM), v_s.reshape(n_s, kv_heads, HEAD_DIM),
                           cache_k.reshape(n_s, wb, kvw), cache_v.reshape(n_s, wb, kvw),
                           sinks[0].reshape(n_q, 1))
    y_s = _epilogue(og1s.reshape(n_s, -1), h1s, p_sample[1].reshape(n_s, -1),
                    w_out_b0, row(post_norm_g[1]), w_pe_b[1], w_pg_b[1], tm=n_s)

    return (y_p.reshape(bsz, seq, dm), y_s.reshape(n_s, 1, dm),
            st_p[None], st_s[None],
            undup(k_p), undup(v_p),
            k_s.reshape(n_s, 1, kv_heads, HEAD_DIM), v_s.reshape(n_s, 1, kv_heads, HEAD_DIM))
```

```python
import functools

import jax
import jax.numpy as jnp
from jax import lax
from jax.experimental import pallas as pl
from jax.experimental.pallas import tpu as pltpu

F32 = jnp.float32
BF16 = jnp.bfloat16

EPS = 1e-6
HGRN_HEAD_DIM = 128
HEAD_DIM = 64
Q_PER_KV = 8
WINDOW = 128
ROT_DIM = 16
ROPE_THETA = 500000.0
PAST_LEN = 16384
ATTN_SCALE = HEAD_DIM ** -0.5

CHUNK = 128
LEVEL_HALVES = (64, 32, 16, 8, 4, 2, 1)
FAST_TAIL = 32
FAST_MAX_EXPONENT = 40.0
MASKED = -1e30
VMEM_LIMIT_BYTES = 56 * 1024 * 1024


def _sigmoid(x):
    return 1.0 / (1.0 + jnp.exp(-x))


def _silu(x):
    return x * _sigmoid(x)


def _rms(x):
    return x * lax.rsqrt(jnp.mean(x * x, axis=-1, keepdims=True) + EPS)


def _dot(a, b):
    return jnp.dot(a, b, preferred_element_type=F32)


def _dot_nt(a, b):
    return lax.dot_general(a, b, (((1,), (1,)), ((), ())), preferred_element_type=F32)


def _dot_tn(a, b):
    return lax.dot_general(a, b, (((0,), (0,)), ((), ())), preferred_element_type=F32)


def _block_diag2(a):
    c = a.shape[1] // 2
    lane = lax.broadcasted_iota(jnp.int32, a.shape, 1)
    zero = jnp.zeros_like(a)
    return jnp.concatenate([jnp.where(lane < c, a, zero), jnp.where(lane >= c, a, zero)], axis=0)


def _rope(x, cos, sin_lo, sin_hi):
    n = x.shape[-1]
    half = ROT_DIM // 2
    return x * cos + pltpu.roll(x, n - half, 1) * sin_lo + pltpu.roll(x, half, 1) * sin_hi


def _rope_tables(pos, reps):
    half = ROT_DIM // 2
    inv = ROPE_THETA ** (-jnp.arange(0, ROT_DIM, 2, dtype=F32) / ROT_DIM)
    ang = pos.astype(F32)[:, None] * inv[None, :]
    cos, sin = jnp.cos(ang), jnp.sin(ang)
    n = pos.shape[0]
    pad = jnp.zeros((n, HEAD_DIM - ROT_DIM), F32)
    zero = jnp.zeros((n, half), F32)
    c = jnp.concatenate([cos, cos, pad + 1.0], axis=1)
    s_lo = jnp.concatenate([-sin, zero, pad], axis=1)
    s_hi = jnp.concatenate([zero, sin, pad], axis=1)
    return tuple(jnp.tile(t, (1, reps)) for t in (c, s_lo, s_hi))


def _lower_bound(lbl):
    m = jnp.max(lbl, axis=0, keepdims=True)
    e = jnp.exp(lbl - m)
    return e[0:1] / jnp.sum(e, axis=0, keepdims=True)


def _level_ref(b, half, row):
    if half >= 8:
        return _block_rows(b, half - 1, 2 * half)
    n = b.shape[0]
    up = lambda x, d: pltpu.roll(x, n - d, 0)
    down = lambda x, d: pltpu.roll(x, d, 0)
    if half == 1:
        return jnp.where(row % 2 == 1, down(b, 1), b)
    y1 = jnp.where(row % 2 == 0, up(b, 1), b)
    if half == 2:
        return jnp.where(row % 4 >= 2, down(y1, 2), y1)
    y2 = jnp.where(row % 4 < 2, up(y1, 2), y1)
    return jnp.where(row % 8 >= 4, down(y2, 4), y2)


def _block_rows(b, offset, size):
    pieces = [jnp.broadcast_to(b[s + offset:s + offset + 1, :], (size, b.shape[1]))
              for s in range(0, b.shape[0], size)]
    return pieces[0] if len(pieces) == 1 else jnp.concatenate(pieces, axis=0)


def _hgrn_chunk(q, k, b, v, st_a, st_b, halves, tail):
    c, w = q.shape
    d = w // 2
    row = lax.broadcasted_iota(jnp.int32, (c, w), 0)
    col_tok = lax.broadcasted_iota(jnp.int32, (c, w), 1) % d
    zero = jnp.zeros((c, w), F32)

    qk = q * k
    diag = jnp.concatenate(
        [jnp.broadcast_to(jnp.sum(qk[:, :d], axis=-1, keepdims=True), (c, d)),
         jnp.broadcast_to(jnp.sum(qk[:, d:], axis=-1, keepdims=True), (c, d))], axis=1)
    a = jnp.where(row == col_tok, diag, zero)
    for half in halves:
        upper = row % (2 * half) >= half
        ref = _level_ref(b, half, row)
        e = jnp.exp(jnp.where(upper, b - ref, ref - b))
        qe = jnp.where(upper, q * e, zero).astype(BF16)
        ke = jnp.where(upper, zero, k * e).astype(BF16)
        p = _dot_nt(qe, _block_diag2(ke))
        a = a + jnp.where(row // (2 * half) == col_tok // (2 * half), p, zero)
    if tail is not None:
        ref = _block_rows(b, tail // 2 - 1, tail)
        qe = (q * jnp.exp(b - ref)).astype(BF16)
        ke = (k * jnp.exp(ref - b)).astype(BF16)
        p = _dot_nt(qe, _block_diag2(ke))
        a = a + jnp.where((row // tail == col_tok // tail) & (row > col_tok), p, zero)

    st_bd = _block_diag2(jnp.concatenate([st_a, st_b], axis=1).astype(BF16))
    o = _dot_nt((q * jnp.exp(b)).astype(BF16), st_bd) + _dot(a.astype(BF16), _block_diag2(v))

    b_last = b[c - 1:c, :]
    k_hat = (k * jnp.exp(b_last - b)).astype(BF16)
    ds = _dot_tn(v, k_hat)
    decay = jnp.exp(b_last)
    st_a = st_a * decay[:, :d] + ds[:d, :d]
    st_b = st_b * decay[:, d:] + ds[d:, d:]
    return o, st_a, st_b


def _l0_prompt_kernel(x_ref, pg_ref, w_ref, lbl_ref, on_ref, og_ref, st_ref,
                      s_scr, q_scr, k_scr, b_scr, v_scr, z_scr):
    l = pl.program_id(1)
    tm = x_ref.shape[1]
    width = q_scr.shape[1]
    d = HGRN_HEAD_DIM
    n_heads = width // d

    @pl.when(l == 0)
    def _():
        s_scr[...] = jnp.zeros_like(s_scr)

    xn = (_rms(x_ref[0]) * pg_ref[...]).astype(BF16)
    lb = _lower_bound(lbl_ref[...])
    q_scr[...] = _silu(_dot(xn, w_ref[:, 0:width]))
    f = lb + (1.0 - lb) * _sigmoid(_dot(xn, w_ref[:, width:2 * width]))
    k_scr[...] = 1.0 - f
    v_scr[...] = _dot(xn, w_ref[:, 2 * width:3 * width]).astype(BF16)
    z_scr[...] = _silu(_dot(xn, w_ref[:, 3 * width:4 * width]))

    g = jnp.log(f)
    g_hi = g.astype(BF16)
    r1 = g - g_hi.astype(F32)
    g_mid = r1.astype(BF16)
    g_lo = (r1 - g_mid.astype(F32)).astype(BF16)
    ti = lax.broadcasted_iota(jnp.int32, (CHUNK, CHUNK), 0)
    tj = lax.broadcasted_iota(jnp.int32, (CHUNK, CHUNK), 1)
    tri = jnp.where(tj <= ti, 1.0, 0.0).astype(BF16)
    worst = jnp.zeros((1, width), F32)
    for c in range(tm // CHUNK):
        rows = slice(c * CHUNK, (c + 1) * CHUNK)
        b = _dot(tri, g_hi[rows]) + _dot(tri, g_mid[rows]) + _dot(tri, g_lo[rows])
        b_scr[rows, :] = b
        for s in range(0, CHUNK, FAST_TAIL):
            mid = b[s + FAST_TAIL // 2 - 1:s + FAST_TAIL // 2, :]
            worst = jnp.maximum(worst, jnp.maximum(b[s:s + 1, :] - mid, mid - b[s + FAST_TAIL - 1:s + FAST_TAIL, :]))
    fast = jnp.max(worst) <= FAST_MAX_EXPONENT

    def scan(halves, tail, unroll):
        def pair_body(pi, carry):
            col = pl.multiple_of(pi * 2 * d, 2 * d)
            st_a = s_scr[2 * pi]
            st_b = s_scr[2 * pi + 1]
            gain = on_ref[:, pl.ds(col, 2 * d)]
            for c in range(tm // CHUNK):
                rows = slice(c * CHUNK, (c + 1) * CHUNK)
                o, st_a, st_b = _hgrn_chunk(
                    q_scr[rows, pl.ds(col, 2 * d)], k_scr[rows, pl.ds(col, 2 * d)], b_scr[rows, pl.ds(col, 2 * d)],
                    v_scr[rows, pl.ds(col, 2 * d)], st_a, st_b, halves, tail)
                on = jnp.concatenate([_rms(o[:, :d]), _rms(o[:, d:])], axis=1) * gain
                og_ref[0, rows, pl.ds(col, 2 * d)] = (on * z_scr[rows, pl.ds(col, 2 * d)]).astype(og_ref.dtype)
            s_scr[2 * pi] = st_a
            s_scr[2 * pi + 1] = st_b
            return carry

        lax.fori_loop(0, n_heads // 2, pair_body, 0, unroll=unroll)

    @pl.when(fast)
    def _():
        scan(tuple(h for h in LEVEL_HALVES if h >= FAST_TAIL), FAST_TAIL, unroll=4)

    @pl.when(jnp.logical_not(fast))
    def _():
        scan(LEVEL_HALVES, None, unroll=1)

    @pl.when(l == pl.num_programs(1) - 1)
    def _():
        for h in range(n_heads):
            st_ref[0, h] = s_scr[h].T


def _l0_prompt(x, pre_g, w_in, lb_logits, onorm, tm=256):
    bsz, seq, dm = x.shape
    width = w_in.shape[1] // 4
    n_heads = width // HGRN_HEAD_DIM
    d = HGRN_HEAD_DIM
    const = lambda *shape: pl.BlockSpec(shape, lambda b, l: (0,) * len(shape))
    return pl.pallas_call(
        _l0_prompt_kernel,
        grid=(bsz, seq // tm),
        in_specs=[
            pl.BlockSpec((1, tm, dm), lambda b, l: (b, l, 0)),
            const(1, dm),
            pl.BlockSpec((dm, 4 * width), lambda b, l: (0, 0), pipeline_mode=pl.Buffered(1)),
            const(*lb_logits.shape),
            const(1, width),
        ],
        out_specs=[
            pl.BlockSpec((1, tm, width), lambda b, l: (b, l, 0)),
            pl.BlockSpec((1, n_heads, d, d), lambda b, l: (b, 0, 0, 0)),
        ],
        out_shape=[
            jax.ShapeDtypeStruct((bsz, seq, width), BF16),
            jax.ShapeDtypeStruct((bsz, n_heads, d, d), F32),
        ],
        scratch_shapes=[
            pltpu.VMEM((n_heads, d, d), F32),
            pltpu.VMEM((tm, width), F32),
            pltpu.VMEM((tm, width), F32),
            pltpu.VMEM((tm, width), F32),
            pltpu.VMEM((tm, width), BF16),
            pltpu.VMEM((tm, width), F32),
        ],
        compiler_params=pltpu.CompilerParams(
            dimension_semantics=("arbitrary", "arbitrary"), vmem_limit_bytes=VMEM_LIMIT_BYTES),
        name="l0_prompt",
    )(x, pre_g, w_in, lb_logits, onorm)


def _epilogue_kernel(og_ref, h_ref, p_ref, wo_ref, pg_ref, wpe_ref, wpg_ref, out_ref):
    mix = _dot(og_ref[...].astype(BF16), wo_ref[...])
    hn = h_ref[...] + _rms(mix) * pg_ref[...]
    pe = _dot(p_ref[...].astype(BF16), wpe_ref[...])
    gate = _sigmoid(_dot(hn.astype(BF16), wpg_ref[...]))
    out_ref[...] = hn + pe * gate


def _epilogue(og, h, p_all, layer, w_out, post_g, w_pe, w_pg, tm):
    t, dm = h.shape
    width = og.shape[1]
    ple = p_all.shape[2]
    tm = min(tm, t)
    const = lambda *shape: pl.BlockSpec(shape, lambda i: (0,) * len(shape))
    return pl.pallas_call(
        _epilogue_kernel,
        grid=(t // tm,),
        in_specs=[
            pl.BlockSpec((tm, width), lambda i: (i, 0)),
            pl.BlockSpec((tm, dm), lambda i: (i, 0)),
            pl.BlockSpec((None, tm, ple), lambda i: (layer, i, 0)),
            const(width, dm), const(1, dm), const(ple, dm), const(dm, dm),
        ],
        out_specs=pl.BlockSpec((tm, dm), lambda i: (i, 0)),
        out_shape=jax.ShapeDtypeStruct((t, dm), F32),
        compiler_params=pltpu.CompilerParams(
            dimension_semantics=("arbitrary",), vmem_limit_bytes=VMEM_LIMIT_BYTES),
        name="epilogue",
    )(og, h, p_all, w_out, post_g, w_pe, w_pg)


def _l1_prompt_kernel(sinks_ref, h_ref, g1_ref, kvg_ref, wkv_ref, win_ref, cos_ref, slo_ref, shi_ref,
                      og_ref, kc_ref, vc_ref, kx_scr, vx_scr, q_scr, z_scr):
    l = pl.program_id(1)
    tm = h_ref.shape[1]
    width = z_scr.shape[1]
    kvw = kx_scr.shape[1]
    pair_w = 2 * HEAD_DIM
    n_pairs = width // pair_w
    pairs_per_kv = Q_PER_KV // 2

    @pl.when(l == 0)
    def _():
        kx_scr[0:WINDOW, :] = jnp.zeros((WINDOW, kvw), kx_scr.dtype)
        vx_scr[0:WINDOW, :] = jnp.zeros((WINDOW, kvw), vx_scr.dtype)

    @pl.when(l > 0)
    def _():
        kx_scr[0:WINDOW, :] = kx_scr[tm:tm + WINDOW, :]
        vx_scr[0:WINDOW, :] = vx_scr[tm:tm + WINDOW, :]

    h0 = _rms(h_ref[0])
    xn = (h0 * g1_ref[...]).astype(BF16)
    hk = (h0 * kvg_ref[...]).astype(BF16)
    cos, slo, shi = cos_ref[...], slo_ref[...], shi_ref[...]
    tile = lambda t, n: jnp.concatenate([t] * n, axis=1)

    kv = _dot(hk, wkv_ref[...])
    k_rot = _rope(kv[:, :kvw], tile(cos, kvw // pair_w), tile(slo, kvw // pair_w), tile(shi, kvw // pair_w))
    v_new = kv[:, kvw:]
    kx_scr[WINDOW:WINDOW + tm, :] = k_rot.astype(kx_scr.dtype)
    vx_scr[WINDOW:WINDOW + tm, :] = v_new.astype(vx_scr.dtype)

    @pl.when(l == pl.num_programs(1) - 1)
    def _():
        kc_ref[0] = k_rot[tm - WINDOW:, :]
        vc_ref[0] = v_new[tm - WINDOW:, :]

    q = _rope(_dot(xn, win_ref[:, 0:width]), tile(cos, n_pairs), tile(slo, n_pairs), tile(shi, n_pairs))
    q_scr[...] = (q * ATTN_SCALE).astype(q_scr.dtype)
    z_scr[...] = _silu(_dot(xn, win_ref[:, width:2 * width]))

    qi = lax.broadcasted_iota(jnp.int32, (WINDOW, 2 * WINDOW), 0)
    ki = lax.broadcasted_iota(jnp.int32, (WINDOW, 2 * WINDOW), 1)
    band = (ki > qi) & (ki <= qi + WINDOW)
    lane = lax.broadcasted_iota(jnp.int32, (WINDOW, pair_w), 1)

    def group_body(g, carry):
        kcol = pl.multiple_of(g * pair_w, pair_w)
        cols = [pl.multiple_of((g * pairs_per_kv + p) * pair_w, pair_w) for p in range(pairs_per_kv)]
        for c in range(tm // WINDOW):
            rows = slice(c * WINDOW, (c + 1) * WINDOW)
            keys = slice(c * WINDOW, (c + 2) * WINDOW)
            k_bd = _block_diag2(kx_scr[keys, pl.ds(kcol, pair_w)])
            v_bd = _block_diag2(vx_scr[keys, pl.ds(kcol, pair_w)])
            q_all = jnp.concatenate([q_scr[rows, pl.ds(col, pair_w)] for col in cols], axis=0)
            s_all = _dot_nt(q_all, k_bd)
            mask = band & (ki >= jnp.where(l == 0, WINDOW, 0)) if c == 0 else band
            probs, inv = [], []
            for p in range(pairs_per_kv):
                pair_probs = []
                for hh in range(2):
                    sink = sinks_ref[(g * pairs_per_kv + p) * 2 + hh]
                    s = s_all[p * WINDOW:(p + 1) * WINDOW, hh * 2 * WINDOW:(hh + 1) * 2 * WINDOW]
                    s = jnp.where(mask, s, MASKED)
                    mx = jnp.maximum(jnp.max(s, axis=-1, keepdims=True), sink)
                    e = jnp.exp(s - mx)
                    inv.append(1.0 / (jnp.sum(e, axis=-1, keepdims=True) + jnp.exp(sink - mx)))
                    pair_probs.append(e.astype(BF16))
                probs.append(jnp.concatenate(pair_probs, axis=1))
            o_all = _dot(jnp.concatenate(probs, axis=0), v_bd)
            for p, col in enumerate(cols):
                o2 = o_all[p * WINDOW:(p + 1) * WINDOW] * jnp.where(lane < HEAD_DIM, inv[2 * p], inv[2 * p + 1])
                og_ref[0, rows, pl.ds(col, pair_w)] = (o2 * z_scr[rows, pl.ds(col, pair_w)]).astype(og_ref.dtype)
        return carry

    lax.fori_loop(0, n_pairs // pairs_per_kv, group_body, 0)


def _l1_prompt(h, g1, kvg, w_kv_dup, w_in, sinks, tables, tm=256):
    bsz, seq, dm = h.shape
    width = w_in.shape[1] // 2
    kvw = w_kv_dup.shape[1] // 2
    pair_w = 2 * HEAD_DIM
    const = lambda *shape: pl.BlockSpec(shape, lambda b, l: (0,) * len(shape))
    table = pl.BlockSpec((tm, pair_w), lambda b, l: (l, 0))
    return pl.pallas_call(
        _l1_prompt_kernel,
        grid=(bsz, seq // tm),
        in_specs=[
            pl.BlockSpec(memory_space=pltpu.SMEM),
            pl.BlockSpec((1, tm, dm), lambda b, l: (b, l, 0)),
            const(1, dm), const(1, dm), const(dm, 2 * kvw), const(dm, 2 * width),
            table, table, table,
        ],
        out_specs=[
            pl.BlockSpec((1, tm, width), lambda b, l: (b, l, 0)),
            pl.BlockSpec((1, WINDOW, kvw), lambda b, l: (b, 0, 0)),
            pl.BlockSpec((1, WINDOW, kvw), lambda b, l: (b, 0, 0)),
        ],
        out_shape=[
            jax.ShapeDtypeStruct((bsz, seq, width), BF16),
            jax.ShapeDtypeStruct((bsz, WINDOW, kvw), F32),
            jax.ShapeDtypeStruct((bsz, WINDOW, kvw), F32),
        ],
        scratch_shapes=[
            pltpu.VMEM((WINDOW + tm, kvw), BF16),
            pltpu.VMEM((WINDOW + tm, kvw), BF16),
            pltpu.VMEM((tm, width), BF16),
            pltpu.VMEM((tm, width), F32),
        ],
        compiler_params=pltpu.CompilerParams(
            dimension_semantics=("arbitrary", "arbitrary"), vmem_limit_bytes=VMEM_LIMIT_BYTES),
        name="l1_prompt",
    )(sinks, h, g1, kvg, w_kv_dup, w_in, *tables)


def _norm_matmul_kernel(x_ref, g_ref, w_ref, o_ref):
    xn = (_rms(x_ref[...]) * g_ref[...]).astype(BF16)
    o_ref[...] = _dot(xn, w_ref[...])


def _norm_matmul(x, g, w, tn=2048):
    t, dm = x.shape
    n = w.shape[1]
    tn = min(tn, n)
    return pl.pallas_call(
        _norm_matmul_kernel,
        grid=(n // tn,),
        in_specs=[
            pl.BlockSpec((t, dm), lambda j: (0, 0)),
            pl.BlockSpec((1, dm), lambda j: (0, 0)),
            pl.BlockSpec((dm, tn), lambda j: (0, j)),
        ],
        out_specs=pl.BlockSpec((t, tn), lambda j: (0, j)),
        out_shape=jax.ShapeDtypeStruct((t, n), F32),
        compiler_params=pltpu.CompilerParams(
            dimension_semantics=("arbitrary",), vmem_limit_bytes=VMEM_LIMIT_BYTES),
        name="norm_matmul",
    )(x, g, w)


def _l0_sample_kernel(u_ref, lbl_ref, on_ref, s0_ref, s1_ref, og_ref):
    bt = u_ref.shape[0]
    width = og_ref.shape[1]
    d = HGRN_HEAD_DIM
    lb_all = _lower_bound(lbl_ref[...])

    for h in range(width // d):
        cols = slice(h * d, (h + 1) * d)
        lb = lb_all[:, cols]
        q = _silu(u_ref[:, h * d:(h + 1) * d])
        f = lb + (1.0 - lb) * _sigmoid(u_ref[:, width + h * d:width + (h + 1) * d])
        v = u_ref[:, 2 * width + h * d:2 * width + (h + 1) * d]
        z = _silu(u_ref[:, 3 * width + h * d:3 * width + (h + 1) * d])
        f_t = f.T
        q_b = q.astype(BF16)
        outs = []
        for j in range(bt):
            v_row = v[j:j + 1, :]
            s_new = v_row + f_t[:, j:j + 1] * (s0_ref[j, h] - v_row)
            s1_ref[j, h] = s_new
            outs.append(_dot(q_b[j:j + 1, :], s_new.astype(BF16)))
        o = jnp.concatenate(outs, axis=0)
        og_ref[:, cols] = _rms(o) * on_ref[:, cols] * z


def _l0_sample(u, lb_logits, onorm, state, bt=8):
    n, heads, d, _ = state.shape
    width = heads * d
    return pl.pallas_call(
        _l0_sample_kernel,
        grid=(n // bt,),
        in_specs=[
            pl.BlockSpec((bt, 4 * width), lambda i: (i, 0)),
            pl.BlockSpec(lb_logits.shape, lambda i: (0, 0)),
            pl.BlockSpec((1, width), lambda i: (0, 0)),
            pl.BlockSpec((bt, heads, d, d), lambda i: (i, 0, 0, 0)),
        ],
        out_specs=[
            pl.BlockSpec((bt, heads, d, d), lambda i: (i, 0, 0, 0)),
            pl.BlockSpec((bt, width), lambda i: (i, 0)),
        ],
        out_shape=[
            jax.ShapeDtypeStruct(state.shape, F32),
            jax.ShapeDtypeStruct((n, width), F32),
        ],
        compiler_params=pltpu.CompilerParams(
            dimension_semantics=("arbitrary",), vmem_limit_bytes=VMEM_LIMIT_BYTES),
        name="l0_sample",
    )(u, lb_logits, onorm, state)


def _l1_sample_front_kernel(h_ref, g1_ref, kvg_ref, win_ref, wkv_ref, cos_ref, slo_ref, shi_ref,
                            q_ref, z_ref, k_ref, v_ref):
    width = q_ref.shape[1]
    kvw = k_ref.shape[1]
    h0 = _rms(h_ref[...])
    xn = (h0 * g1_ref[...]).astype(BF16)
    hk = (h0 * kvg_ref[...]).astype(BF16)
    cos, slo, shi = cos_ref[...], slo_ref[...], shi_ref[...]
    q = _rope(_dot(xn, win_ref[:, 0:width]), cos, slo, shi)
    q_ref[...] = q * ATTN_SCALE
    z_ref[...] = _silu(_dot(xn, win_ref[:, width:2 * width]))
    kv = _dot(hk, wkv_ref[...])
    k_ref[...] = _rope(kv[:, :kvw], cos[:, :kvw], slo[:, :kvw], shi[:, :kvw])
    v_ref[...] = kv[:, kvw:]


def _l1_sample_front(h, g1, kvg, w_in, w_kv, tables):
    t, dm = h.shape
    width = w_in.shape[1] // 2
    kvw = w_kv.shape[1] // 2
    full = lambda a: pl.BlockSpec(a.shape, lambda i: (0,) * a.ndim)
    args = (h, g1, kvg, w_in, w_kv, *tables)
    return pl.pallas_call(
        _l1_sample_front_kernel,
        grid=(1,),
        in_specs=[full(a) for a in args],
        out_specs=[pl.BlockSpec((t, width), lambda i: (0, 0)), pl.BlockSpec((t, width), lambda i: (0, 0)),
                   pl.BlockSpec((t, kvw), lambda i: (0, 0)), pl.BlockSpec((t, kvw), lambda i: (0, 0))],
        out_shape=[jax.ShapeDtypeStruct((t, width), F32), jax.ShapeDtypeStruct((t, width), F32),
                   jax.ShapeDtypeStruct((t, kvw), F32), jax.ShapeDtypeStruct((t, kvw), F32)],
        compiler_params=pltpu.CompilerParams(
            dimension_semantics=("arbitrary",), vmem_limit_bytes=VMEM_LIMIT_BYTES),
        name="l1_sample_front",
    )(*args)


def _l1_sample_attn_kernel(q_ref, z_ref, kn_ref, vn_ref, ckt_ref, cvt_ref, sink_ref, og_ref):
    bt, kv_heads, hd, wb = ckt_ref.shape
    n_q = q_ref.shape[1]
    kvw = kv_heads * hd
    valid = lax.broadcasted_iota(jnp.int32, (n_q, wb), 1) > wb - WINDOW
    head_kv = lax.broadcasted_iota(jnp.int32, (n_q, kvw), 0) // Q_PER_KV
    own_lanes = lax.broadcasted_iota(jnp.int32, (n_q, kvw), 1) // hd == head_kv
    row_kv = lax.broadcasted_iota(jnp.int32, (n_q, hd), 0) // Q_PER_KV
    sink = sink_ref[...]
    for j in range(bt):
        q = q_ref[j]
        q_bd = jnp.where(own_lanes, jnp.concatenate([q] * kv_heads, axis=1), 0.0)
        s = jnp.where(valid, _dot(q_bd.astype(BF16), ckt_ref[j].reshape(kvw, wb).astype(BF16)), MASKED)
        s_new = jnp.sum(q_bd * kn_ref[j:j + 1, :], axis=-1, keepdims=True)
        mx = jnp.maximum(jnp.maximum(jnp.max(s, axis=-1, keepdims=True), s_new), sink)
        e = jnp.exp(s - mx)
        e_new = jnp.exp(s_new - mx)
        den = jnp.sum(e, axis=-1, keepdims=True) + e_new + jnp.exp(sink - mx)
        o_all = _dot_nt(e.astype(BF16), cvt_ref[j].reshape(kvw, wb).astype(BF16)) + e_new * vn_ref[j:j + 1, :]
        o = jnp.zeros((n_q, hd), F32)
        for g in range(kv_heads):
            o = o + jnp.where(row_kv == g, o_all[:, g * hd:(g + 1) * hd], 0.0)
        og_ref[j] = o / den * z_ref[j]


def _l1_sample_attn(q3, z3, k_new, v_new, cache_kt, cache_vt, sinks_col, bt=8):
    n, heads, hd = q3.shape
    kv_heads, wb = cache_kt.shape[1], cache_kt.shape[3]
    qspec = pl.BlockSpec((bt, heads, hd), lambda i: (i, 0, 0))
    nspec = pl.BlockSpec((bt, kv_heads * hd), lambda i: (i, 0))
    cspec = pl.BlockSpec((bt, kv_heads, hd, wb), lambda i: (i, 0, 0, 0))
    return pl.pallas_call(
        _l1_sample_attn_kernel,
        grid=(n // bt,),
        in_specs=[qspec, qspec, nspec, nspec, cspec, cspec, pl.BlockSpec((heads, 1), lambda i: (0, 0))],
        out_specs=qspec,
        out_shape=jax.ShapeDtypeStruct((n, heads, hd), F32),
        compiler_params=pltpu.CompilerParams(
            dimension_semantics=("arbitrary",), vmem_limit_bytes=VMEM_LIMIT_BYTES),
        name="l1_sample_attn",
    )(q3, z3, k_new, v_new, cache_kt, cache_vt, sinks_col)


def kernel(x_prompt, x_sample, p_prompt, p_sample, state_hgrn, cache_k, cache_v, pre_norm_g, post_norm_g,
           w_in_a, lb_logits, onorm_a, w_out_a, kv_norm_g, w_kv, w_in_b, sinks, w_out_b, w_pe, w_pg):
    bsz, seq, dm = x_prompt.shape
    n_s = x_sample.shape[0]
    assert x_sample.shape[1] == 1 and w_in_a.shape[0] == 1 and w_in_b.shape[0] == 1
    assert seq % 256 == 0 and n_s % 8 == 0
    kv_heads = w_kv.shape[1] // (2 * HEAD_DIM)
    n_q = w_in_b.shape[2] // (2 * HEAD_DIM)
    assert n_q == kv_heads * Q_PER_KV

    row = lambda a: a.reshape(1, -1)
    bf = lambda a: a.astype(BF16)
    w_in_a0, w_out_a0 = bf(w_in_a[0]), bf(w_out_a[0])
    w_in_b0, w_out_b0 = bf(w_in_b[0]), bf(w_out_b[0])
    w_pe_b, w_pg_b = bf(w_pe), bf(w_pg)
    w_kv_b = bf(w_kv)
    dup = lambda w: jnp.concatenate([w.reshape(dm, kv_heads, 1, HEAD_DIM)] * 2, axis=2).reshape(dm, -1)
    kvw = kv_heads * HEAD_DIM
    w_kv_dup = jnp.concatenate([dup(w_kv_b[:, :kvw]), dup(w_kv_b[:, kvw:])], axis=1)
    undup = lambda a: a.reshape(a.shape[0], a.shape[1], kv_heads, 2, HEAD_DIM)[:, :, :, 0, :]

    pp = p_prompt.reshape(p_prompt.shape[0], bsz * seq, -1)
    ps = p_sample.reshape(p_sample.shape[0], n_s, -1)

    og0, st_p = _l0_prompt(x_prompt, row(pre_norm_g[0]), w_in_a0, lb_logits, row(onorm_a[0]))
    h1 = _epilogue(og0.reshape(bsz * seq, -1), x_prompt.reshape(bsz * seq, dm), pp, 0,
                   w_out_a0, row(post_norm_g[0]), w_pe_b[0], w_pg_b[0], tm=512)
    tables_p = _rope_tables(jnp.arange(seq), 2)
    og1, k_p, v_p = _l1_prompt(h1.reshape(bsz, seq, dm), row(pre_norm_g[1]), row(kv_norm_g), w_kv_dup, w_in_b0,
                               sinks[0], tables_p)
    y_p = _epilogue(og1.reshape(bsz * seq, -1), h1, pp, 1,
                    w_out_b0, row(post_norm_g[1]), w_pe_b[1], w_pg_b[1], tm=512)

    xs = x_sample.reshape(n_s, dm)
    u0 = _norm_matmul(xs, row(pre_norm_g[0]), w_in_a0)
    st_s, og0s = _l0_sample(u0, lb_logits, row(onorm_a[0]), state_hgrn[0])
    h1s = _epilogue(og0s, xs, ps, 0,
                    w_out_a0, row(post_norm_g[0]), w_pe_b[0], w_pg_b[0], tm=n_s)
    tables_s = _rope_tables(PAST_LEN + jnp.arange(1), n_q)
    q_s, z_s, k_s, v_s = _l1_sample_front(h1s, row(pre_norm_g[1]), row(kv_norm_g), w_in_b0, w_kv_b, tables_s)
    og1s = _l1_sample_attn(q_s.reshape(n_s, n_q, HEAD_DIM), z_s.reshape(n_s, n_q, HEAD_DIM), k_s, v_s,
                           cache_k.transpose(0, 2, 3, 1), cache_v.transpose(0, 2, 3, 1),
                           sinks[0].reshape(n_q, 1))
    y_s = _epilogue(og1s.reshape(n_s, -1), h1s, ps, 1,
                    w_out_b0, row(post_norm_g[1]), w_pe_b[1], w_pg_b[1], tm=n_s)

    return (y_p.reshape(bsz, seq, dm), y_s.reshape(n_s, 1, dm),
            st_p[None], st_s[None],
            undup(k_p), undup(v_p),
            k_s.reshape(n_s, 1, kv_heads, HEAD_DIM), v_s.reshape(n_s, 1, kv_heads, HEAD_DIM))
```

```python
import functools

import jax
import jax.numpy as jnp
from jax import lax
from jax.experimental import pallas as pl
from jax.experimental.pallas import tpu as pltpu

F32 = jnp.float32
BF16 = jnp.bfloat16

EPS = 1e-6
HGRN_HEAD_DIM = 128
HEAD_DIM = 64
Q_PER_KV = 8
WINDOW = 128
ROT_DIM = 16
ROPE_THETA = 500000.0
PAST_LEN = 16384
ATTN_SCALE = HEAD_DIM ** -0.5
LOG2_E = 1.4426950408889634

CHUNK = 128
LEVEL_HALVES = (64, 32, 16, 8, 4, 2, 1)
FAST_TAIL = 32
FAST_MAX_EXPONENT = 40.0
MASKED = -1e30
VMEM_LIMIT_BYTES = 56 * 1024 * 1024


def _sigmoid(x):
    return 1.0 / (1.0 + jnp.exp(-x))


def _silu(x):
    return x * _sigmoid(x)


def _rms(x):
    return x * lax.rsqrt(jnp.mean(x * x, axis=-1, keepdims=True) + EPS)


def _dot(a, b):
    return jnp.dot(a, b, preferred_element_type=F32)


def _dot_nt(a, b):
    return lax.dot_general(a, b, (((1,), (1,)), ((), ())), preferred_element_type=F32)


def _dot_tn(a, b):
    return lax.dot_general(a, b, (((0,), (0,)), ((), ())), preferred_element_type=F32)


def _block_diag2(a):
    c = a.shape[1] // 2
    lane = lax.broadcasted_iota(jnp.int32, a.shape, 1)
    zero = jnp.zeros_like(a)
    return jnp.concatenate([jnp.where(lane < c, a, zero), jnp.where(lane >= c, a, zero)], axis=0)


def _rope(x, cos, sin_lo, sin_hi):
    n = x.shape[-1]
    half = ROT_DIM // 2
    return x * cos + pltpu.roll(x, n - half, 1) * sin_lo + pltpu.roll(x, half, 1) * sin_hi


def _rope_tables(pos, reps):
    half = ROT_DIM // 2
    inv = ROPE_THETA ** (-jnp.arange(0, ROT_DIM, 2, dtype=F32) / ROT_DIM)
    ang = pos.astype(F32)[:, None] * inv[None, :]
    cos, sin = jnp.cos(ang), jnp.sin(ang)
    n = pos.shape[0]
    pad = jnp.zeros((n, HEAD_DIM - ROT_DIM), F32)
    zero = jnp.zeros((n, half), F32)
    c = jnp.concatenate([cos, cos, pad + 1.0], axis=1)
    s_lo = jnp.concatenate([-sin, zero, pad], axis=1)
    s_hi = jnp.concatenate([zero, sin, pad], axis=1)
    return tuple(jnp.tile(t, (1, reps)) for t in (c, s_lo, s_hi))


def _lower_bound(lbl):
    m = jnp.max(lbl, axis=0, keepdims=True)
    e = jnp.exp(lbl - m)
    return e[0:1] / jnp.sum(e, axis=0, keepdims=True)


def _level_ref(b, half, row):
    if half >= 8:
        return _block_rows(b, half - 1, 2 * half)
    n = b.shape[0]
    up = lambda x, d: pltpu.roll(x, n - d, 0)
    down = lambda x, d: pltpu.roll(x, d, 0)
    if half == 1:
        return jnp.where(row % 2 == 1, down(b, 1), b)
    y1 = jnp.where(row % 2 == 0, up(b, 1), b)
    if half == 2:
        return jnp.where(row % 4 >= 2, down(y1, 2), y1)
    y2 = jnp.where(row % 4 < 2, up(y1, 2), y1)
    return jnp.where(row % 8 >= 4, down(y2, 4), y2)


def _block_rows(b, offset, size):
    pieces = [jnp.broadcast_to(b[s + offset:s + offset + 1, :], (size, b.shape[1]))
              for s in range(0, b.shape[0], size)]
    return pieces[0] if len(pieces) == 1 else jnp.concatenate(pieces, axis=0)


def _block_diag2_aligned(a):
    c = a.shape[1] // 2
    zero = jnp.zeros((a.shape[0], c), a.dtype)
    return jnp.concatenate([jnp.concatenate([a[:, :c], zero], axis=1),
                            jnp.concatenate([zero, a[:, c:]], axis=1)], axis=0)


def _score_masks(c, w, halves, tail):
    d = w // 2
    row = lax.broadcasted_iota(jnp.int32, (c, w), 0)
    col_tok = lax.broadcasted_iota(jnp.int32, (c, w), 1) % d
    one = lambda cond: jnp.where(cond, 1.0, 0.0).astype(F32)
    masks = {"diag": one(row == col_tok)}
    for half in halves:
        masks[half] = one(row // (2 * half) == col_tok // (2 * half))
    if tail is not None:
        masks["tail"] = one((row // tail == col_tok // tail) & (row > col_tok))
    return masks


def _chunk_scores(q, k, q_bf, k_bf, b, halves, tail, masks):
    c, w = q.shape
    d = w // 2
    qk = q * k
    diag = jnp.concatenate(
        [jnp.broadcast_to(jnp.sum(qk[:, :d], axis=-1, keepdims=True), (c, d)),
         jnp.broadcast_to(jnp.sum(qk[:, d:], axis=-1, keepdims=True), (c, d))], axis=1)
    a = diag * masks["diag"]
    for half in halves:
        if half >= 16:
            delta = b - _block_rows(b, half - 1, 2 * half)
            zero = jnp.zeros((half, w), BF16)
            qe, ke = [], []
            for s in range(0, c, 2 * half):
                lo, up = slice(s, s + half), slice(s + half, s + 2 * half)
                ke += [k_bf[lo] * jnp.exp2(-delta[lo]).astype(BF16), zero]
                qe += [zero, q_bf[up] * jnp.exp2(delta[up]).astype(BF16)]
            qe, ke = jnp.concatenate(qe, axis=0), jnp.concatenate(ke, axis=0)
        else:
            row = lax.broadcasted_iota(jnp.int32, (c, w), 0)
            upper = row % (2 * half) >= half
            ref = _level_ref(b, half, row)
            e = jnp.exp2(jnp.where(upper, b - ref, ref - b))
            qe = jnp.where(upper, q * e, 0.0).astype(BF16)
            ke = jnp.where(upper, 0.0, k * e).astype(BF16)
        p = _dot_nt(qe, _block_diag2_aligned(ke))
        a = a + (p if 2 * half == c else p * masks[half])
    if tail is not None:
        delta = b - _block_rows(b, tail // 2 - 1, tail)
        qe = q_bf * jnp.exp2(delta).astype(BF16)
        ke = k_bf * jnp.exp2(-delta).astype(BF16)
        a = a + _dot_nt(qe, _block_diag2_aligned(ke)) * masks["tail"]
    return a


def _chunk_apply(q_bf, k_bf, b, v, a, st_a, st_b):
    c, w = b.shape
    d = w // 2
    st_bd = _block_diag2_aligned(jnp.concatenate([st_a, st_b], axis=1).astype(BF16))
    o = _dot_nt(q_bf * jnp.exp2(b).astype(BF16), st_bd) + _dot(a.astype(BF16), _block_diag2_aligned(v))
    b_last = b[c - 1:c, :]
    k_hat = k_bf * jnp.exp2(b_last - b).astype(BF16)
    ds = _dot_tn(v, k_hat)
    decay = jnp.exp2(b_last)
    st_a = st_a * decay[:, :d] + ds[:d, :d]
    st_b = st_b * decay[:, d:] + ds[d:, d:]
    return o, st_a, st_b


def _l0_prompt_kernel(x_ref, pg_ref, w_ref, lbl_ref, on_ref, og_ref, st_ref,
                      s_scr, q_scr, k_scr, b_scr, v_scr, z_scr):
    l = pl.program_id(1)
    tm = x_ref.shape[1]
    width = q_scr.shape[1]
    d = HGRN_HEAD_DIM
    n_heads = width // d

    @pl.when(l == 0)
    def _():
        s_scr[...] = jnp.zeros_like(s_scr)

    xn = (_rms(x_ref[0]) * pg_ref[...]).astype(BF16)
    lb = _lower_bound(lbl_ref[...])
    q_scr[...] = _silu(_dot(xn, w_ref[:, 0:width]))
    f = lb + (1.0 - lb) * _sigmoid(_dot(xn, w_ref[:, width:2 * width]))
    k_scr[...] = 1.0 - f
    v_scr[...] = _dot(xn, w_ref[:, 2 * width:3 * width]).astype(BF16)
    z_scr[...] = _silu(_dot(xn, w_ref[:, 3 * width:4 * width])) * on_ref[...]

    g = jnp.log2(f)
    g_hi = g.astype(BF16)
    r1 = g - g_hi.astype(F32)
    g_mid = r1.astype(BF16)
    g_lo = (r1 - g_mid.astype(F32)).astype(BF16)
    ti = lax.broadcasted_iota(jnp.int32, (CHUNK, CHUNK), 0)
    tj = lax.broadcasted_iota(jnp.int32, (CHUNK, CHUNK), 1)
    tri = jnp.where(tj <= ti, 1.0, 0.0).astype(BF16)
    tri2 = jnp.concatenate([tri, tri], axis=1)
    worst = jnp.zeros((1, width), F32)
    for c in range(tm // CHUNK):
        rows = slice(c * CHUNK, (c + 1) * CHUNK)
        b = _dot(tri2, jnp.concatenate([g_hi[rows], g_mid[rows]], axis=0)) + _dot(tri, g_lo[rows])
        b_scr[rows, :] = b
        for s in range(0, CHUNK, FAST_TAIL):
            mid = b[s + FAST_TAIL // 2 - 1:s + FAST_TAIL // 2, :]
            worst = jnp.maximum(worst, jnp.maximum(b[s:s + 1, :] - mid, mid - b[s + FAST_TAIL - 1:s + FAST_TAIL, :]))
    fast = jnp.max(worst) <= FAST_MAX_EXPONENT * LOG2_E

    def scan(halves, tail, unroll):
        masks = _score_masks(CHUNK, 2 * d, halves, tail)
        chunks = [slice(c * CHUNK, (c + 1) * CHUNK) for c in range(tm // CHUNK)]

        def pair_body(pi, carry):
            col = pl.multiple_of(pi * 2 * d, 2 * d)
            load = lambda ref, rows: ref[rows, pl.ds(col, 2 * d)]
            st_a = s_scr[2 * pi]
            st_b = s_scr[2 * pi + 1]
            operands = []
            for rows in chunks:
                q, k, b = load(q_scr, rows), load(k_scr, rows), load(b_scr, rows)
                q_bf, k_bf = q.astype(BF16), k.astype(BF16)
                operands.append((q_bf, k_bf, b, _chunk_scores(q, k, q_bf, k_bf, b, halves, tail, masks)))
            for rows, (q_bf, k_bf, b, a) in zip(chunks, operands):
                o, st_a, st_b = _chunk_apply(q_bf, k_bf, b, load(v_scr, rows), a, st_a, st_b)
                on = jnp.concatenate([_rms(o[:, :d]), _rms(o[:, d:])], axis=1)
                og_ref[0, rows, pl.ds(col, 2 * d)] = (on * load(z_scr, rows)).astype(og_ref.dtype)
            s_scr[2 * pi] = st_a
            s_scr[2 * pi + 1] = st_b
            return carry

        lax.fori_loop(0, n_heads // 2, pair_body, 0, unroll=unroll)

    @pl.when(fast)
    def _():
        scan(tuple(h for h in LEVEL_HALVES if h >= FAST_TAIL), FAST_TAIL, unroll=4)

    @pl.when(jnp.logical_not(fast))
    def _():
        scan(LEVEL_HALVES, None, unroll=1)

    @pl.when(l == pl.num_programs(1) - 1)
    def _():
        for h in range(n_heads):
            st_ref[0, h] = s_scr[h].T


def _l0_prompt(x, pre_g, w_in, lb_logits, onorm, tm=256):
    bsz, seq, dm = x.shape
    width = w_in.shape[1] // 4
    n_heads = width // HGRN_HEAD_DIM
    d = HGRN_HEAD_DIM
    const = lambda *shape: pl.BlockSpec(shape, lambda b, l: (0,) * len(shape))
    return pl.pallas_call(
        _l0_prompt_kernel,
        grid=(bsz, seq // tm),
        in_specs=[
            pl.BlockSpec((1, tm, dm), lambda b, l: (b, l, 0)),
            const(1, dm),
            pl.BlockSpec((dm, 4 * width), lambda b, l: (0, 0), pipeline_mode=pl.Buffered(1)),
            const(*lb_logits.shape),
            const(1, width),
        ],
        out_specs=[
            pl.BlockSpec((1, tm, width), lambda b, l: (b, l, 0)),
            pl.BlockSpec((1, n_heads, d, d), lambda b, l: (b, 0, 0, 0)),
        ],
        out_shape=[
            jax.ShapeDtypeStruct((bsz, seq, width), BF16),
            jax.ShapeDtypeStruct((bsz, n_heads, d, d), F32),
        ],
        scratch_shapes=[
            pltpu.VMEM((n_heads, d, d), F32),
            pltpu.VMEM((tm, width), F32),
            pltpu.VMEM((tm, width), F32),
            pltpu.VMEM((tm, width), F32),
            pltpu.VMEM((tm, width), BF16),
            pltpu.VMEM((tm, width), F32),
        ],
        compiler_params=pltpu.CompilerParams(
            dimension_semantics=("arbitrary", "arbitrary"), vmem_limit_bytes=VMEM_LIMIT_BYTES),
        name="l0_prompt",
    )(x, pre_g, w_in, lb_logits, onorm)


def _epilogue_kernel(og_ref, h_ref, p_ref, wo_ref, pg_ref, wpe_ref, wpg_ref, out_ref):
    mix = _dot(og_ref[...].astype(BF16), wo_ref[...])
    hn = h_ref[...] + _rms(mix) * pg_ref[...]
    pe = _dot(p_ref[...].astype(BF16), wpe_ref[...])
    gate = _sigmoid(_dot(hn.astype(BF16), wpg_ref[...]))
    out_ref[...] = hn + pe * gate


def _epilogue(og, h, p_all, layer, w_out, post_g, w_pe, w_pg, tm):
    t, dm = h.shape
    width = og.shape[1]
    ple = p_all.shape[2]
    tm = min(tm, t)
    const = lambda *shape: pl.BlockSpec(shape, lambda i: (0,) * len(shape))
    return pl.pallas_call(
        _epilogue_kernel,
        grid=(t // tm,),
        in_specs=[
            pl.BlockSpec((tm, width), lambda i: (i, 0)),
            pl.BlockSpec((tm, dm), lambda i: (i, 0)),
            pl.BlockSpec((None, tm, ple), lambda i: (layer, i, 0)),
            const(width, dm), const(1, dm), const(ple, dm), const(dm, dm),
        ],
        out_specs=pl.BlockSpec((tm, dm), lambda i: (i, 0)),
        out_shape=jax.ShapeDtypeStruct((t, dm), F32),
        compiler_params=pltpu.CompilerParams(
            dimension_semantics=("arbitrary",), vmem_limit_bytes=VMEM_LIMIT_BYTES),
        name="epilogue",
    )(og, h, p_all, w_out, post_g, w_pe, w_pg)


def _l1_prompt_kernel(sinks_ref, h_ref, g1_ref, kvg_ref, wkv_ref, win_ref, cos_ref, slo_ref, shi_ref,
                      og_ref, kc_ref, vc_ref, kx_scr, vx_scr, q_scr, z_scr):
    l = pl.program_id(1)
    tm = h_ref.shape[1]
    width = z_scr.shape[1]
    kvw = kx_scr.shape[1]
    pair_w = 2 * HEAD_DIM
    n_pairs = width // pair_w
    pairs_per_kv = Q_PER_KV // 2

    @pl.when(l == 0)
    def _():
        kx_scr[0:WINDOW, :] = jnp.zeros((WINDOW, kvw), kx_scr.dtype)
        vx_scr[0:WINDOW, :] = jnp.zeros((WINDOW, kvw), vx_scr.dtype)

    @pl.when(l > 0)
    def _():
        kx_scr[0:WINDOW, :] = kx_scr[tm:tm + WINDOW, :]
        vx_scr[0:WINDOW, :] = vx_scr[tm:tm + WINDOW, :]

    h0 = _rms(h_ref[0])
    xn = (h0 * g1_ref[...]).astype(BF16)
    hk = (h0 * kvg_ref[...]).astype(BF16)
    cos, slo, shi = cos_ref[...], slo_ref[...], shi_ref[...]
    tile = lambda t, n: jnp.concatenate([t] * n, axis=1)

    kv = _dot(hk, wkv_ref[...])
    k_rot = _rope(kv[:, :kvw], tile(cos, kvw // pair_w), tile(slo, kvw // pair_w), tile(shi, kvw // pair_w))
    v_new = kv[:, kvw:]
    kx_scr[WINDOW:WINDOW + tm, :] = k_rot.astype(kx_scr.dtype)
    vx_scr[WINDOW:WINDOW + tm, :] = v_new.astype(vx_scr.dtype)

    @pl.when(l == pl.num_programs(1) - 1)
    def _():
        kc_ref[0] = k_rot[tm - WINDOW:, :]
        vc_ref[0] = v_new[tm - WINDOW:, :]

    q = _rope(_dot(xn, win_ref[:, 0:width]), tile(cos, n_pairs), tile(slo, n_pairs), tile(shi, n_pairs))
    q_scr[...] = (q * (ATTN_SCALE * LOG2_E)).astype(q_scr.dtype)
    z_scr[...] = _silu(_dot(xn, win_ref[:, width:2 * width]))

    qi = lax.broadcasted_iota(jnp.int32, (WINDOW, 2 * WINDOW), 0)
    ki = lax.broadcasted_iota(jnp.int32, (WINDOW, 2 * WINDOW), 1)
    band = (ki > qi) & (ki <= qi + WINDOW)
    lane = lax.broadcasted_iota(jnp.int32, (WINDOW, pair_w), 1)

    def group_body(g, carry):
        kcol = pl.multiple_of(g * pair_w, pair_w)
        cols = [pl.multiple_of((g * pairs_per_kv + p) * pair_w, pair_w) for p in range(pairs_per_kv)]
        for c in range(tm // WINDOW):
            rows = slice(c * WINDOW, (c + 1) * WINDOW)
            keys = slice(c * WINDOW, (c + 2) * WINDOW)
            k_bd = _block_diag2(kx_scr[keys, pl.ds(kcol, pair_w)])
            v_bd = _block_diag2(vx_scr[keys, pl.ds(kcol, pair_w)])
            q_all = jnp.concatenate([q_scr[rows, pl.ds(col, pair_w)] for col in cols], axis=0)
            s_all = _dot_nt(q_all, k_bd)
            mask = band & (ki >= jnp.where(l == 0, WINDOW, 0)) if c == 0 else band
            probs, inv = [], []
            for p in range(pairs_per_kv):
                pair_probs = []
                for hh in range(2):
                    sink = sinks_ref[(g * pairs_per_kv + p) * 2 + hh] * LOG2_E
                    s = s_all[p * WINDOW:(p + 1) * WINDOW, hh * 2 * WINDOW:(hh + 1) * 2 * WINDOW]
                    s = jnp.where(mask, s, MASKED)
                    mx = jnp.maximum(jnp.max(s, axis=-1, keepdims=True), sink)
                    e = jnp.exp2(s - mx)
                    inv.append(1.0 / (jnp.sum(e, axis=-1, keepdims=True) + jnp.exp2(sink - mx)))
                    pair_probs.append(e.astype(BF16))
                probs.append(jnp.concatenate(pair_probs, axis=1))
            o_all = _dot(jnp.concatenate(probs, axis=0), v_bd)
            for p, col in enumerate(cols):
                o2 = o_all[p * WINDOW:(p + 1) * WINDOW] * jnp.where(lane < HEAD_DIM, inv[2 * p], inv[2 * p + 1])
                og_ref[0, rows, pl.ds(col, pair_w)] = (o2 * z_scr[rows, pl.ds(col, pair_w)]).astype(og_ref.dtype)
        return carry

    lax.fori_loop(0, n_pairs // pairs_per_kv, group_body, 0)


def _l1_prompt(h, g1, kvg, w_kv_dup, w_in, sinks, tables, tm=256):
    bsz, seq, dm = h.shape
    width = w_in.shape[1] // 2
    kvw = w_kv_dup.shape[1] // 2
    pair_w = 2 * HEAD_DIM
    const = lambda *shape: pl.BlockSpec(shape, lambda b, l: (0,) * len(shape))
    table = pl.BlockSpec((tm, pair_w), lambda b, l: (l, 0))
    return pl.pallas_call(
        _l1_prompt_kernel,
        grid=(bsz, seq // tm),
        in_specs=[
            pl.BlockSpec(memory_space=pltpu.SMEM),
            pl.BlockSpec((1, tm, dm), lambda b, l: (b, l, 0)),
            const(1, dm), const(1, dm), const(dm, 2 * kvw), const(dm, 2 * width),
            table, table, table,
        ],
        out_specs=[
            pl.BlockSpec((1, tm, width), lambda b, l: (b, l, 0)),
            pl.BlockSpec((1, WINDOW, kvw), lambda b, l: (b, 0, 0)),
            pl.BlockSpec((1, WINDOW, kvw), lambda b, l: (b, 0, 0)),
        ],
        out_shape=[
            jax.ShapeDtypeStruct((bsz, seq, width), BF16),
            jax.ShapeDtypeStruct((bsz, WINDOW, kvw), F32),
            jax.ShapeDtypeStruct((bsz, WINDOW, kvw), F32),
        ],
        scratch_shapes=[
            pltpu.VMEM((WINDOW + tm, kvw), BF16),
            pltpu.VMEM((WINDOW + tm, kvw), BF16),
            pltpu.VMEM((tm, width), BF16),
            pltpu.VMEM((tm, width), F32),
        ],
        compiler_params=pltpu.CompilerParams(
            dimension_semantics=("arbitrary", "arbitrary"), vmem_limit_bytes=VMEM_LIMIT_BYTES),
        name="l1_prompt",
    )(sinks, h, g1, kvg, w_kv_dup, w_in, *tables)


def _norm_matmul_kernel(x_ref, g_ref, w_ref, o_ref):
    xn = (_rms(x_ref[...]) * g_ref[...]).astype(BF16)
    o_ref[...] = _dot(xn, w_ref[...])


def _norm_matmul(x, g, w, tn=2048):
    t, dm = x.shape
    n = w.shape[1]
    tn = min(tn, n)
    return pl.pallas_call(
        _norm_matmul_kernel,
        grid=(n // tn,),
        in_specs=[
            pl.BlockSpec((t, dm), lambda j: (0, 0)),
            pl.BlockSpec((1, dm), lambda j: (0, 0)),
            pl.BlockSpec((dm, tn), lambda j: (0, j)),
        ],
        out_specs=pl.BlockSpec((t, tn), lambda j: (0, j)),
        out_shape=jax.ShapeDtypeStruct((t, n), F32),
        compiler_params=pltpu.CompilerParams(
            dimension_semantics=("arbitrary",), vmem_limit_bytes=VMEM_LIMIT_BYTES),
        name="norm_matmul",
    )(x, g, w)


def _l0_sample_kernel(u_ref, lbl_ref, on_ref, s0_ref, s1_ref, og_ref):
    bt = u_ref.shape[0]
    width = og_ref.shape[1]
    d = HGRN_HEAD_DIM
    lb_all = _lower_bound(lbl_ref[...])

    for h in range(width // d):
        cols = slice(h * d, (h + 1) * d)
        lb = lb_all[:, cols]
        q = _silu(u_ref[:, h * d:(h + 1) * d])
        f = lb + (1.0 - lb) * _sigmoid(u_ref[:, width + h * d:width + (h + 1) * d])
        v = u_ref[:, 2 * width + h * d:2 * width + (h + 1) * d]
        z = _silu(u_ref[:, 3 * width + h * d:3 * width + (h + 1) * d])
        f_t = f.T
        q_b = q.astype(BF16)
        outs = []
        for j in range(bt):
            v_row = v[j:j + 1, :]
            s_new = v_row + f_t[:, j:j + 1] * (s0_ref[j, h] - v_row)
            s1_ref[j, h] = s_new
            outs.append(_dot(q_b[j:j + 1, :], s_new.astype(BF16)))
        o = jnp.concatenate(outs, axis=0)
        og_ref[:, cols] = _rms(o) * on_ref[:, cols] * z


def _l0_sample(u, lb_logits, onorm, state, bt=8):
    n, heads, d, _ = state.shape
    width = heads * d
    return pl.pallas_call(
        _l0_sample_kernel,
        grid=(n // bt,),
        in_specs=[
            pl.BlockSpec((bt, 4 * width), lambda i: (i, 0)),
            pl.BlockSpec(lb_logits.shape, lambda i: (0, 0)),
            pl.BlockSpec((1, width), lambda i: (0, 0)),
            pl.BlockSpec((bt, heads, d, d), lambda i: (i, 0, 0, 0)),
        ],
        out_specs=[
            pl.BlockSpec((bt, heads, d, d), lambda i: (i, 0, 0, 0)),
            pl.BlockSpec((bt, width), lambda i: (i, 0)),
        ],
        out_shape=[
            jax.ShapeDtypeStruct(state.shape, F32),
            jax.ShapeDtypeStruct((n, width), F32),
        ],
        compiler_params=pltpu.CompilerParams(
            dimension_semantics=("arbitrary",), vmem_limit_bytes=VMEM_LIMIT_BYTES),
        name="l0_sample",
    )(u, lb_logits, onorm, state)


def _l1_sample_front_kernel(h_ref, g1_ref, kvg_ref, win_ref, wkv_ref, cos_ref, slo_ref, shi_ref,
                            q_ref, z_ref, k_ref, v_ref):
    width = q_ref.shape[1]
    kvw = k_ref.shape[1]
    h0 = _rms(h_ref[...])
    xn = (h0 * g1_ref[...]).astype(BF16)
    hk = (h0 * kvg_ref[...]).astype(BF16)
    cos, slo, shi = cos_ref[...], slo_ref[...], shi_ref[...]
    q = _rope(_dot(xn, win_ref[:, 0:width]), cos, slo, shi)
    q_ref[...] = q * ATTN_SCALE
    z_ref[...] = _silu(_dot(xn, win_ref[:, width:2 * width]))
    kv = _dot(hk, wkv_ref[...])
    k_ref[...] = _rope(kv[:, :kvw], cos[:, :kvw], slo[:, :kvw], shi[:, :kvw])
    v_ref[...] = kv[:, kvw:]


def _l1_sample_front(h, g1, kvg, w_in, w_kv, tables):
    t, dm = h.shape
    width = w_in.shape[1] // 2
    kvw = w_kv.shape[1] // 2
    full = lambda a: pl.BlockSpec(a.shape, lambda i: (0,) * a.ndim)
    args = (h, g1, kvg, w_in, w_kv, *tables)
    return pl.pallas_call(
        _l1_sample_front_kernel,
        grid=(1,),
        in_specs=[full(a) for a in args],
        out_specs=[pl.BlockSpec((t, width), lambda i: (0, 0)), pl.BlockSpec((t, width), lambda i: (0, 0)),
                   pl.BlockSpec((t, kvw), lambda i: (0, 0)), pl.BlockSpec((t, kvw), lambda i: (0, 0))],
        out_shape=[jax.ShapeDtypeStruct((t, width), F32), jax.ShapeDtypeStruct((t, width), F32),
                   jax.ShapeDtypeStruct((t, kvw), F32), jax.ShapeDtypeStruct((t, kvw), F32)],
        compiler_params=pltpu.CompilerParams(
            dimension_semantics=("arbitrary",), vmem_limit_bytes=VMEM_LIMIT_BYTES),
        name="l1_sample_front",
    )(*args)


def _l1_sample_attn_kernel(q_ref, z_ref, kn_ref, vn_ref, ckt_ref, cvt_ref, sink_ref, og_ref):
    bt, kv_heads, hd, wb = ckt_ref.shape
    n_q = q_ref.shape[1]
    kvw = kv_heads * hd
    valid = lax.broadcasted_iota(jnp.int32, (n_q, wb), 1) > wb - WINDOW
    head_kv = lax.broadcasted_iota(jnp.int32, (n_q, kvw), 0) // Q_PER_KV
    own_lanes = lax.broadcasted_iota(jnp.int32, (n_q, kvw), 1) // hd == head_kv
    row_kv = lax.broadcasted_iota(jnp.int32, (n_q, hd), 0) // Q_PER_KV
    sink = sink_ref[...]
    for j in range(bt):
        q = q_ref[j]
        q_bd = jnp.where(own_lanes, jnp.concatenate([q] * kv_heads, axis=1), 0.0)
        s = jnp.where(valid, _dot(q_bd.astype(BF16), ckt_ref[j].reshape(kvw, wb).astype(BF16)), MASKED)
        s_new = jnp.sum(q_bd * kn_ref[j:j + 1, :], axis=-1, keepdims=True)
        mx = jnp.maximum(jnp.maximum(jnp.max(s, axis=-1, keepdims=True), s_new), sink)
        e = jnp.exp(s - mx)
        e_new = jnp.exp(s_new - mx)
        den = jnp.sum(e, axis=-1, keepdims=True) + e_new + jnp.exp(sink - mx)
        o_all = _dot_nt(e.astype(BF16), cvt_ref[j].reshape(kvw, wb).astype(BF16)) + e_new * vn_ref[j:j + 1, :]
        o = jnp.zeros((n_q, hd), F32)
        for g in range(kv_heads):
            o = o + jnp.where(row_kv == g, o_all[:, g * hd:(g + 1) * hd], 0.0)
        og_ref[j] = o / den * z_ref[j]


def _l1_sample_attn(q3, z3, k_new, v_new, cache_kt, cache_vt, sinks_col, bt=8):
    n, heads, hd = q3.shape
    kv_heads, wb = cache_kt.shape[1], cache_kt.shape[3]
    qspec = pl.BlockSpec((bt, heads, hd), lambda i: (i, 0, 0))
    nspec = pl.BlockSpec((bt, kv_heads * hd), lambda i: (i, 0))
    cspec = pl.BlockSpec((bt, kv_heads, hd, wb), lambda i: (i, 0, 0, 0))
    return pl.pallas_call(
        _l1_sample_attn_kernel,
        grid=(n // bt,),
        in_specs=[qspec, qspec, nspec, nspec, cspec, cspec, pl.BlockSpec((heads, 1), lambda i: (0, 0))],
        out_specs=qspec,
        out_shape=jax.ShapeDtypeStruct((n, heads, hd), F32),
        compiler_params=pltpu.CompilerParams(
            dimension_semantics=("arbitrary",), vmem_limit_bytes=VMEM_LIMIT_BYTES),
        name="l1_sample_attn",
    )(q3, z3, k_new, v_new, cache_kt, cache_vt, sinks_col)


def kernel(x_prompt, x_sample, p_prompt, p_sample, state_hgrn, cache_k, cache_v, pre_norm_g, post_norm_g,
           w_in_a, lb_logits, onorm_a, w_out_a, kv_norm_g, w_kv, w_in_b, sinks, w_out_b, w_pe, w_pg):
    bsz, seq, dm = x_prompt.shape
    n_s = x_sample.shape[0]
    assert x_sample.shape[1] == 1 and w_in_a.shape[0] == 1 and w_in_b.shape[0] == 1
    assert seq % 256 == 0 and n_s % 8 == 0
    kv_heads = w_kv.shape[1] // (2 * HEAD_DIM)
    n_q = w_in_b.shape[2] // (2 * HEAD_DIM)
    assert n_q == kv_heads * Q_PER_KV

    row = lambda a: a.reshape(1, -1)
    bf = lambda a: a.astype(BF16)
    w_in_a0, w_out_a0 = bf(w_in_a[0]), bf(w_out_a[0])
    w_in_b0, w_out_b0 = bf(w_in_b[0]), bf(w_out_b[0])
    w_pe_b, w_pg_b = bf(w_pe), bf(w_pg)
    w_kv_b = bf(w_kv)
    dup = lambda w: jnp.concatenate([w.reshape(dm, kv_heads, 1, HEAD_DIM)] * 2, axis=2).reshape(dm, -1)
    kvw = kv_heads * HEAD_DIM
    w_kv_dup = jnp.concatenate([dup(w_kv_b[:, :kvw]), dup(w_kv_b[:, kvw:])], axis=1)
    undup = lambda a: a.reshape(a.shape[0], a.shape[1], kv_heads, 2, HEAD_DIM)[:, :, :, 0, :]

    pp = p_prompt.reshape(p_prompt.shape[0], bsz * seq, -1)
    ps = p_sample.reshape(p_sample.shape[0], n_s, -1)

    og0, st_p = _l0_prompt(x_prompt, row(pre_norm_g[0]), w_in_a0, lb_logits, row(onorm_a[0]))
    h1 = _epilogue(og0.reshape(bsz * seq, -1), x_prompt.reshape(bsz * seq, dm), pp, 0,
                   w_out_a0, row(post_norm_g[0]), w_pe_b[0], w_pg_b[0], tm=512)
    tables_p = _rope_tables(jnp.arange(seq), 2)
    og1, k_p, v_p = _l1_prompt(h1.reshape(bsz, seq, dm), row(pre_norm_g[1]), row(kv_norm_g), w_kv_dup, w_in_b0,
                               sinks[0], tables_p)
    y_p = _epilogue(og1.reshape(bsz * seq, -1), h1, pp, 1,
                    w_out_b0, row(post_norm_g[1]), w_pe_b[1], w_pg_b[1], tm=512)

    xs = x_sample.reshape(n_s, dm)
    u0 = _norm_matmul(xs, row(pre_norm_g[0]), w_in_a0)
    st_s, og0s = _l0_sample(u0, lb_logits, row(onorm_a[0]), state_hgrn[0])
    h1s = _epilogue(og0s, xs, ps, 0,
                    w_out_a0, row(post_norm_g[0]), w_pe_b[0], w_pg_b[0], tm=n_s)
    tables_s = _rope_tables(PAST_LEN + jnp.arange(1), n_q)
    q_s, z_s, k_s, v_s = _l1_sample_front(h1s, row(pre_norm_g[1]), row(kv_norm_g), w_in_b0, w_kv_b, tables_s)
    og1s = _l1_sample_attn(q_s.reshape(n_s, n_q, HEAD_DIM), z_s.reshape(n_s, n_q, HEAD_DIM), k_s, v_s,
                           cache_k.transpose(0, 2, 3, 1), cache_v.transpose(0, 2, 3, 1),
                           sinks[0].reshape(n_q, 1))
    y_s = _epilogue(og1s.reshape(n_s, -1), h1s, ps, 1,
                    w_out_b0, row(post_norm_g[1]), w_pe_b[1], w_pg_b[1], tm=n_s)

    return (y_p.reshape(bsz, seq, dm), y_s.reshape(n_s, 1, dm),
            st_p[None], st_s[None],
            undup(k_p), undup(v_p),
            k_s.reshape(n_s, 1, kv_heads, HEAD_DIM), v_s.reshape(n_s, 1, kv_heads, HEAD_DIM))
```

```python
import functools

import jax
import jax.numpy as jnp
from jax import lax
from jax.experimental import pallas as pl
from jax.experimental.pallas import tpu as pltpu

F32 = jnp.float32
BF16 = jnp.bfloat16

EPS = 1e-6
HGRN_HEAD_DIM = 128
HEAD_DIM = 64
Q_PER_KV = 8
WINDOW = 128
ROT_DIM = 16
ROPE_THETA = 500000.0
PAST_LEN = 16384
ATTN_SCALE = HEAD_DIM ** -0.5
LOG2_E = 1.4426950408889634

CHUNK = 128
LEVEL_HALVES = (64, 32, 16, 8, 4, 2, 1)
FAST_TAIL = 32
PROJ_COLS = 512
PROMPT_TILE = 512
FAST_MAX_EXPONENT = 40.0
MASKED = -1e30
VMEM_LIMIT_BYTES = 56 * 1024 * 1024


def _sigmoid(x):
    return 1.0 / (1.0 + jnp.exp(-x))


def _silu(x):
    return x * _sigmoid(x)


def _rms(x):
    return x * lax.rsqrt(jnp.mean(x * x, axis=-1, keepdims=True) + EPS)


def _dot(a, b):
    return jnp.dot(a, b, preferred_element_type=F32)


def _dot_nt(a, b):
    return lax.dot_general(a, b, (((1,), (1,)), ((), ())), preferred_element_type=F32)


def _dot_tn(a, b):
    return lax.dot_general(a, b, (((0,), (0,)), ((), ())), preferred_element_type=F32)


def _block_diag2(a):
    c = a.shape[1] // 2
    lane = lax.broadcasted_iota(jnp.int32, a.shape, 1)
    zero = jnp.zeros_like(a)
    return jnp.concatenate([jnp.where(lane < c, a, zero), jnp.where(lane >= c, a, zero)], axis=0)


def _rope(x, cos, sin_lo, sin_hi):
    n = x.shape[-1]
    half = ROT_DIM // 2
    return x * cos + pltpu.roll(x, n - half, 1) * sin_lo + pltpu.roll(x, half, 1) * sin_hi


def _rope_tables(pos, reps):
    half = ROT_DIM // 2
    inv = ROPE_THETA ** (-jnp.arange(0, ROT_DIM, 2, dtype=F32) / ROT_DIM)
    ang = pos.astype(F32)[:, None] * inv[None, :]
    cos, sin = jnp.cos(ang), jnp.sin(ang)
    n = pos.shape[0]
    pad = jnp.zeros((n, HEAD_DIM - ROT_DIM), F32)
    zero = jnp.zeros((n, half), F32)
    c = jnp.concatenate([cos, cos, pad + 1.0], axis=1)
    s_lo = jnp.concatenate([-sin, zero, pad], axis=1)
    s_hi = jnp.concatenate([zero, sin, pad], axis=1)
    return tuple(jnp.tile(t, (1, reps)) for t in (c, s_lo, s_hi))


def _lower_bound(lbl):
    m = jnp.max(lbl, axis=0, keepdims=True)
    e = jnp.exp(lbl - m)
    return e[0:1] / jnp.sum(e, axis=0, keepdims=True)


def _level_ref(b, half, row):
    if half >= 8:
        return _block_rows(b, half - 1, 2 * half)
    n = b.shape[0]
    up = lambda x, d: pltpu.roll(x, n - d, 0)
    down = lambda x, d: pltpu.roll(x, d, 0)
    if half == 1:
        return jnp.where(row % 2 == 1, down(b, 1), b)
    y1 = jnp.where(row % 2 == 0, up(b, 1), b)
    if half == 2:
        return jnp.where(row % 4 >= 2, down(y1, 2), y1)
    y2 = jnp.where(row % 4 < 2, up(y1, 2), y1)
    return jnp.where(row % 8 >= 4, down(y2, 4), y2)


def _block_rows(b, offset, size):
    pieces = [jnp.broadcast_to(b[s + offset:s + offset + 1, :], (size, b.shape[1]))
              for s in range(0, b.shape[0], size)]
    return pieces[0] if len(pieces) == 1 else jnp.concatenate(pieces, axis=0)


def _block_diag2_aligned(a):
    c = a.shape[1] // 2
    zero = jnp.zeros((a.shape[0], c), a.dtype)
    return jnp.concatenate([jnp.concatenate([a[:, :c], zero], axis=1),
                            jnp.concatenate([zero, a[:, c:]], axis=1)], axis=0)


def _score_masks(c, w, halves, tail):
    d = w // 2
    row = lax.broadcasted_iota(jnp.int32, (c, w), 0)
    col_tok = lax.broadcasted_iota(jnp.int32, (c, w), 1) % d
    one = lambda cond: jnp.where(cond, 1.0, 0.0).astype(F32)
    masks = {"diag": one(row == col_tok)}
    for half in halves:
        masks[half] = one(row // (2 * half) == col_tok // (2 * half))
    if tail is not None:
        masks["tail"] = one((row // tail == col_tok // tail) & (row > col_tok))
    return masks


def _chunk_scores(q, k, q_bf, k_bf, b, halves, tail, masks):
    c, w = q.shape
    d = w // 2
    qk = q * k
    diag = jnp.concatenate(
        [jnp.broadcast_to(jnp.sum(qk[:, :d], axis=-1, keepdims=True), (c, d)),
         jnp.broadcast_to(jnp.sum(qk[:, d:], axis=-1, keepdims=True), (c, d))], axis=1)
    a = diag * masks["diag"]
    for half in halves:
        if half >= 16:
            delta = b - _block_rows(b, half - 1, 2 * half)
            zero = jnp.zeros((half, w), BF16)
            qe, ke = [], []
            for s in range(0, c, 2 * half):
                lo, up = slice(s, s + half), slice(s + half, s + 2 * half)
                ke += [k_bf[lo] * jnp.exp2(-delta[lo]).astype(BF16), zero]
                qe += [zero, q_bf[up] * jnp.exp2(delta[up]).astype(BF16)]
            qe, ke = jnp.concatenate(qe, axis=0), jnp.concatenate(ke, axis=0)
        else:
            row = lax.broadcasted_iota(jnp.int32, (c, w), 0)
            upper = row % (2 * half) >= half
            ref = _level_ref(b, half, row)
            e = jnp.exp2(jnp.where(upper, b - ref, ref - b))
            qe = jnp.where(upper, q * e, 0.0).astype(BF16)
            ke = jnp.where(upper, 0.0, k * e).astype(BF16)
        p = _dot_nt(qe, _block_diag2_aligned(ke))
        a = a + (p if 2 * half == c else p * masks[half])
    if tail is not None:
        delta = b - _block_rows(b, tail // 2 - 1, tail)
        qe = q_bf * jnp.exp2(delta).astype(BF16)
        ke = k_bf * jnp.exp2(-delta).astype(BF16)
        a = a + _dot_nt(qe, _block_diag2_aligned(ke)) * masks["tail"]
    return a


def _chunk_apply(q_bf, k_bf, b, v, a, st_a, st_b):
    c, w = b.shape
    d = w // 2
    st_bd = _block_diag2_aligned(jnp.concatenate([st_a, st_b], axis=1).astype(BF16))
    o = _dot_nt(q_bf * jnp.exp2(b).astype(BF16), st_bd) + _dot(a.astype(BF16), _block_diag2_aligned(v))
    b_last = b[c - 1:c, :]
    k_hat = k_bf * jnp.exp2(b_last - b).astype(BF16)
    ds = _dot_tn(v, k_hat)
    decay = jnp.exp2(b_last)
    st_a = st_a * decay[:, :d] + ds[:d, :d]
    st_b = st_b * decay[:, d:] + ds[d:, d:]
    return o, st_a, st_b


def _l0_prompt_kernel(x_ref, pg_ref, w_ref, lbl_ref, on_ref, og_ref, st_ref,
                      s_scr, q_scr, k_scr, b_scr, v_scr, z_scr):
    l = pl.program_id(1)
    tm = x_ref.shape[1]
    width = q_scr.shape[1]
    d = HGRN_HEAD_DIM
    n_heads = width // d

    @pl.when(l == 0)
    def _():
        s_scr[...] = jnp.zeros_like(s_scr)

    xn = (_rms(x_ref[0]) * pg_ref[...]).astype(BF16)
    ti = lax.broadcasted_iota(jnp.int32, (CHUNK, CHUNK), 0)
    tj = lax.broadcasted_iota(jnp.int32, (CHUNK, CHUNK), 1)
    tri = jnp.where(tj <= ti, 1.0, 0.0).astype(BF16)
    tri2 = jnp.concatenate([tri, tri], axis=1)
    worst = jnp.zeros((1, PROJ_COLS), F32)
    for j in range(width // PROJ_COLS):
        cols = slice(j * PROJ_COLS, (j + 1) * PROJ_COLS)
        part = lambda i: w_ref[:, i * width + j * PROJ_COLS:i * width + (j + 1) * PROJ_COLS]
        lb = _lower_bound(lbl_ref[:, cols])
        q_scr[:, cols] = _silu(_dot(xn, part(0)))
        f = lb + (1.0 - lb) * _sigmoid(_dot(xn, part(1)))
        k_scr[:, cols] = 1.0 - f
        v_scr[:, cols] = _dot(xn, part(2)).astype(BF16)
        z_scr[:, cols] = _silu(_dot(xn, part(3))) * on_ref[:, cols]
        g = jnp.log2(f)
        g_hi = g.astype(BF16)
        r1 = g - g_hi.astype(F32)
        g_mid = r1.astype(BF16)
        g_lo = (r1 - g_mid.astype(F32)).astype(BF16)
        for c in range(tm // CHUNK):
            rows = slice(c * CHUNK, (c + 1) * CHUNK)
            b = _dot(tri2, jnp.concatenate([g_hi[rows], g_mid[rows]], axis=0)) + _dot(tri, g_lo[rows])
            b_scr[rows, cols] = b
            for s in range(0, CHUNK, FAST_TAIL):
                mid = b[s + FAST_TAIL // 2 - 1:s + FAST_TAIL // 2, :]
                worst = jnp.maximum(
                    worst, jnp.maximum(b[s:s + 1, :] - mid, mid - b[s + FAST_TAIL - 1:s + FAST_TAIL, :]))
    fast = jnp.max(worst) <= FAST_MAX_EXPONENT * LOG2_E

    def scan(halves, tail, unroll):
        masks = _score_masks(CHUNK, 2 * d, halves, tail)
        chunks = [slice(c * CHUNK, (c + 1) * CHUNK) for c in range(tm // CHUNK)]

        def pair_body(pi, carry):
            col = pl.multiple_of(pi * 2 * d, 2 * d)
            load = lambda ref, rows: ref[rows, pl.ds(col, 2 * d)]
            st_a = s_scr[2 * pi]
            st_b = s_scr[2 * pi + 1]
            operands = []
            for rows in chunks:
                q, k, b = load(q_scr, rows), load(k_scr, rows), load(b_scr, rows)
                q_bf, k_bf = q.astype(BF16), k.astype(BF16)
                operands.append((q_bf, k_bf, b, _chunk_scores(q, k, q_bf, k_bf, b, halves, tail, masks)))
            for rows, (q_bf, k_bf, b, a) in zip(chunks, operands):
                o, st_a, st_b = _chunk_apply(q_bf, k_bf, b, load(v_scr, rows), a, st_a, st_b)
                on = jnp.concatenate([_rms(o[:, :d]), _rms(o[:, d:])], axis=1)
                og_ref[0, rows, pl.ds(col, 2 * d)] = (on * load(z_scr, rows)).astype(og_ref.dtype)
            s_scr[2 * pi] = st_a
            s_scr[2 * pi + 1] = st_b
            return carry

        lax.fori_loop(0, n_heads // 2, pair_body, 0, unroll=unroll)

    @pl.when(fast)
    def _():
        scan(tuple(h for h in LEVEL_HALVES if h >= FAST_TAIL), FAST_TAIL, unroll=4)

    @pl.when(jnp.logical_not(fast))
    def _():
        scan(LEVEL_HALVES, None, unroll=1)

    @pl.when(l == pl.num_programs(1) - 1)
    def _():
        for h in range(n_heads):
            st_ref[0, h] = s_scr[h].T


def _l0_prompt(x, pre_g, w_in, lb_logits, onorm, tm=PROMPT_TILE):
    bsz, seq, dm = x.shape
    width = w_in.shape[1] // 4
    n_heads = width // HGRN_HEAD_DIM
    d = HGRN_HEAD_DIM
    const = lambda *shape: pl.BlockSpec(shape, lambda b, l: (0,) * len(shape))
    return pl.pallas_call(
        _l0_prompt_kernel,
        grid=(bsz, seq // tm),
        in_specs=[
            pl.BlockSpec((1, tm, dm), lambda b, l: (b, l, 0)),
            const(1, dm),
            pl.BlockSpec((dm, 4 * width), lambda b, l: (0, 0), pipeline_mode=pl.Buffered(1)),
            const(*lb_logits.shape),
            const(1, width),
        ],
        out_specs=[
            pl.BlockSpec((1, tm, width), lambda b, l: (b, l, 0)),
            pl.BlockSpec((1, n_heads, d, d), lambda b, l: (b, 0, 0, 0)),
        ],
        out_shape=[
            jax.ShapeDtypeStruct((bsz, seq, width), BF16),
            jax.ShapeDtypeStruct((bsz, n_heads, d, d), F32),
        ],
        scratch_shapes=[
            pltpu.VMEM((n_heads, d, d), F32),
            pltpu.VMEM((tm, width), F32),
            pltpu.VMEM((tm, width), F32),
            pltpu.VMEM((tm, width), F32),
            pltpu.VMEM((tm, width), BF16),
            pltpu.VMEM((tm, width), F32),
        ],
        compiler_params=pltpu.CompilerParams(
            dimension_semantics=("arbitrary", "arbitrary"), vmem_limit_bytes=VMEM_LIMIT_BYTES),
        name="l0_prompt",
    )(x, pre_g, w_in, lb_logits, onorm)


def _epilogue_kernel(og_ref, h_ref, p_ref, wo_ref, pg_ref, wpe_ref, wpg_ref, out_ref):
    mix = _dot(og_ref[...].astype(BF16), wo_ref[...])
    hn = h_ref[...] + _rms(mix) * pg_ref[...]
    pe = _dot(p_ref[...].astype(BF16), wpe_ref[...])
    gate = _sigmoid(_dot(hn.astype(BF16), wpg_ref[...]))
    out_ref[...] = hn + pe * gate


def _epilogue(og, h, p_all, layer, w_out, post_g, w_pe, w_pg, tm):
    t, dm = h.shape
    width = og.shape[1]
    ple = p_all.shape[2]
    tm = min(tm, t)
    const = lambda *shape: pl.BlockSpec(shape, lambda i: (0,) * len(shape))
    return pl.pallas_call(
        _epilogue_kernel,
        grid=(t // tm,),
        in_specs=[
            pl.BlockSpec((tm, width), lambda i: (i, 0)),
            pl.BlockSpec((tm, dm), lambda i: (i, 0)),
            pl.BlockSpec((None, tm, ple), lambda i: (layer, i, 0)),
            const(width, dm), const(1, dm), const(ple, dm), const(dm, dm),
        ],
        out_specs=pl.BlockSpec((tm, dm), lambda i: (i, 0)),
        out_shape=jax.ShapeDtypeStruct((t, dm), F32),
        compiler_params=pltpu.CompilerParams(
            dimension_semantics=("arbitrary",), vmem_limit_bytes=VMEM_LIMIT_BYTES),
        name="epilogue",
    )(og, h, p_all, w_out, post_g, w_pe, w_pg)


def _l1_prompt_kernel(sinks_ref, h_ref, g1_ref, kvg_ref, wkv_ref, win_ref, cos_ref, slo_ref, shi_ref,
                      og_ref, kc_ref, vc_ref, kx_scr, vx_scr, q_scr, z_scr):
    l = pl.program_id(1)
    tm = h_ref.shape[1]
    width = z_scr.shape[1]
    kvw = kx_scr.shape[1]
    pair_w = 2 * HEAD_DIM
    n_pairs = width // pair_w
    pairs_per_kv = Q_PER_KV // 2

    @pl.when(l == 0)
    def _():
        kx_scr[0:WINDOW, :] = jnp.zeros((WINDOW, kvw), kx_scr.dtype)
        vx_scr[0:WINDOW, :] = jnp.zeros((WINDOW, kvw), vx_scr.dtype)

    @pl.when(l > 0)
    def _():
        kx_scr[0:WINDOW, :] = kx_scr[tm:tm + WINDOW, :]
        vx_scr[0:WINDOW, :] = vx_scr[tm:tm + WINDOW, :]

    h0 = _rms(h_ref[0])
    xn = (h0 * g1_ref[...]).astype(BF16)
    hk = (h0 * kvg_ref[...]).astype(BF16)
    cos, slo, shi = cos_ref[...], slo_ref[...], shi_ref[...]
    tile = lambda t, n: jnp.concatenate([t] * n, axis=1)

    kv = _dot(hk, wkv_ref[...])
    k_rot = _rope(kv[:, :kvw], tile(cos, kvw // pair_w), tile(slo, kvw // pair_w), tile(shi, kvw // pair_w))
    v_new = kv[:, kvw:]
    kx_scr[WINDOW:WINDOW + tm, :] = k_rot.astype(kx_scr.dtype)
    vx_scr[WINDOW:WINDOW + tm, :] = v_new.astype(vx_scr.dtype)

    @pl.when(l == pl.num_programs(1) - 1)
    def _():
        kc_ref[0] = k_rot[tm - WINDOW:, :]
        vc_ref[0] = v_new[tm - WINDOW:, :]

    q = _rope(_dot(xn, win_ref[:, 0:width]), tile(cos, n_pairs), tile(slo, n_pairs), tile(shi, n_pairs))
    q_scr[...] = (q * (ATTN_SCALE * LOG2_E)).astype(q_scr.dtype)
    z_scr[...] = _silu(_dot(xn, win_ref[:, width:2 * width]))

    qi = lax.broadcasted_iota(jnp.int32, (WINDOW, 2 * WINDOW), 0)
    ki = lax.broadcasted_iota(jnp.int32, (WINDOW, 2 * WINDOW), 1)
    band = (ki > qi) & (ki <= qi + WINDOW)
    lane = lax.broadcasted_iota(jnp.int32, (WINDOW, pair_w), 1)

    def group_body(g, carry):
        kcol = pl.multiple_of(g * pair_w, pair_w)
        cols = [pl.multiple_of((g * pairs_per_kv + p) * pair_w, pair_w) for p in range(pairs_per_kv)]
        for c in range(tm // WINDOW):
            rows = slice(c * WINDOW, (c + 1) * WINDOW)
            keys = slice(c * WINDOW, (c + 2) * WINDOW)
            k_bd = _block_diag2(kx_scr[keys, pl.ds(kcol, pair_w)])
            v_bd = _block_diag2(vx_scr[keys, pl.ds(kcol, pair_w)])
            q_all = jnp.concatenate([q_scr[rows, pl.ds(col, pair_w)] for col in cols], axis=0)
            s_all = _dot_nt(q_all, k_bd)
            mask = band & (ki >= jnp.where(l == 0, WINDOW, 0)) if c == 0 else band
            probs, inv = [], []
            for p in range(pairs_per_kv):
                pair_probs = []
                for hh in range(2):
                    sink = sinks_ref[(g * pairs_per_kv + p) * 2 + hh] * LOG2_E
                    s = s_all[p * WINDOW:(p + 1) * WINDOW, hh * 2 * WINDOW:(hh + 1) * 2 * WINDOW]
                    s = jnp.where(mask, s, MASKED)
                    mx = jnp.maximum(jnp.max(s, axis=-1, keepdims=True), sink)
                    e = jnp.exp2(s - mx)
                    inv.append(1.0 / (jnp.sum(e, axis=-1, keepdims=True) + jnp.exp2(sink - mx)))
                    pair_probs.append(e.astype(BF16))
                probs.append(jnp.concatenate(pair_probs, axis=1))
            o_all = _dot(jnp.concatenate(probs, axis=0), v_bd)
            for p, col in enumerate(cols):
                o2 = o_all[p * WINDOW:(p + 1) * WINDOW] * jnp.where(lane < HEAD_DIM, inv[2 * p], inv[2 * p + 1])
                og_ref[0, rows, pl.ds(col, pair_w)] = (o2 * z_scr[rows, pl.ds(col, pair_w)]).astype(og_ref.dtype)
        return carry

    lax.fori_loop(0, n_pairs // pairs_per_kv, group_body, 0)


def _l1_prompt(h, g1, kvg, w_kv_dup, w_in, sinks, tables, tm=PROMPT_TILE):
    bsz, seq, dm = h.shape
    width = w_in.shape[1] // 2
    kvw = w_kv_dup.shape[1] // 2
    pair_w = 2 * HEAD_DIM
    const = lambda *shape: pl.BlockSpec(shape, lambda b, l: (0,) * len(shape))
    table = pl.BlockSpec((tm, pair_w), lambda b, l: (l, 0))
    return pl.pallas_call(
        _l1_prompt_kernel,
        grid=(bsz, seq // tm),
        in_specs=[
            pl.BlockSpec(memory_space=pltpu.SMEM),
            pl.BlockSpec((1, tm, dm), lambda b, l: (b, l, 0)),
            const(1, dm), const(1, dm), const(dm, 2 * kvw), const(dm, 2 * width),
            table, table, table,
        ],
        out_specs=[
            pl.BlockSpec((1, tm, width), lambda b, l: (b, l, 0)),
            pl.BlockSpec((1, WINDOW, kvw), lambda b, l: (b, 0, 0)),
            pl.BlockSpec((1, WINDOW, kvw), lambda b, l: (b, 0, 0)),
        ],
        out_shape=[
            jax.ShapeDtypeStruct((bsz, seq, width), BF16),
            jax.ShapeDtypeStruct((bsz, WINDOW, kvw), F32),
            jax.ShapeDtypeStruct((bsz, WINDOW, kvw), F32),
        ],
        scratch_shapes=[
            pltpu.VMEM((WINDOW + tm, kvw), BF16),
            pltpu.VMEM((WINDOW + tm, kvw), BF16),
            pltpu.VMEM((tm, width), BF16),
            pltpu.VMEM((tm, width), F32),
        ],
        compiler_params=pltpu.CompilerParams(
            dimension_semantics=("arbitrary", "arbitrary"), vmem_limit_bytes=VMEM_LIMIT_BYTES),
        name="l1_prompt",
    )(sinks, h, g1, kvg, w_kv_dup, w_in, *tables)


def _norm_matmul_kernel(x_ref, g_ref, w_ref, o_ref):
    xn = (_rms(x_ref[...]) * g_ref[...]).astype(BF16)
    o_ref[...] = _dot(xn, w_ref[...])


def _norm_matmul(x, g, w, tn=2048):
    t, dm = x.shape
    n = w.shape[1]
    tn = min(tn, n)
    return pl.pallas_call(
        _norm_matmul_kernel,
        grid=(n // tn,),
        in_specs=[
            pl.BlockSpec((t, dm), lambda j: (0, 0)),
            pl.BlockSpec((1, dm), lambda j: (0, 0)),
            pl.BlockSpec((dm, tn), lambda j: (0, j)),
        ],
        out_specs=pl.BlockSpec((t, tn), lambda j: (0, j)),
        out_shape=jax.ShapeDtypeStruct((t, n), F32),
        compiler_params=pltpu.CompilerParams(
            dimension_semantics=("arbitrary",), vmem_limit_bytes=VMEM_LIMIT_BYTES),
        name="norm_matmul",
    )(x, g, w)


def _l0_sample_kernel(u_ref, lbl_ref, on_ref, s0_ref, s1_ref, og_ref):
    bt = u_ref.shape[0]
    width = og_ref.shape[1]
    d = HGRN_HEAD_DIM
    lb_all = _lower_bound(lbl_ref[...])

    for h in range(width // d):
        cols = slice(h * d, (h + 1) * d)
        lb = lb_all[:, cols]
        q = _silu(u_ref[:, h * d:(h + 1) * d])
        f = lb + (1.0 - lb) * _sigmoid(u_ref[:, width + h * d:width + (h + 1) * d])
        v = u_ref[:, 2 * width + h * d:2 * width + (h + 1) * d]
        z = _silu(u_ref[:, 3 * width + h * d:3 * width + (h + 1) * d])
        f_t = f.T
        q_b = q.astype(BF16)
        outs = []
        for j in range(bt):
            v_row = v[j:j + 1, :]
            s_new = v_row + f_t[:, j:j + 1] * (s0_ref[j, h] - v_row)
            s1_ref[j, h] = s_new
            outs.append(_dot(q_b[j:j + 1, :], s_new.astype(BF16)))
        o = jnp.concatenate(outs, axis=0)
        og_ref[:, cols] = _rms(o) * on_ref[:, cols] * z


def _l0_sample(u, lb_logits, onorm, state, bt=8):
    n, heads, d, _ = state.shape
    width = heads * d
    return pl.pallas_call(
        _l0_sample_kernel,
        grid=(n // bt,),
        in_specs=[
            pl.BlockSpec((bt, 4 * width), lambda i: (i, 0)),
            pl.BlockSpec(lb_logits.shape, lambda i: (0, 0)),
            pl.BlockSpec((1, width), lambda i: (0, 0)),
            pl.BlockSpec((bt, heads, d, d), lambda i: (i, 0, 0, 0)),
        ],
        out_specs=[
            pl.BlockSpec((bt, heads, d, d), lambda i: (i, 0, 0, 0)),
            pl.BlockSpec((bt, width), lambda i: (i, 0)),
        ],
        out_shape=[
            jax.ShapeDtypeStruct(state.shape, F32),
            jax.ShapeDtypeStruct((n, width), F32),
        ],
        compiler_params=pltpu.CompilerParams(
            dimension_semantics=("arbitrary",), vmem_limit_bytes=VMEM_LIMIT_BYTES),
        name="l0_sample",
    )(u, lb_logits, onorm, state)


def _l1_sample_front_kernel(h_ref, g1_ref, kvg_ref, win_ref, wkv_ref, cos_ref, slo_ref, shi_ref,
                            q_ref, z_ref, k_ref, v_ref):
    width = q_ref.shape[1]
    kvw = k_ref.shape[1]
    h0 = _rms(h_ref[...])
    xn = (h0 * g1_ref[...]).astype(BF16)
    hk = (h0 * kvg_ref[...]).astype(BF16)
    cos, slo, shi = cos_ref[...], slo_ref[...], shi_ref[...]
    q = _rope(_dot(xn, win_ref[:, 0:width]), cos, slo, shi)
    q_ref[...] = q * ATTN_SCALE
    z_ref[...] = _silu(_dot(xn, win_ref[:, width:2 * width]))
    kv = _dot(hk, wkv_ref[...])
    k_ref[...] = _rope(kv[:, :kvw], cos[:, :kvw], slo[:, :kvw], shi[:, :kvw])
    v_ref[...] = kv[:, kvw:]


def _l1_sample_front(h, g1, kvg, w_in, w_kv, tables):
    t, dm = h.shape
    width = w_in.shape[1] // 2
    kvw = w_kv.shape[1] // 2
    full = lambda a: pl.BlockSpec(a.shape, lambda i: (0,) * a.ndim)
    args = (h, g1, kvg, w_in, w_kv, *tables)
    return pl.pallas_call(
        _l1_sample_front_kernel,
        grid=(1,),
        in_specs=[full(a) for a in args],
        out_specs=[pl.BlockSpec((t, width), lambda i: (0, 0)), pl.BlockSpec((t, width), lambda i: (0, 0)),
                   pl.BlockSpec((t, kvw), lambda i: (0, 0)), pl.BlockSpec((t, kvw), lambda i: (0, 0))],
        out_shape=[jax.ShapeDtypeStruct((t, width), F32), jax.ShapeDtypeStruct((t, width), F32),
                   jax.ShapeDtypeStruct((t, kvw), F32), jax.ShapeDtypeStruct((t, kvw), F32)],
        compiler_params=pltpu.CompilerParams(
            dimension_semantics=("arbitrary",), vmem_limit_bytes=VMEM_LIMIT_BYTES),
        name="l1_sample_front",
    )(*args)


def _l1_sample_attn_kernel(q_ref, z_ref, kn_ref, vn_ref, ckt_ref, cvt_ref, sink_ref, og_ref):
    bt, kv_heads, hd, wb = ckt_ref.shape
    n_q = q_ref.shape[1]
    kvw = kv_heads * hd
    valid = lax.broadcasted_iota(jnp.int32, (n_q, wb), 1) > wb - WINDOW
    head_kv = lax.broadcasted_iota(jnp.int32, (n_q, kvw), 0) // Q_PER_KV
    own_lanes = lax.broadcasted_iota(jnp.int32, (n_q, kvw), 1) // hd == head_kv
    row_kv = lax.broadcasted_iota(jnp.int32, (n_q, hd), 0) // Q_PER_KV
    sink = sink_ref[...]
    for j in range(bt):
        q = q_ref[j]
        q_bd = jnp.where(own_lanes, jnp.concatenate([q] * kv_heads, axis=1), 0.0)
        s = jnp.where(valid, _dot(q_bd.astype(BF16), ckt_ref[j].reshape(kvw, wb).astype(BF16)), MASKED)
        s_new = jnp.sum(q_bd * kn_ref[j:j + 1, :], axis=-1, keepdims=True)
        mx = jnp.maximum(jnp.maximum(jnp.max(s, axis=-1, keepdims=True), s_new), sink)
        e = jnp.exp(s - mx)
        e_new = jnp.exp(s_new - mx)
        den = jnp.sum(e, axis=-1, keepdims=True) + e_new + jnp.exp(sink - mx)
        o_all = _dot_nt(e.astype(BF16), cvt_ref[j].reshape(kvw, wb).astype(BF16)) + e_new * vn_ref[j:j + 1, :]
        o = jnp.zeros((n_q, hd), F32)
        for g in range(kv_heads):
            o = o + jnp.where(row_kv == g, o_all[:, g * hd:(g + 1) * hd], 0.0)
        og_ref[j] = o / den * z_ref[j]


def _l1_sample_attn(q3, z3, k_new, v_new, cache_kt, cache_vt, sinks_col, bt=8):
    n, heads, hd = q3.shape
    kv_heads, wb = cache_kt.shape[1], cache_kt.shape[3]
    qspec = pl.BlockSpec((bt, heads, hd), lambda i: (i, 0, 0))
    nspec = pl.BlockSpec((bt, kv_heads * hd), lambda i: (i, 0))
    cspec = pl.BlockSpec((bt, kv_heads, hd, wb), lambda i: (i, 0, 0, 0))
    return pl.pallas_call(
        _l1_sample_attn_kernel,
        grid=(n // bt,),
        in_specs=[qspec, qspec, nspec, nspec, cspec, cspec, pl.BlockSpec((heads, 1), lambda i: (0, 0))],
        out_specs=qspec,
        out_shape=jax.ShapeDtypeStruct((n, heads, hd), F32),
        compiler_params=pltpu.CompilerParams(
            dimension_semantics=("arbitrary",), vmem_limit_bytes=VMEM_LIMIT_BYTES),
        name="l1_sample_attn",
    )(q3, z3, k_new, v_new, cache_kt, cache_vt, sinks_col)


def kernel(x_prompt, x_sample, p_prompt, p_sample, state_hgrn, cache_k, cache_v, pre_norm_g, post_norm_g,
           w_in_a, lb_logits, onorm_a, w_out_a, kv_norm_g, w_kv, w_in_b, sinks, w_out_b, w_pe, w_pg):
    bsz, seq, dm = x_prompt.shape
    n_s = x_sample.shape[0]
    assert x_sample.shape[1] == 1 and w_in_a.shape[0] == 1 and w_in_b.shape[0] == 1
    assert seq % PROMPT_TILE == 0 and n_s % 8 == 0
    kv_heads = w_kv.shape[1] // (2 * HEAD_DIM)
    n_q = w_in_b.shape[2] // (2 * HEAD_DIM)
    assert n_q == kv_heads * Q_PER_KV

    row = lambda a: a.reshape(1, -1)
    bf = lambda a: a.astype(BF16)
    w_in_a0, w_out_a0 = bf(w_in_a[0]), bf(w_out_a[0])
    w_in_b0, w_out_b0 = bf(w_in_b[0]), bf(w_out_b[0])
    w_pe_b, w_pg_b = bf(w_pe), bf(w_pg)
    w_kv_b = bf(w_kv)
    dup = lambda w: jnp.concatenate([w.reshape(dm, kv_heads, 1, HEAD_DIM)] * 2, axis=2).reshape(dm, -1)
    kvw = kv_heads * HEAD_DIM
    w_kv_dup = jnp.concatenate([dup(w_kv_b[:, :kvw]), dup(w_kv_b[:, kvw:])], axis=1)
    undup = lambda a: a.reshape(a.shape[0], a.shape[1], kv_heads, 2, HEAD_DIM)[:, :, :, 0, :]

    pp = p_prompt.reshape(p_prompt.shape[0], bsz * seq, -1)
    ps = p_sample.reshape(p_sample.shape[0], n_s, -1)

    og0, st_p = _l0_prompt(x_prompt, row(pre_norm_g[0]), w_in_a0, lb_logits, row(onorm_a[0]))
    h1 = _epilogue(og0.reshape(bsz * seq, -1), x_prompt.reshape(bsz * seq, dm), pp, 0,
                   w_out_a0, row(post_norm_g[0]), w_pe_b[0], w_pg_b[0], tm=512)
    tables_p = _rope_tables(jnp.arange(seq), 2)
    og1, k_p, v_p = _l1_prompt(h1.reshape(bsz, seq, dm), row(pre_norm_g[1]), row(kv_norm_g), w_kv_dup, w_in_b0,
                               sinks[0], tables_p)
    y_p = _epilogue(og1.reshape(bsz * seq, -1), h1, pp, 1,
                    w_out_b0, row(post_norm_g[1]), w_pe_b[1], w_pg_b[1], tm=512)

    xs = x_sample.reshape(n_s, dm)
    u0 = _norm_matmul(xs, row(pre_norm_g[0]), w_in_a0)
    st_s, og0s = _l0_sample(u0, lb_logits, row(onorm_a[0]), state_hgrn[0])
    h1s = _epilogue(og0s, xs, ps, 0,
                    w_out_a0, row(post_norm_g[0]), w_pe_b[0], w_pg_b[0], tm=n_s)
    tables_s = _rope_tables(PAST_LEN + jnp.arange(1), n_q)
    q_s, z_s, k_s, v_s = _l1_sample_front(h1s, row(pre_norm_g[1]), row(kv_norm_g), w_in_b0, w_kv_b, tables_s)
    og1s = _l1_sample_attn(q_s.reshape(n_s, n_q, HEAD_DIM), z_s.reshape(n_s, n_q, HEAD_DIM), k_s, v_s,
                           cache_k.transpose(0, 2, 3, 1), cache_v.transpose(0, 2, 3, 1),
                           sinks[0].reshape(n_q, 1))
    y_s = _epilogue(og1s.reshape(n_s, -1), h1s, ps, 1,
                    w_out_b0, row(post_norm_g[1]), w_pe_b[1], w_pg_b[1], tm=n_s)

    return (y_p.reshape(bsz, seq, dm), y_s.reshape(n_s, 1, dm),
            st_p[None], st_s[None],
            undup(k_p), undup(v_p),
            k_s.reshape(n_s, 1, kv_heads, HEAD_DIM), v_s.reshape(n_s, 1, kv_heads, HEAD_DIM))
```

```python
import functools

import jax
import jax.numpy as jnp
from jax import lax
from jax.experimental import pallas as pl
from jax.experimental.pallas import tpu as pltpu

F32 = jnp.float32
BF16 = jnp.bfloat16

EPS = 1e-6
HGRN_HEAD_DIM = 128
HEAD_DIM = 64
Q_PER_KV = 8
WINDOW = 128
ROT_DIM = 16
ROPE_THETA = 500000.0
PAST_LEN = 16384
ATTN_SCALE = HEAD_DIM ** -0.5
LOG2_E = 1.4426950408889634

CHUNK = 128
LEVEL_HALVES = (64, 32, 16, 8, 4, 2, 1)
FAST_TAIL = 32
PROJ_COLS = 512
PROMPT_TILE = 512
EPILOGUE_TILE = 1024
FAST_MAX_EXPONENT = 40.0
MASKED = -1e30
VMEM_LIMIT_BYTES = 56 * 1024 * 1024


def _sigmoid(x):
    return 1.0 / (1.0 + jnp.exp(-x))


def _silu(x):
    return x * _sigmoid(x)


def _rms(x):
    return x * lax.rsqrt(jnp.mean(x * x, axis=-1, keepdims=True) + EPS)


def _dot(a, b):
    return jnp.dot(a, b, preferred_element_type=F32)


def _dot_nt(a, b):
    return lax.dot_general(a, b, (((1,), (1,)), ((), ())), preferred_element_type=F32)


def _dot_tn(a, b):
    return lax.dot_general(a, b, (((0,), (0,)), ((), ())), preferred_element_type=F32)


def _block_diag2(a):
    c = a.shape[1] // 2
    lane = lax.broadcasted_iota(jnp.int32, a.shape, 1)
    zero = jnp.zeros_like(a)
    return jnp.concatenate([jnp.where(lane < c, a, zero), jnp.where(lane >= c, a, zero)], axis=0)


def _rope(x, cos, sin_lo, sin_hi):
    n = x.shape[-1]
    half = ROT_DIM // 2
    return x * cos + pltpu.roll(x, n - half, 1) * sin_lo + pltpu.roll(x, half, 1) * sin_hi


def _rope_tables(pos, reps):
    half = ROT_DIM // 2
    inv = ROPE_THETA ** (-jnp.arange(0, ROT_DIM, 2, dtype=F32) / ROT_DIM)
    ang = pos.astype(F32)[:, None] * inv[None, :]
    cos, sin = jnp.cos(ang), jnp.sin(ang)
    n = pos.shape[0]
    pad = jnp.zeros((n, HEAD_DIM - ROT_DIM), F32)
    zero = jnp.zeros((n, half), F32)
    c = jnp.concatenate([cos, cos, pad + 1.0], axis=1)
    s_lo = jnp.concatenate([-sin, zero, pad], axis=1)
    s_hi = jnp.concatenate([zero, sin, pad], axis=1)
    return tuple(jnp.tile(t, (1, reps)) for t in (c, s_lo, s_hi))


def _lower_bound(lbl):
    m = jnp.max(lbl, axis=0, keepdims=True)
    e = jnp.exp(lbl - m)
    return e[0:1] / jnp.sum(e, axis=0, keepdims=True)


def _level_ref(b, half, row):
    if half >= 8:
        return _block_rows(b, half - 1, 2 * half)
    n = b.shape[0]
    up = lambda x, d: pltpu.roll(x, n - d, 0)
    down = lambda x, d: pltpu.roll(x, d, 0)
    if half == 1:
        return jnp.where(row % 2 == 1, down(b, 1), b)
    y1 = jnp.where(row % 2 == 0, up(b, 1), b)
    if half == 2:
        return jnp.where(row % 4 >= 2, down(y1, 2), y1)
    y2 = jnp.where(row % 4 < 2, up(y1, 2), y1)
    return jnp.where(row % 8 >= 4, down(y2, 4), y2)


def _block_rows(b, offset, size):
    pieces = [jnp.broadcast_to(b[s + offset:s + offset + 1, :], (size, b.shape[1]))
              for s in range(0, b.shape[0], size)]
    return pieces[0] if len(pieces) == 1 else jnp.concatenate(pieces, axis=0)


def _block_diag2_aligned(a):
    c = a.shape[1] // 2
    zero = jnp.zeros((a.shape[0], c), a.dtype)
    return jnp.concatenate([jnp.concatenate([a[:, :c], zero], axis=1),
                            jnp.concatenate([zero, a[:, c:]], axis=1)], axis=0)


def _score_masks(c, w, halves, tail):
    d = w // 2
    row = lax.broadcasted_iota(jnp.int32, (c, w), 0)
    col_tok = lax.broadcasted_iota(jnp.int32, (c, w), 1) % d
    one = lambda cond: jnp.where(cond, 1.0, 0.0).astype(F32)
    masks = {"diag": one(row == col_tok)}
    for half in halves:
        masks[half] = one(row // (2 * half) == col_tok // (2 * half))
    if tail is not None:
        masks["tail"] = one((row // tail == col_tok // tail) & (row > col_tok))
    return masks


def _chunk_scores(q, k, q_bf, k_bf, b, halves, tail, masks):
    c, w = q.shape
    d = w // 2
    qk = q * k
    diag = jnp.concatenate(
        [jnp.broadcast_to(jnp.sum(qk[:, :d], axis=-1, keepdims=True), (c, d)),
         jnp.broadcast_to(jnp.sum(qk[:, d:], axis=-1, keepdims=True), (c, d))], axis=1)
    a = diag * masks["diag"]
    for half in halves:
        if half >= 16:
            delta = b - _block_rows(b, half - 1, 2 * half)
            zero = jnp.zeros((half, w), BF16)
            qe, ke = [], []
            for s in range(0, c, 2 * half):
                lo, up = slice(s, s + half), slice(s + half, s + 2 * half)
                ke += [k_bf[lo] * jnp.exp2(-delta[lo]).astype(BF16), zero]
                qe += [zero, q_bf[up] * jnp.exp2(delta[up]).astype(BF16)]
            qe, ke = jnp.concatenate(qe, axis=0), jnp.concatenate(ke, axis=0)
        else:
            row = lax.broadcasted_iota(jnp.int32, (c, w), 0)
            upper = row % (2 * half) >= half
            ref = _level_ref(b, half, row)
            e = jnp.exp2(jnp.where(upper, b - ref, ref - b))
            qe = jnp.where(upper, q * e, 0.0).astype(BF16)
            ke = jnp.where(upper, 0.0, k * e).astype(BF16)
        p = _dot_nt(qe, _block_diag2_aligned(ke))
        a = a + (p if 2 * half == c else p * masks[half])
    if tail is not None:
        delta = b - _block_rows(b, tail // 2 - 1, tail)
        qe = q_bf * jnp.exp2(delta).astype(BF16)
        ke = k_bf * jnp.exp2(-delta).astype(BF16)
        a = a + _dot_nt(qe, _block_diag2_aligned(ke)) * masks["tail"]
    return a


def _chunk_apply(q_bf, k_bf, b, v, a, st_a, st_b):
    c, w = b.shape
    d = w // 2
    st_bd = _block_diag2_aligned(jnp.concatenate([st_a, st_b], axis=1).astype(BF16))
    o = _dot_nt(q_bf * jnp.exp2(b).astype(BF16), st_bd) + _dot(a.astype(BF16), _block_diag2_aligned(v))
    b_last = b[c - 1:c, :]
    k_hat = k_bf * jnp.exp2(b_last - b).astype(BF16)
    ds = _dot_tn(v, k_hat)
    decay = jnp.exp2(b_last)
    st_a = st_a * decay[:, :d] + ds[:d, :d]
    st_b = st_b * decay[:, d:] + ds[d:, d:]
    return o, st_a, st_b


def _l0_prompt_kernel(x_ref, pg_ref, w_ref, lbl_ref, on_ref, og_ref, st_ref,
                      s_scr, q_scr, k_scr, b_scr, v_scr, z_scr):
    l = pl.program_id(1)
    tm = x_ref.shape[1]
    width = q_scr.shape[1]
    d = HGRN_HEAD_DIM
    n_heads = width // d

    @pl.when(l == 0)
    def _():
        s_scr[...] = jnp.zeros_like(s_scr)

    xn = (_rms(x_ref[0]) * pg_ref[...]).astype(BF16)
    ti = lax.broadcasted_iota(jnp.int32, (CHUNK, CHUNK), 0)
    tj = lax.broadcasted_iota(jnp.int32, (CHUNK, CHUNK), 1)
    tri = jnp.where(tj <= ti, 1.0, 0.0).astype(BF16)
    tri2 = jnp.concatenate([tri, tri], axis=1)
    worst = jnp.zeros((1, PROJ_COLS), F32)
    for j in range(width // PROJ_COLS):
        cols = slice(j * PROJ_COLS, (j + 1) * PROJ_COLS)
        part = lambda i: w_ref[:, i * width + j * PROJ_COLS:i * width + (j + 1) * PROJ_COLS]
        lb = _lower_bound(lbl_ref[:, cols])
        q_scr[:, cols] = _silu(_dot(xn, part(0)))
        f = lb + (1.0 - lb) * _sigmoid(_dot(xn, part(1)))
        k_scr[:, cols] = 1.0 - f
        v_scr[:, cols] = _dot(xn, part(2)).astype(BF16)
        z_scr[:, cols] = _silu(_dot(xn, part(3))) * on_ref[:, cols]
        g = jnp.log2(f)
        g_hi = g.astype(BF16)
        r1 = g - g_hi.astype(F32)
        g_mid = r1.astype(BF16)
        g_lo = (r1 - g_mid.astype(F32)).astype(BF16)
        for c in range(tm // CHUNK):
            rows = slice(c * CHUNK, (c + 1) * CHUNK)
            b = _dot(tri2, jnp.concatenate([g_hi[rows], g_mid[rows]], axis=0)) + _dot(tri, g_lo[rows])
            b_scr[rows, cols] = b
            for s in range(0, CHUNK, FAST_TAIL):
                mid = b[s + FAST_TAIL // 2 - 1:s + FAST_TAIL // 2, :]
                worst = jnp.maximum(
                    worst, jnp.maximum(b[s:s + 1, :] - mid, mid - b[s + FAST_TAIL - 1:s + FAST_TAIL, :]))
    fast = jnp.max(worst) <= FAST_MAX_EXPONENT * LOG2_E

    def scan(halves, tail, unroll):
        masks = _score_masks(CHUNK, 2 * d, halves, tail)
        chunks = [slice(c * CHUNK, (c + 1) * CHUNK) for c in range(tm // CHUNK)]

        def pair_body(pi, carry):
            col = pl.multiple_of(pi * 2 * d, 2 * d)
            load = lambda ref, rows: ref[rows, pl.ds(col, 2 * d)]
            st_a = s_scr[2 * pi]
            st_b = s_scr[2 * pi + 1]
            operands = []
            for rows in chunks:
                q, k, b = load(q_scr, rows), load(k_scr, rows), load(b_scr, rows)
                q_bf, k_bf = q.astype(BF16), k.astype(BF16)
                operands.append((q_bf, k_bf, b, _chunk_scores(q, k, q_bf, k_bf, b, halves, tail, masks)))
            for rows, (q_bf, k_bf, b, a) in zip(chunks, operands):
                o, st_a, st_b = _chunk_apply(q_bf, k_bf, b, load(v_scr, rows), a, st_a, st_b)
                on = jnp.concatenate([_rms(o[:, :d]), _rms(o[:, d:])], axis=1)
                og_ref[0, rows, pl.ds(col, 2 * d)] = (on * load(z_scr, rows)).astype(og_ref.dtype)
            s_scr[2 * pi] = st_a
            s_scr[2 * pi + 1] = st_b
            return carry

        lax.fori_loop(0, n_heads // 2, pair_body, 0, unroll=unroll)

    @pl.when(fast)
    def _():
        scan(tuple(h for h in LEVEL_HALVES if h >= FAST_TAIL), FAST_TAIL, unroll=4)

    @pl.when(jnp.logical_not(fast))
    def _():
        scan(LEVEL_HALVES, None, unroll=1)

    @pl.when(l == pl.num_programs(1) - 1)
    def _():
        for h in range(n_heads):
            st_ref[0, h] = s_scr[h].T


def _l0_prompt(x, pre_g, w_in, lb_logits, onorm, tm=PROMPT_TILE):
    bsz, seq, dm = x.shape
    width = w_in.shape[1] // 4
    n_heads = width // HGRN_HEAD_DIM
    d = HGRN_HEAD_DIM
    const = lambda *shape: pl.BlockSpec(shape, lambda b, l: (0,) * len(shape))
    return pl.pallas_call(
        _l0_prompt_kernel,
        grid=(bsz, seq // tm),
        in_specs=[
            pl.BlockSpec((1, tm, dm), lambda b, l: (b, l, 0)),
            const(1, dm),
            pl.BlockSpec((dm, 4 * width), lambda b, l: (0, 0), pipeline_mode=pl.Buffered(1)),
            const(*lb_logits.shape),
            const(1, width),
        ],
        out_specs=[
            pl.BlockSpec((1, tm, width), lambda b, l: (b, l, 0)),
            pl.BlockSpec((1, n_heads, d, d), lambda b, l: (b, 0, 0, 0)),
        ],
        out_shape=[
            jax.ShapeDtypeStruct((bsz, seq, width), BF16),
            jax.ShapeDtypeStruct((bsz, n_heads, d, d), F32),
        ],
        scratch_shapes=[
            pltpu.VMEM((n_heads, d, d), F32),
            pltpu.VMEM((tm, width), F32),
            pltpu.VMEM((tm, width), F32),
            pltpu.VMEM((tm, width), F32),
            pltpu.VMEM((tm, width), BF16),
            pltpu.VMEM((tm, width), F32),
        ],
        compiler_params=pltpu.CompilerParams(
            dimension_semantics=("arbitrary", "arbitrary"), vmem_limit_bytes=VMEM_LIMIT_BYTES),
        name="l0_prompt",
    )(x, pre_g, w_in, lb_logits, onorm)


def _epilogue_kernel(og_ref, h_ref, p_ref, wo_ref, pg_ref, wpe_ref, wpg_ref, out_ref):
    mix = _dot(og_ref[...].astype(BF16), wo_ref[...])
    hn = h_ref[...] + _rms(mix) * pg_ref[...]
    pe = _dot(p_ref[...].astype(BF16), wpe_ref[...])
    gate = _sigmoid(_dot(hn.astype(BF16), wpg_ref[...]))
    out_ref[...] = hn + pe * gate


def _epilogue(og, h, p_all, layer, w_out, post_g, w_pe, w_pg, tm):
    t, dm = h.shape
    width = og.shape[1]
    ple = p_all.shape[2]
    tm = min(tm, t)
    const = lambda *shape: pl.BlockSpec(shape, lambda i: (0,) * len(shape))
    return pl.pallas_call(
        _epilogue_kernel,
        grid=(t // tm,),
        in_specs=[
            pl.BlockSpec((tm, width), lambda i: (i, 0)),
            pl.BlockSpec((tm, dm), lambda i: (i, 0)),
            pl.BlockSpec((None, tm, ple), lambda i: (layer, i, 0)),
            const(width, dm), const(1, dm), const(ple, dm), const(dm, dm),
        ],
        out_specs=pl.BlockSpec((tm, dm), lambda i: (i, 0)),
        out_shape=jax.ShapeDtypeStruct((t, dm), F32),
        compiler_params=pltpu.CompilerParams(
            dimension_semantics=("arbitrary",), vmem_limit_bytes=VMEM_LIMIT_BYTES),
        name="epilogue",
    )(og, h, p_all, w_out, post_g, w_pe, w_pg)


def _dup_heads(x):
    pair_w = 2 * HEAD_DIM
    lane = lax.broadcasted_iota(jnp.int32, (x.shape[0], pair_w), 1)
    out = []
    for c in range(x.shape[1] // pair_w):
        both = x[:, c * pair_w:(c + 1) * pair_w]
        swapped = pltpu.roll(both, HEAD_DIM, 1)
        out += [jnp.where(lane < HEAD_DIM, both, swapped), jnp.where(lane < HEAD_DIM, swapped, both)]
    return jnp.concatenate(out, axis=1)


def _l1_prompt_kernel(sinks_ref, h_ref, g1_ref, kvg_ref, wkv_ref, win_ref, cos_ref, slo_ref, shi_ref,
                      og_ref, kc_ref, vc_ref, kx_scr, vx_scr, q_scr, z_scr):
    l = pl.program_id(1)
    tm = h_ref.shape[1]
    width = z_scr.shape[1]
    kvw = kx_scr.shape[1]
    kv_cols = kc_ref.shape[2]
    pair_w = 2 * HEAD_DIM
    n_pairs = width // pair_w
    pairs_per_kv = Q_PER_KV // 2

    @pl.when(l == 0)
    def _():
        kx_scr[0:WINDOW, :] = jnp.zeros((WINDOW, kvw), kx_scr.dtype)
        vx_scr[0:WINDOW, :] = jnp.zeros((WINDOW, kvw), vx_scr.dtype)

    @pl.when(l > 0)
    def _():
        kx_scr[0:WINDOW, :] = kx_scr[tm:tm + WINDOW, :]
        vx_scr[0:WINDOW, :] = vx_scr[tm:tm + WINDOW, :]

    h0 = _rms(h_ref[0])
    xn = (h0 * g1_ref[...]).astype(BF16)
    hk = (h0 * kvg_ref[...]).astype(BF16)
    cos, slo, shi = cos_ref[...], slo_ref[...], shi_ref[...]
    tile = lambda t, n: jnp.concatenate([t] * n, axis=1)

    kv = _dot(hk, wkv_ref[...])
    reps = kv_cols // pair_w
    k_rot = _rope(kv[:, :kv_cols], tile(cos, reps), tile(slo, reps), tile(shi, reps))
    v_new = kv[:, kv_cols:]
    kx_scr[WINDOW:WINDOW + tm, :] = _dup_heads(k_rot).astype(kx_scr.dtype)
    vx_scr[WINDOW:WINDOW + tm, :] = _dup_heads(v_new).astype(vx_scr.dtype)

    @pl.when(l == pl.num_programs(1) - 1)
    def _():
        kc_ref[0] = k_rot[tm - WINDOW:, :]
        vc_ref[0] = v_new[tm - WINDOW:, :]

    q = _rope(_dot(xn, win_ref[:, 0:width]), tile(cos, n_pairs), tile(slo, n_pairs), tile(shi, n_pairs))
    q_scr[...] = (q * (ATTN_SCALE * LOG2_E)).astype(q_scr.dtype)
    z_scr[...] = _silu(_dot(xn, win_ref[:, width:2 * width]))

    qi = lax.broadcasted_iota(jnp.int32, (WINDOW, 2 * WINDOW), 0)
    ki = lax.broadcasted_iota(jnp.int32, (WINDOW, 2 * WINDOW), 1)
    band = (ki > qi) & (ki <= qi + WINDOW)
    first_head = lax.broadcasted_iota(jnp.int32, (WINDOW, pair_w), 1) < HEAD_DIM

    def group_body(g, carry):
        kcol = pl.multiple_of(g * pair_w, pair_w)
        cols = [pl.multiple_of((g * pairs_per_kv + p) * pair_w, pair_w) for p in range(pairs_per_kv)]
        for c in range(tm // WINDOW):
            rows = slice(c * WINDOW, (c + 1) * WINDOW)
            keys = slice(c * WINDOW, (c + 2) * WINDOW)
            k_bd = _block_diag2(kx_scr[keys, pl.ds(kcol, pair_w)])
            v_bd = _block_diag2(vx_scr[keys, pl.ds(kcol, pair_w)])
            q_all = jnp.concatenate([q_scr[rows, pl.ds(col, pair_w)] for col in cols], axis=0)
            s_all = _dot_nt(q_all, k_bd)
            mask = band & (ki >= jnp.where(l == 0, WINDOW, 0)) if c == 0 else band
            probs, inv = [], []
            for p in range(pairs_per_kv):
                pair_probs, maxes, sums, pair_sinks = [], [], [], []
                for hh in range(2):
                    sink = sinks_ref[(g * pairs_per_kv + p) * 2 + hh] * LOG2_E
                    s = s_all[p * WINDOW:(p + 1) * WINDOW, hh * 2 * WINDOW:(hh + 1) * 2 * WINDOW]
                    s = jnp.where(mask, s, MASKED)
                    mx = jnp.maximum(jnp.max(s, axis=-1, keepdims=True), sink)
                    e = jnp.exp2(s - mx)
                    pair_probs.append(e.astype(BF16))
                    maxes.append(mx)
                    sums.append(jnp.sum(e, axis=-1, keepdims=True))
                    pair_sinks.append(sink)
                probs.append(jnp.concatenate(pair_probs, axis=1))
                mx2 = jnp.where(first_head, maxes[0], maxes[1])
                sum2 = jnp.where(first_head, sums[0], sums[1])
                sink2 = jnp.where(first_head, pair_sinks[0], pair_sinks[1])
                inv.append(1.0 / (sum2 + jnp.exp2(sink2 - mx2)))
            o_all = _dot(jnp.concatenate(probs, axis=0), v_bd)
            for p, col in enumerate(cols):
                o2 = o_all[p * WINDOW:(p + 1) * WINDOW] * inv[p]
                og_ref[0, rows, pl.ds(col, pair_w)] = (o2 * z_scr[rows, pl.ds(col, pair_w)]).astype(og_ref.dtype)
        return carry

    lax.fori_loop(0, n_pairs // pairs_per_kv, group_body, 0, unroll=2)


def _l1_prompt(h, g1, kvg, w_kv, w_in, sinks, tables, tm=PROMPT_TILE):
    bsz, seq, dm = h.shape
    width = w_in.shape[1] // 2
    kv_cols = w_kv.shape[1] // 2
    kvw = 2 * kv_cols
    pair_w = 2 * HEAD_DIM
    const = lambda *shape: pl.BlockSpec(shape, lambda b, l: (0,) * len(shape))
    table = pl.BlockSpec((tm, pair_w), lambda b, l: (l, 0))
    return pl.pallas_call(
        _l1_prompt_kernel,
        grid=(bsz, seq // tm),
        in_specs=[
            pl.BlockSpec(memory_space=pltpu.SMEM),
            pl.BlockSpec((1, tm, dm), lambda b, l: (b, l, 0)),
            const(1, dm), const(1, dm), const(dm, 2 * kv_cols), const(dm, 2 * width),
            table, table, table,
        ],
        out_specs=[
            pl.BlockSpec((1, tm, width), lambda b, l: (b, l, 0)),
            pl.BlockSpec((1, WINDOW, kv_cols), lambda b, l: (b, 0, 0)),
            pl.BlockSpec((1, WINDOW, kv_cols), lambda b, l: (b, 0, 0)),
        ],
        out_shape=[
            jax.ShapeDtypeStruct((bsz, seq, width), BF16),
            jax.ShapeDtypeStruct((bsz, WINDOW, kv_cols), F32),
            jax.ShapeDtypeStruct((bsz, WINDOW, kv_cols), F32),
        ],
        scratch_shapes=[
            pltpu.VMEM((WINDOW + tm, kvw), BF16),
            pltpu.VMEM((WINDOW + tm, kvw), BF16),
            pltpu.VMEM((tm, width), BF16),
            pltpu.VMEM((tm, width), F32),
        ],
        compiler_params=pltpu.CompilerParams(
            dimension_semantics=("arbitrary", "arbitrary"), vmem_limit_bytes=VMEM_LIMIT_BYTES),
        name="l1_prompt",
    )(sinks, h, g1, kvg, w_kv, w_in, *tables)


def _norm_matmul_kernel(x_ref, g_ref, w_ref, o_ref):
    xn = (_rms(x_ref[...]) * g_ref[...]).astype(BF16)
    o_ref[...] = _dot(xn, w_ref[...])


def _norm_matmul(x, g, w, tn=2048):
    t, dm = x.shape
    n = w.shape[1]
    tn = min(tn, n)
    return pl.pallas_call(
        _norm_matmul_kernel,
        grid=(n // tn,),
        in_specs=[
            pl.BlockSpec((t, dm), lambda j: (0, 0)),
            pl.BlockSpec((1, dm), lambda j: (0, 0)),
            pl.BlockSpec((dm, tn), lambda j: (0, j)),
        ],
        out_specs=pl.BlockSpec((t, tn), lambda j: (0, j)),
        out_shape=jax.ShapeDtypeStruct((t, n), F32),
        compiler_params=pltpu.CompilerParams(
            dimension_semantics=("arbitrary",), vmem_limit_bytes=VMEM_LIMIT_BYTES),
        name="norm_matmul",
    )(x, g, w)


def _l0_sample_kernel(u_ref, lbl_ref, on_ref, s0_ref, s1_ref, og_ref):
    bt = u_ref.shape[0]
    width = og_ref.shape[1]
    d = HGRN_HEAD_DIM
    lb_all = _lower_bound(lbl_ref[...])

    for h in range(width // d):
        cols = slice(h * d, (h + 1) * d)
        lb = lb_all[:, cols]
        q = _silu(u_ref[:, h * d:(h + 1) * d])
        f = lb + (1.0 - lb) * _sigmoid(u_ref[:, width + h * d:width + (h + 1) * d])
        v = u_ref[:, 2 * width + h * d:2 * width + (h + 1) * d]
        z = _silu(u_ref[:, 3 * width + h * d:3 * width + (h + 1) * d])
        f_t = f.T
        q_b = q.astype(BF16)
        outs = []
        for j in range(bt):
            v_row = v[j:j + 1, :]
            s_new = v_row + f_t[:, j:j + 1] * (s0_ref[j, h] - v_row)
            s1_ref[j, h] = s_new
            outs.append(_dot(q_b[j:j + 1, :], s_new.astype(BF16)))
        o = jnp.concatenate(outs, axis=0)
        og_ref[:, cols] = _rms(o) * on_ref[:, cols] * z


def _l0_sample(u, lb_logits, onorm, state, bt=8):
    n, heads, d, _ = state.shape
    width = heads * d
    return pl.pallas_call(
        _l0_sample_kernel,
        grid=(n // bt,),
        in_specs=[
            pl.BlockSpec((bt, 4 * width), lambda i: (i, 0)),
            pl.BlockSpec(lb_logits.shape, lambda i: (0, 0)),
            pl.BlockSpec((1, width), lambda i: (0, 0)),
            pl.BlockSpec((bt, heads, d, d), lambda i: (i, 0, 0, 0)),
        ],
        out_specs=[
            pl.BlockSpec((bt, heads, d, d), lambda i: (i, 0, 0, 0)),
            pl.BlockSpec((bt, width), lambda i: (i, 0)),
        ],
        out_shape=[
            jax.ShapeDtypeStruct(state.shape, F32),
            jax.ShapeDtypeStruct((n, width), F32),
        ],
        compiler_params=pltpu.CompilerParams(
            dimension_semantics=("arbitrary",), vmem_limit_bytes=VMEM_LIMIT_BYTES),
        name="l0_sample",
    )(u, lb_logits, onorm, state)


def _l1_sample_front_kernel(h_ref, g1_ref, kvg_ref, win_ref, wkv_ref, cos_ref, slo_ref, shi_ref,
                            q_ref, z_ref, k_ref, v_ref):
    width = q_ref.shape[1]
    kvw = k_ref.shape[1]
    h0 = _rms(h_ref[...])
    xn = (h0 * g1_ref[...]).astype(BF16)
    hk = (h0 * kvg_ref[...]).astype(BF16)
    cos, slo, shi = cos_ref[...], slo_ref[...], shi_ref[...]
    q = _rope(_dot(xn, win_ref[:, 0:width]), cos, slo, shi)
    q_ref[...] = q * ATTN_SCALE
    z_ref[...] = _silu(_dot(xn, win_ref[:, width:2 * width]))
    kv = _dot(hk, wkv_ref[...])
    k_ref[...] = _rope(kv[:, :kvw], cos[:, :kvw], slo[:, :kvw], shi[:, :kvw])
    v_ref[...] = kv[:, kvw:]


def _l1_sample_front(h, g1, kvg, w_in, w_kv, tables):
    t, dm = h.shape
    width = w_in.shape[1] // 2
    kvw = w_kv.shape[1] // 2
    full = lambda a: pl.BlockSpec(a.shape, lambda i: (0,) * a.ndim)
    args = (h, g1, kvg, w_in, w_kv, *tables)
    return pl.pallas_call(
        _l1_sample_front_kernel,
        grid=(1,),
        in_specs=[full(a) for a in args],
        out_specs=[pl.BlockSpec((t, width), lambda i: (0, 0)), pl.BlockSpec((t, width), lambda i: (0, 0)),
                   pl.BlockSpec((t, kvw), lambda i: (0, 0)), pl.BlockSpec((t, kvw), lambda i: (0, 0))],
        out_shape=[jax.ShapeDtypeStruct((t, width), F32), jax.ShapeDtypeStruct((t, width), F32),
                   jax.ShapeDtypeStruct((t, kvw), F32), jax.ShapeDtypeStruct((t, kvw), F32)],
        compiler_params=pltpu.CompilerParams(
            dimension_semantics=("arbitrary",), vmem_limit_bytes=VMEM_LIMIT_BYTES),
        name="l1_sample_front",
    )(*args)


def _l1_sample_attn_kernel(q_ref, z_ref, kn_ref, vn_ref, ckt_ref, cvt_ref, sink_ref, og_ref):
    bt, kv_heads, hd, wb = ckt_ref.shape
    n_q = q_ref.shape[1]
    kvw = kv_heads * hd
    valid = lax.broadcasted_iota(jnp.int32, (n_q, wb), 1) > wb - WINDOW
    head_kv = lax.broadcasted_iota(jnp.int32, (n_q, kvw), 0) // Q_PER_KV
    own_lanes = lax.broadcasted_iota(jnp.int32, (n_q, kvw), 1) // hd == head_kv
    row_kv = lax.broadcasted_iota(jnp.int32, (n_q, hd), 0) // Q_PER_KV
    sink = sink_ref[...]
    for j in range(bt):
        q = q_ref[j]
        q_bd = jnp.where(own_lanes, jnp.concatenate([q] * kv_heads, axis=1), 0.0)
        s = jnp.where(valid, _dot(q_bd.astype(BF16), ckt_ref[j].reshape(kvw, wb).astype(BF16)), MASKED)
        s_new = jnp.sum(q_bd * kn_ref[j:j + 1, :], axis=-1, keepdims=True)
        mx = jnp.maximum(jnp.maximum(jnp.max(s, axis=-1, keepdims=True), s_new), sink)
        e = jnp.exp(s - mx)
        e_new = jnp.exp(s_new - mx)
        den = jnp.sum(e, axis=-1, keepdims=True) + e_new + jnp.exp(sink - mx)
        o_all = _dot_nt(e.astype(BF16), cvt_ref[j].reshape(kvw, wb).astype(BF16)) + e_new * vn_ref[j:j + 1, :]
        o = jnp.zeros((n_q, hd), F32)
        for g in range(kv_heads):
            o = o + jnp.where(row_kv == g, o_all[:, g * hd:(g + 1) * hd], 0.0)
        og_ref[j] = o / den * z_ref[j]


def _l1_sample_attn(q3, z3, k_new, v_new, cache_kt, cache_vt, sinks_col, bt=8):
    n, heads, hd = q3.shape
    kv_heads, wb = cache_kt.shape[1], cache_kt.shape[3]
    qspec = pl.BlockSpec((bt, heads, hd), lambda i: (i, 0, 0))
    nspec = pl.BlockSpec((bt, kv_heads * hd), lambda i: (i, 0))
    cspec = pl.BlockSpec((bt, kv_heads, hd, wb), lambda i: (i, 0, 0, 0))
    return pl.pallas_call(
        _l1_sample_attn_kernel,
        grid=(n // bt,),
        in_specs=[qspec, qspec, nspec, nspec, cspec, cspec, pl.BlockSpec((heads, 1), lambda i: (0, 0))],
        out_specs=qspec,
        out_shape=jax.ShapeDtypeStruct((n, heads, hd), F32),
        compiler_params=pltpu.CompilerParams(
            dimension_semantics=("arbitrary",), vmem_limit_bytes=VMEM_LIMIT_BYTES),
        name="l1_sample_attn",
    )(q3, z3, k_new, v_new, cache_kt, cache_vt, sinks_col)


def kernel(x_prompt, x_sample, p_prompt, p_sample, state_hgrn, cache_k, cache_v, pre_norm_g, post_norm_g,
           w_in_a, lb_logits, onorm_a, w_out_a, kv_norm_g, w_kv, w_in_b, sinks, w_out_b, w_pe, w_pg):
    bsz, seq, dm = x_prompt.shape
    n_s = x_sample.shape[0]
    assert x_sample.shape[1] == 1 and w_in_a.shape[0] == 1 and w_in_b.shape[0] == 1
    assert seq % PROMPT_TILE == 0 and n_s % 8 == 0
    kv_heads = w_kv.shape[1] // (2 * HEAD_DIM)
    n_q = w_in_b.shape[2] // (2 * HEAD_DIM)
    assert n_q == kv_heads * Q_PER_KV

    row = lambda a: a.reshape(1, -1)
    bf = lambda a: a.astype(BF16)
    w_in_a0, w_out_a0 = bf(w_in_a[0]), bf(w_out_a[0])
    w_in_b0, w_out_b0 = bf(w_in_b[0]), bf(w_out_b[0])
    w_pe_b, w_pg_b = bf(w_pe), bf(w_pg)
    w_kv_b = bf(w_kv)

    pp = p_prompt.reshape(p_prompt.shape[0], bsz * seq, -1)
    ps = p_sample.reshape(p_sample.shape[0], n_s, -1)

    og0, st_p = _l0_prompt(x_prompt, row(pre_norm_g[0]), w_in_a0, lb_logits, row(onorm_a[0]))
    h1 = _epilogue(og0.reshape(bsz * seq, -1), x_prompt.reshape(bsz * seq, dm), pp, 0,
                   w_out_a0, row(post_norm_g[0]), w_pe_b[0], w_pg_b[0], tm=EPILOGUE_TILE)
    tables_p = _rope_tables(jnp.arange(seq), 2)
    og1, k_p, v_p = _l1_prompt(h1.reshape(bsz, seq, dm), row(pre_norm_g[1]), row(kv_norm_g), w_kv_b, w_in_b0,
                               sinks[0], tables_p)
    y_p = _epilogue(og1.reshape(bsz * seq, -1), h1, pp, 1,
                    w_out_b0, row(post_norm_g[1]), w_pe_b[1], w_pg_b[1], tm=EPILOGUE_TILE)

    xs = x_sample.reshape(n_s, dm)
    u0 = _norm_matmul(xs, row(pre_norm_g[0]), w_in_a0)
    st_s, og0s = _l0_sample(u0, lb_logits, row(onorm_a[0]), state_hgrn[0])
    h1s = _epilogue(og0s, xs, ps, 0,
                    w_out_a0, row(post_norm_g[0]), w_pe_b[0], w_pg_b[0], tm=n_s)
    tables_s = _rope_tables(PAST_LEN + jnp.arange(1), n_q)
    q_s, z_s, k_s, v_s = _l1_sample_front(h1s, row(pre_norm_g[1]), row(kv_norm_g), w_in_b0, w_kv_b, tables_s)
    og1s = _l1_sample_attn(q_s.reshape(n_s, n_q, HEAD_DIM), z_s.reshape(n_s, n_q, HEAD_DIM), k_s, v_s,
                           cache_k.transpose(0, 2, 3, 1), cache_v.transpose(0, 2, 3, 1),
                           sinks[0].reshape(n_q, 1))
    y_s = _epilogue(og1s.reshape(n_s, -1), h1s, ps, 1,
                    w_out_b0, row(post_norm_g[1]), w_pe_b[1], w_pg_b[1], tm=n_s)

    return (y_p.reshape(bsz, seq, dm), y_s.reshape(n_s, 1, dm),
            st_p[None], st_s[None],
            k_p.reshape(bsz, WINDOW, kv_heads, HEAD_DIM), v_p.reshape(bsz, WINDOW, kv_heads, HEAD_DIM),
            k_s.reshape(n_s, 1, kv_heads, HEAD_DIM), v_s.reshape(n_s, 1, kv_heads, HEAD_DIM))
```

```python
import functools

import jax
import jax.numpy as jnp
from jax import lax
from jax.experimental import pallas as pl
from jax.experimental.pallas import tpu as pltpu

F32 = jnp.float32
BF16 = jnp.bfloat16

EPS = 1e-6
HGRN_HEAD_DIM = 128
HEAD_DIM = 64
Q_PER_KV = 8
WINDOW = 128
ROT_DIM = 16
ROPE_THETA = 500000.0
PAST_LEN = 16384
ATTN_SCALE = HEAD_DIM ** -0.5
LOG2_E = 1.4426950408889634

CHUNK = 128
LEVEL_HALVES = (64, 32, 16, 8, 4, 2, 1)
FAST_TAIL = 32
PROJ_COLS = 512
PROMPT_TILE = 512
EPILOGUE_TILE = 1024
FAST_MAX_EXPONENT = 40.0
MASKED = -1e30
VMEM_LIMIT_BYTES = 56 * 1024 * 1024


def _sigmoid(x):
    return 1.0 / (1.0 + jnp.exp(-x))


def _silu(x):
    return x * _sigmoid(x)


def _rms(x):
    return x * lax.rsqrt(jnp.mean(x * x, axis=-1, keepdims=True) + EPS)


def _dot(a, b):
    return jnp.dot(a, b, preferred_element_type=F32)


def _dot_nt(a, b):
    return lax.dot_general(a, b, (((1,), (1,)), ((), ())), preferred_element_type=F32)


def _dot_tn(a, b):
    return lax.dot_general(a, b, (((0,), (0,)), ((), ())), preferred_element_type=F32)


def _block_diag2(a):
    c = a.shape[1] // 2
    lane = lax.broadcasted_iota(jnp.int32, a.shape, 1)
    zero = jnp.zeros_like(a)
    return jnp.concatenate([jnp.where(lane < c, a, zero), jnp.where(lane >= c, a, zero)], axis=0)


def _rope(x, cos, sin_lo, sin_hi):
    n = x.shape[-1]
    half = ROT_DIM // 2
    return x * cos + pltpu.roll(x, n - half, 1) * sin_lo + pltpu.roll(x, half, 1) * sin_hi


def _rope_tables(pos, reps):
    half = ROT_DIM // 2
    inv = ROPE_THETA ** (-jnp.arange(0, ROT_DIM, 2, dtype=F32) / ROT_DIM)
    ang = pos.astype(F32)[:, None] * inv[None, :]
    cos, sin = jnp.cos(ang), jnp.sin(ang)
    n = pos.shape[0]
    pad = jnp.zeros((n, HEAD_DIM - ROT_DIM), F32)
    zero = jnp.zeros((n, half), F32)
    c = jnp.concatenate([cos, cos, pad + 1.0], axis=1)
    s_lo = jnp.concatenate([-sin, zero, pad], axis=1)
    s_hi = jnp.concatenate([zero, sin, pad], axis=1)
    return tuple(jnp.tile(t, (1, reps)) for t in (c, s_lo, s_hi))


def _lower_bound(lbl):
    m = jnp.max(lbl, axis=0, keepdims=True)
    e = jnp.exp(lbl - m)
    return e[0:1] / jnp.sum(e, axis=0, keepdims=True)


def _level_ref(b, half, row):
    if half >= 8:
        return _block_rows(b, half - 1, 2 * half)
    n = b.shape[0]
    up = lambda x, d: pltpu.roll(x, n - d, 0)
    down = lambda x, d: pltpu.roll(x, d, 0)
    if half == 1:
        return jnp.where(row % 2 == 1, down(b, 1), b)
    y1 = jnp.where(row % 2 == 0, up(b, 1), b)
    if half == 2:
        return jnp.where(row % 4 >= 2, down(y1, 2), y1)
    y2 = jnp.where(row % 4 < 2, up(y1, 2), y1)
    return jnp.where(row % 8 >= 4, down(y2, 4), y2)


def _block_rows(b, offset, size):
    pieces = [jnp.broadcast_to(b[s + offset:s + offset + 1, :], (size, b.shape[1]))
              for s in range(0, b.shape[0], size)]
    return pieces[0] if len(pieces) == 1 else jnp.concatenate(pieces, axis=0)


def _block_diag2_aligned(a):
    c = a.shape[1] // 2
    zero = jnp.zeros((a.shape[0], c), a.dtype)
    return jnp.concatenate([jnp.concatenate([a[:, :c], zero], axis=1),
                            jnp.concatenate([zero, a[:, c:]], axis=1)], axis=0)


def _score_masks(c, w, halves, tail):
    d = w // 2
    row = lax.broadcasted_iota(jnp.int32, (c, w), 0)
    col_tok = lax.broadcasted_iota(jnp.int32, (c, w), 1) % d
    one = lambda cond: jnp.where(cond, 1.0, 0.0).astype(F32)
    masks = {"diag": one(row == col_tok)}
    for half in halves:
        masks[half] = one(row // (2 * half) == col_tok // (2 * half))
    if tail is not None:
        masks["tail"] = one((row // tail == col_tok // tail) & (row > col_tok))
    return masks


def _chunk_scores(q, k, q_bf, k_bf, b, halves, tail, masks):
    c, w = q.shape
    d = w // 2
    qk = q * k
    diag = jnp.concatenate(
        [jnp.broadcast_to(jnp.sum(qk[:, :d], axis=-1, keepdims=True), (c, d)),
         jnp.broadcast_to(jnp.sum(qk[:, d:], axis=-1, keepdims=True), (c, d))], axis=1)
    a = diag * masks["diag"]
    for half in halves:
        if half >= 16:
            delta = b - _block_rows(b, half - 1, 2 * half)
            zero = jnp.zeros((half, w), BF16)
            qe, ke = [], []
            for s in range(0, c, 2 * half):
                lo, up = slice(s, s + half), slice(s + half, s + 2 * half)
                ke += [k_bf[lo] * jnp.exp2(-delta[lo]).astype(BF16), zero]
                qe += [zero, q_bf[up] * jnp.exp2(delta[up]).astype(BF16)]
            qe, ke = jnp.concatenate(qe, axis=0), jnp.concatenate(ke, axis=0)
        else:
            row = lax.broadcasted_iota(jnp.int32, (c, w), 0)
            upper = row % (2 * half) >= half
            ref = _level_ref(b, half, row)
            e = jnp.exp2(jnp.where(upper, b - ref, ref - b))
            qe = jnp.where(upper, q * e, 0.0).astype(BF16)
            ke = jnp.where(upper, 0.0, k * e).astype(BF16)
        p = _dot_nt(qe, _block_diag2_aligned(ke))
        a = a + (p if 2 * half == c else p * masks[half])
    if tail is not None:
        delta = b - _block_rows(b, tail // 2 - 1, tail)
        qe = q_bf * jnp.exp2(delta).astype(BF16)
        ke = k_bf * jnp.exp2(-delta).astype(BF16)
        a = a + _dot_nt(qe, _block_diag2_aligned(ke)) * masks["tail"]
    return a


def _chunk_apply(q_bf, k_bf, b, v, a, st_a, st_b):
    c, w = b.shape
    d = w // 2
    st_bd = _block_diag2_aligned(jnp.concatenate([st_a, st_b], axis=1).astype(BF16))
    o = _dot_nt(q_bf * jnp.exp2(b).astype(BF16), st_bd) + _dot(a.astype(BF16), _block_diag2_aligned(v))
    b_last = b[c - 1:c, :]
    k_hat = k_bf * jnp.exp2(b_last - b).astype(BF16)
    ds = _dot_tn(v, k_hat)
    decay = jnp.exp2(b_last)
    st_a = st_a * decay[:, :d] + ds[:d, :d]
    st_b = st_b * decay[:, d:] + ds[d:, d:]
    return o, st_a, st_b


def _l0_prompt_kernel(x_ref, pg_ref, w_ref, lbl_ref, on_ref, og_ref, st_ref,
                      s_scr, q_scr, k_scr, b_scr, v_scr, z_scr):
    l = pl.program_id(1)
    tm = x_ref.shape[1]
    width = q_scr.shape[1]
    d = HGRN_HEAD_DIM
    n_heads = width // d

    @pl.when(l == 0)
    def _():
        s_scr[...] = jnp.zeros_like(s_scr)

    xn = (_rms(x_ref[0]) * pg_ref[...]).astype(BF16)
    ti = lax.broadcasted_iota(jnp.int32, (CHUNK, CHUNK), 0)
    tj = lax.broadcasted_iota(jnp.int32, (CHUNK, CHUNK), 1)
    tri = jnp.where(tj <= ti, 1.0, 0.0).astype(BF16)
    tri2 = jnp.concatenate([tri, tri], axis=1)
    worst = jnp.zeros((1, PROJ_COLS), F32)
    for j in range(width // PROJ_COLS):
        cols = slice(j * PROJ_COLS, (j + 1) * PROJ_COLS)
        part = lambda i: w_ref[:, i * width + j * PROJ_COLS:i * width + (j + 1) * PROJ_COLS]
        lb = _lower_bound(lbl_ref[:, cols])
        q_scr[:, cols] = _silu(_dot(xn, part(0)))
        f = lb + (1.0 - lb) * _sigmoid(_dot(xn, part(1)))
        k_scr[:, cols] = 1.0 - f
        v_scr[:, cols] = _dot(xn, part(2)).astype(BF16)
        z_scr[:, cols] = _silu(_dot(xn, part(3))) * on_ref[:, cols]
        g = jnp.log2(f)
        g_hi = g.astype(BF16)
        r1 = g - g_hi.astype(F32)
        g_mid = r1.astype(BF16)
        g_lo = (r1 - g_mid.astype(F32)).astype(BF16)
        for c in range(tm // CHUNK):
            rows = slice(c * CHUNK, (c + 1) * CHUNK)
            b = _dot(tri2, jnp.concatenate([g_hi[rows], g_mid[rows]], axis=0)) + _dot(tri, g_lo[rows])
            b_scr[rows, cols] = b
            for s in range(0, CHUNK, FAST_TAIL):
                mid = b[s + FAST_TAIL // 2 - 1:s + FAST_TAIL // 2, :]
                worst = jnp.maximum(
                    worst, jnp.maximum(b[s:s + 1, :] - mid, mid - b[s + FAST_TAIL - 1:s + FAST_TAIL, :]))
    fast = jnp.max(worst) <= FAST_MAX_EXPONENT * LOG2_E

    def scan(halves, tail, group):
        masks = _score_masks(CHUNK, 2 * d, halves, tail)
        chunks = [slice(c * CHUNK, (c + 1) * CHUNK) for c in range(tm // CHUNK)]

        def group_body(gi, carry):
            pairs = [gi * group + p for p in range(group)]
            cols = [pl.multiple_of(pi * 2 * d, 2 * d) for pi in pairs]
            load = lambda ref, rows, col: ref[rows, pl.ds(col, 2 * d)]
            states = [(s_scr[2 * pi], s_scr[2 * pi + 1]) for pi in pairs]
            operands = {}
            for n, rows in enumerate(chunks):
                for p, col in enumerate(cols):
                    q, k, b = load(q_scr, rows, col), load(k_scr, rows, col), load(b_scr, rows, col)
                    q_bf, k_bf = q.astype(BF16), k.astype(BF16)
                    operands[p, n] = (q_bf, k_bf, b, _chunk_scores(q, k, q_bf, k_bf, b, halves, tail, masks))
            for n, rows in enumerate(chunks):
                for p, col in enumerate(cols):
                    q_bf, k_bf, b, a = operands[p, n]
                    o, st_a, st_b = _chunk_apply(q_bf, k_bf, b, load(v_scr, rows, col), a, *states[p])
                    states[p] = (st_a, st_b)
                    on = jnp.concatenate([_rms(o[:, :d]), _rms(o[:, d:])], axis=1)
                    og_ref[0, rows, pl.ds(col, 2 * d)] = (on * load(z_scr, rows, col)).astype(og_ref.dtype)
            for pi, (st_a, st_b) in zip(pairs, states):
                s_scr[2 * pi] = st_a
                s_scr[2 * pi + 1] = st_b
            return carry

        lax.fori_loop(0, n_heads // 2 // group, group_body, 0)

    @pl.when(fast)
    def _():
        scan(tuple(h for h in LEVEL_HALVES if h >= FAST_TAIL), FAST_TAIL, group=4)

    @pl.when(jnp.logical_not(fast))
    def _():
        scan(LEVEL_HALVES, None, group=1)

    @pl.when(l == pl.num_programs(1) - 1)
    def _():
        for h in range(n_heads):
            st_ref[0, h] = s_scr[h].T


def _l0_prompt(x, pre_g, w_in, lb_logits, onorm, tm=PROMPT_TILE):
    bsz, seq, dm = x.shape
    width = w_in.shape[1] // 4
    n_heads = width // HGRN_HEAD_DIM
    d = HGRN_HEAD_DIM
    const = lambda *shape: pl.BlockSpec(shape, lambda b, l: (0,) * len(shape))
    return pl.pallas_call(
        _l0_prompt_kernel,
        grid=(bsz, seq // tm),
        in_specs=[
            pl.BlockSpec((1, tm, dm), lambda b, l: (b, l, 0)),
            const(1, dm),
            pl.BlockSpec((dm, 4 * width), lambda b, l: (0, 0), pipeline_mode=pl.Buffered(1)),
            const(*lb_logits.shape),
            const(1, width),
        ],
        out_specs=[
            pl.BlockSpec((1, tm, width), lambda b, l: (b, l, 0)),
            pl.BlockSpec((1, n_heads, d, d), lambda b, l: (b, 0, 0, 0)),
        ],
        out_shape=[
            jax.ShapeDtypeStruct((bsz, seq, width), BF16),
            jax.ShapeDtypeStruct((bsz, n_heads, d, d), F32),
        ],
        scratch_shapes=[
            pltpu.VMEM((n_heads, d, d), F32),
            pltpu.VMEM((tm, width), F32),
            pltpu.VMEM((tm, width), F32),
            pltpu.VMEM((tm, width), F32),
            pltpu.VMEM((tm, width), BF16),
            pltpu.VMEM((tm, width), F32),
        ],
        compiler_params=pltpu.CompilerParams(
            dimension_semantics=("arbitrary", "arbitrary"), vmem_limit_bytes=VMEM_LIMIT_BYTES),
        name="l0_prompt",
    )(x, pre_g, w_in, lb_logits, onorm)


def _epilogue_kernel(og_ref, h_ref, p_ref, wo_ref, pg_ref, wpe_ref, wpg_ref, out_ref):
    mix = _dot(og_ref[...].astype(BF16), wo_ref[...])
    hn = h_ref[...] + _rms(mix) * pg_ref[...]
    pe = _dot(p_ref[...].astype(BF16), wpe_ref[...])
    gate = _sigmoid(_dot(hn.astype(BF16), wpg_ref[...]))
    out_ref[...] = hn + pe * gate


def _epilogue(og, h, p_all, layer, w_out, post_g, w_pe, w_pg, tm):
    t, dm = h.shape
    width = og.shape[1]
    ple = p_all.shape[2]
    tm = min(tm, t)
    const = lambda *shape: pl.BlockSpec(shape, lambda i: (0,) * len(shape))
    return pl.pallas_call(
        _epilogue_kernel,
        grid=(t // tm,),
        in_specs=[
            pl.BlockSpec((tm, width), lambda i: (i, 0)),
            pl.BlockSpec((tm, dm), lambda i: (i, 0)),
            pl.BlockSpec((None, tm, ple), lambda i: (layer, i, 0)),
            const(width, dm), const(1, dm), const(ple, dm), const(dm, dm),
        ],
        out_specs=pl.BlockSpec((tm, dm), lambda i: (i, 0)),
        out_shape=jax.ShapeDtypeStruct((t, dm), F32),
        compiler_params=pltpu.CompilerParams(
            dimension_semantics=("arbitrary",), vmem_limit_bytes=VMEM_LIMIT_BYTES),
        name="epilogue",
    )(og, h, p_all, w_out, post_g, w_pe, w_pg)


def _dup_heads(x):
    pair_w = 2 * HEAD_DIM
    lane = lax.broadcasted_iota(jnp.int32, (x.shape[0], pair_w), 1)
    out = []
    for c in range(x.shape[1] // pair_w):
        both = x[:, c * pair_w:(c + 1) * pair_w]
        swapped = pltpu.roll(both, HEAD_DIM, 1)
        out += [jnp.where(lane < HEAD_DIM, both, swapped), jnp.where(lane < HEAD_DIM, swapped, both)]
    return jnp.concatenate(out, axis=1)


def _l1_prompt_kernel(sinks_ref, h_ref, g1_ref, kvg_ref, wkv_ref, win_ref, cos_ref, slo_ref, shi_ref,
                      og_ref, kc_ref, vc_ref, kx_scr, vx_scr, q_scr, z_scr):
    l = pl.program_id(1)
    tm = h_ref.shape[1]
    width = z_scr.shape[1]
    kvw = kx_scr.shape[1]
    kv_cols = kc_ref.shape[2]
    pair_w = 2 * HEAD_DIM
    n_pairs = width // pair_w
    pairs_per_kv = Q_PER_KV // 2

    @pl.when(l == 0)
    def _():
        kx_scr[0:WINDOW, :] = jnp.zeros((WINDOW, kvw), kx_scr.dtype)
        vx_scr[0:WINDOW, :] = jnp.zeros((WINDOW, kvw), vx_scr.dtype)

    @pl.when(l > 0)
    def _():
        kx_scr[0:WINDOW, :] = kx_scr[tm:tm + WINDOW, :]
        vx_scr[0:WINDOW, :] = vx_scr[tm:tm + WINDOW, :]

    h0 = _rms(h_ref[0])
    xn = (h0 * g1_ref[...]).astype(BF16)
    hk = (h0 * kvg_ref[...]).astype(BF16)
    cos, slo, shi = cos_ref[...], slo_ref[...], shi_ref[...]
    tile = lambda t, n: jnp.concatenate([t] * n, axis=1)

    q = _rope(_dot(xn, win_ref[:, 0:width]), tile(cos, n_pairs), tile(slo, n_pairs), tile(shi, n_pairs))
    q_scr[...] = (q * (ATTN_SCALE * LOG2_E)).astype(q_scr.dtype)
    z_scr[...] = _silu(_dot(xn, win_ref[:, width:2 * width]))

    kv = _dot(hk, wkv_ref[...])
    reps = kv_cols // pair_w
    k_rot = _rope(kv[:, :kv_cols], tile(cos, reps), tile(slo, reps), tile(shi, reps))
    v_new = kv[:, kv_cols:]
    kx_scr[WINDOW:WINDOW + tm, :] = _dup_heads(k_rot).astype(kx_scr.dtype)
    vx_scr[WINDOW:WINDOW + tm, :] = _dup_heads(v_new).astype(vx_scr.dtype)

    @pl.when(l == pl.num_programs(1) - 1)
    def _():
        kc_ref[0] = k_rot[tm - WINDOW:, :]
        vc_ref[0] = v_new[tm - WINDOW:, :]

    qi = lax.broadcasted_iota(jnp.int32, (WINDOW, 2 * WINDOW), 0)
    ki = lax.broadcasted_iota(jnp.int32, (WINDOW, 2 * WINDOW), 1)
    band = (ki > qi) & (ki <= qi + WINDOW)
    first_head = lax.broadcasted_iota(jnp.int32, (WINDOW, pair_w), 1) < HEAD_DIM

    def group_body(g, carry):
        kcol = pl.multiple_of(g * pair_w, pair_w)
        cols = [pl.multiple_of((g * pairs_per_kv + p) * pair_w, pair_w) for p in range(pairs_per_kv)]
        for c in range(tm // WINDOW):
            rows = slice(c * WINDOW, (c + 1) * WINDOW)
            keys = slice(c * WINDOW, (c + 2) * WINDOW)
            k_bd = _block_diag2(kx_scr[keys, pl.ds(kcol, pair_w)])
            v_bd = _block_diag2(vx_scr[keys, pl.ds(kcol, pair_w)])
            q_all = jnp.concatenate([q_scr[rows, pl.ds(col, pair_w)] for col in cols], axis=0)
            s_all = _dot_nt(q_all, k_bd)
            mask = band & (ki >= jnp.where(l == 0, WINDOW, 0)) if c == 0 else band
            probs, inv = [], []
            for p in range(pairs_per_kv):
                pair_probs, maxes, sums, pair_sinks = [], [], [], []
                for hh in range(2):
                    sink = sinks_ref[(g * pairs_per_kv + p) * 2 + hh] * LOG2_E
                    s = s_all[p * WINDOW:(p + 1) * WINDOW, hh * 2 * WINDOW:(hh + 1) * 2 * WINDOW]
                    s = jnp.where(mask, s, MASKED)
                    mx = jnp.maximum(jnp.max(s, axis=-1, keepdims=True), sink)
                    e = jnp.exp2(s - mx)
                    pair_probs.append(e.astype(BF16))
                    maxes.append(mx)
                    sums.append(jnp.sum(e, axis=-1, keepdims=True))
                    pair_sinks.append(sink)
                probs.append(jnp.concatenate(pair_probs, axis=1))
                mx2 = jnp.where(first_head, maxes[0], maxes[1])
                sum2 = jnp.where(first_head, sums[0], sums[1])
                sink2 = jnp.where(first_head, pair_sinks[0], pair_sinks[1])
                inv.append(1.0 / (sum2 + jnp.exp2(sink2 - mx2)))
            o_all = _dot(jnp.concatenate(probs, axis=0), v_bd)
            for p, col in enumerate(cols):
                o2 = o_all[p * WINDOW:(p + 1) * WINDOW] * inv[p]
                og_ref[0, rows, pl.ds(col, pair_w)] = (o2 * z_scr[rows, pl.ds(col, pair_w)]).astype(og_ref.dtype)
        return carry

    lax.fori_loop(0, n_pairs // pairs_per_kv, group_body, 0, unroll=2)


def _l1_prompt(h, g1, kvg, w_kv, w_in, sinks, tables, tm=PROMPT_TILE):
    bsz, seq, dm = h.shape
    width = w_in.shape[1] // 2
    kv_cols = w_kv.shape[1] // 2
    kvw = 2 * kv_cols
    pair_w = 2 * HEAD_DIM
    const = lambda *shape: pl.BlockSpec(shape, lambda b, l: (0,) * len(shape))
    table = pl.BlockSpec((tm, pair_w), lambda b, l: (l, 0))
    return pl.pallas_call(
        _l1_prompt_kernel,
        grid=(bsz, seq // tm),
        in_specs=[
            pl.BlockSpec(memory_space=pltpu.SMEM),
            pl.BlockSpec((1, tm, dm), lambda b, l: (b, l, 0)),
            const(1, dm), const(1, dm), const(dm, 2 * kv_cols), const(dm, 2 * width),
            table, table, table,
        ],
        out_specs=[
            pl.BlockSpec((1, tm, width), lambda b, l: (b, l, 0)),
            pl.BlockSpec((1, WINDOW, kv_cols), lambda b, l: (b, 0, 0)),
            pl.BlockSpec((1, WINDOW, kv_cols), lambda b, l: (b, 0, 0)),
        ],
        out_shape=[
            jax.ShapeDtypeStruct((bsz, seq, width), BF16),
            jax.ShapeDtypeStruct((bsz, WINDOW, kv_cols), F32),
            jax.ShapeDtypeStruct((bsz, WINDOW, kv_cols), F32),
        ],
        scratch_shapes=[
            pltpu.VMEM((WINDOW + tm, kvw), BF16),
            pltpu.VMEM((WINDOW + tm, kvw), BF16),
            pltpu.VMEM((tm, width), BF16),
            pltpu.VMEM((tm, width), F32),
        ],
        compiler_params=pltpu.CompilerParams(
            dimension_semantics=("arbitrary", "arbitrary"), vmem_limit_bytes=VMEM_LIMIT_BYTES),
        name="l1_prompt",
    )(sinks, h, g1, kvg, w_kv, w_in, *tables)


def _norm_matmul_kernel(x_ref, g_ref, w_ref, o_ref):
    xn = (_rms(x_ref[...]) * g_ref[...]).astype(BF16)
    o_ref[...] = _dot(xn, w_ref[...])


def _norm_matmul(x, g, w, tn=2048):
    t, dm = x.shape
    n = w.shape[1]
    tn = min(tn, n)
    return pl.pallas_call(
        _norm_matmul_kernel,
        grid=(n // tn,),
        in_specs=[
            pl.BlockSpec((t, dm), lambda j: (0, 0)),
            pl.BlockSpec((1, dm), lambda j: (0, 0)),
            pl.BlockSpec((dm, tn), lambda j: (0, j)),
        ],
        out_specs=pl.BlockSpec((t, tn), lambda j: (0, j)),
        out_shape=jax.ShapeDtypeStruct((t, n), F32),
        compiler_params=pltpu.CompilerParams(
            dimension_semantics=("arbitrary",), vmem_limit_bytes=VMEM_LIMIT_BYTES),
        name="norm_matmul",
    )(x, g, w)


def _l0_sample_kernel(u_ref, lbl_ref, on_ref, s0_ref, s1_ref, og_ref):
    bt = u_ref.shape[0]
    width = og_ref.shape[1]
    d = HGRN_HEAD_DIM
    lb_all = _lower_bound(lbl_ref[...])

    for h in range(width // d):
        cols = slice(h * d, (h + 1) * d)
        lb = lb_all[:, cols]
        q = _silu(u_ref[:, h * d:(h + 1) * d])
        f = lb + (1.0 - lb) * _sigmoid(u_ref[:, width + h * d:width + (h + 1) * d])
        v = u_ref[:, 2 * width + h * d:2 * width + (h + 1) * d]
        z = _silu(u_ref[:, 3 * width + h * d:3 * width + (h + 1) * d])
        f_t = f.T
        q_b = q.astype(BF16)
        outs = []
        for j in range(bt):
            v_row = v[j:j + 1, :]
            s_new = v_row + f_t[:, j:j + 1] * (s0_ref[j, h] - v_row)
            s1_ref[j, h] = s_new
            outs.append(_dot(q_b[j:j + 1, :], s_new.astype(BF16)))
        o = jnp.concatenate(outs, axis=0)
        og_ref[:, cols] = _rms(o) * on_ref[:, cols] * z


def _l0_sample(u, lb_logits, onorm, state, bt=8):
    n, heads, d, _ = state.shape
    width = heads * d
    return pl.pallas_call(
        _l0_sample_kernel,
        grid=(n // bt,),
        in_specs=[
            pl.BlockSpec((bt, 4 * width), lambda i: (i, 0)),
            pl.BlockSpec(lb_logits.shape, lambda i: (0, 0)),
            pl.BlockSpec((1, width), lambda i: (0, 0)),
            pl.BlockSpec((bt, heads, d, d), lambda i: (i, 0, 0, 0)),
        ],
        out_specs=[
            pl.BlockSpec((bt, heads, d, d), lambda i: (i, 0, 0, 0)),
            pl.BlockSpec((bt, width), lambda i: (i, 0)),
        ],
        out_shape=[
            jax.ShapeDtypeStruct(state.shape, F32),
            jax.ShapeDtypeStruct((n, width), F32),
        ],
        compiler_params=pltpu.CompilerParams(
            dimension_semantics=("arbitrary",), vmem_limit_bytes=VMEM_LIMIT_BYTES),
        name="l0_sample",
    )(u, lb_logits, onorm, state)


def _l1_sample_front_kernel(h_ref, g1_ref, kvg_ref, win_ref, wkv_ref, cos_ref, slo_ref, shi_ref,
                            q_ref, z_ref, k_ref, v_ref):
    width = q_ref.shape[1]
    kvw = k_ref.shape[1]
    h0 = _rms(h_ref[...])
    xn = (h0 * g1_ref[...]).astype(BF16)
    hk = (h0 * kvg_ref[...]).astype(BF16)
    cos, slo, shi = cos_ref[...], slo_ref[...], shi_ref[...]
    q = _rope(_dot(xn, win_ref[:, 0:width]), cos, slo, shi)
    q_ref[...] = q * ATTN_SCALE
    z_ref[...] = _silu(_dot(xn, win_ref[:, width:2 * width]))
    kv = _dot(hk, wkv_ref[...])
    k_ref[...] = _rope(kv[:, :kvw], cos[:, :kvw], slo[:, :kvw], shi[:, :kvw])
    v_ref[...] = kv[:, kvw:]


def _l1_sample_front(h, g1, kvg, w_in, w_kv, tables):
    t, dm = h.shape
    width = w_in.shape[1] // 2
    kvw = w_kv.shape[1] // 2
    full = lambda a: pl.BlockSpec(a.shape, lambda i: (0,) * a.ndim)
    args = (h, g1, kvg, w_in, w_kv, *tables)
    return pl.pallas_call(
        _l1_sample_front_kernel,
        grid=(1,),
        in_specs=[full(a) for a in args],
        out_specs=[pl.BlockSpec((t, width), lambda i: (0, 0)), pl.BlockSpec((t, width), lambda i: (0, 0)),
                   pl.BlockSpec((t, kvw), lambda i: (0, 0)), pl.BlockSpec((t, kvw), lambda i: (0, 0))],
        out_shape=[jax.ShapeDtypeStruct((t, width), F32), jax.ShapeDtypeStruct((t, width), F32),
                   jax.ShapeDtypeStruct((t, kvw), F32), jax.ShapeDtypeStruct((t, kvw), F32)],
        compiler_params=pltpu.CompilerParams(
            dimension_semantics=("arbitrary",), vmem_limit_bytes=VMEM_LIMIT_BYTES),
        name="l1_sample_front",
    )(*args)


def _l1_sample_attn_kernel(q_ref, z_ref, kn_ref, vn_ref, ckt_ref, cvt_ref, sink_ref, og_ref):
    bt, kv_heads, hd, wb = ckt_ref.shape
    n_q = q_ref.shape[1]
    kvw = kv_heads * hd
    valid = lax.broadcasted_iota(jnp.int32, (n_q, wb), 1) > wb - WINDOW
    head_kv = lax.broadcasted_iota(jnp.int32, (n_q, kvw), 0) // Q_PER_KV
    own_lanes = lax.broadcasted_iota(jnp.int32, (n_q, kvw), 1) // hd == head_kv
    row_kv = lax.broadcasted_iota(jnp.int32, (n_q, hd), 0) // Q_PER_KV
    sink = sink_ref[...]
    for j in range(bt):
        q = q_ref[j]
        q_bd = jnp.where(own_lanes, jnp.concatenate([q] * kv_heads, axis=1), 0.0)
        s = jnp.where(valid, _dot(q_bd.astype(BF16), ckt_ref[j].reshape(kvw, wb).astype(BF16)), MASKED)
        s_new = jnp.sum(q_bd * kn_ref[j:j + 1, :], axis=-1, keepdims=True)
        mx = jnp.maximum(jnp.maximum(jnp.max(s, axis=-1, keepdims=True), s_new), sink)
        e = jnp.exp(s - mx)
        e_new = jnp.exp(s_new - mx)
        den = jnp.sum(e, axis=-1, keepdims=True) + e_new + jnp.exp(sink - mx)
        o_all = _dot_nt(e.astype(BF16), cvt_ref[j].reshape(kvw, wb).astype(BF16)) + e_new * vn_ref[j:j + 1, :]
        o = jnp.zeros((n_q, hd), F32)
        for g in range(kv_heads):
            o = o + jnp.where(row_kv == g, o_all[:, g * hd:(g + 1) * hd], 0.0)
        og_ref[j] = o / den * z_ref[j]


def _l1_sample_attn(q3, z3, k_new, v_new, cache_kt, cache_vt, sinks_col, bt=8):
    n, heads, hd = q3.shape
    kv_heads, wb = cache_kt.shape[1], cache_kt.shape[3]
    qspec = pl.BlockSpec((bt, heads, hd), lambda i: (i, 0, 0))
    nspec = pl.BlockSpec((bt, kv_heads * hd), lambda i: (i, 0))
    cspec = pl.BlockSpec((bt, kv_heads, hd, wb), lambda i: (i, 0, 0, 0))
    return pl.pallas_call(
        _l1_sample_attn_kernel,
        grid=(n // bt,),
        in_specs=[qspec, qspec, nspec, nspec, cspec, cspec, pl.BlockSpec((heads, 1), lambda i: (0, 0))],
        out_specs=qspec,
        out_shape=jax.ShapeDtypeStruct((n, heads, hd), F32),
        compiler_params=pltpu.CompilerParams(
            dimension_semantics=("arbitrary",), vmem_limit_bytes=VMEM_LIMIT_BYTES),
        name="l1_sample_attn",
    )(q3, z3, k_new, v_new, cache_kt, cache_vt, sinks_col)


def kernel(x_prompt, x_sample, p_prompt, p_sample, state_hgrn, cache_k, cache_v, pre_norm_g, post_norm_g,
           w_in_a, lb_logits, onorm_a, w_out_a, kv_norm_g, w_kv, w_in_b, sinks, w_out_b, w_pe, w_pg):
    bsz, seq, dm = x_prompt.shape
    n_s = x_sample.shape[0]
    assert x_sample.shape[1] == 1 and w_in_a.shape[0] == 1 and w_in_b.shape[0] == 1
    assert seq % PROMPT_TILE == 0 and n_s % 8 == 0
    kv_heads = w_kv.shape[1] // (2 * HEAD_DIM)
    n_q = w_in_b.shape[2] // (2 * HEAD_DIM)
    assert n_q == kv_heads * Q_PER_KV

    row = lambda a: a.reshape(1, -1)
    bf = lambda a: a.astype(BF16)
    w_in_a0, w_out_a0 = bf(w_in_a[0]), bf(w_out_a[0])
    w_in_b0, w_out_b0 = bf(w_in_b[0]), bf(w_out_b[0])
    w_pe_b, w_pg_b = bf(w_pe), bf(w_pg)
    w_kv_b = bf(w_kv)

    pp = p_prompt.reshape(p_prompt.shape[0], bsz * seq, -1)
    ps = p_sample.reshape(p_sample.shape[0], n_s, -1)

    og0, st_p = _l0_prompt(x_prompt, row(pre_norm_g[0]), w_in_a0, lb_logits, row(onorm_a[0]))
    h1 = _epilogue(og0.reshape(bsz * seq, -1), x_prompt.reshape(bsz * seq, dm), pp, 0,
                   w_out_a0, row(post_norm_g[0]), w_pe_b[0], w_pg_b[0], tm=EPILOGUE_TILE)
    tables_p = _rope_tables(jnp.arange(seq), 2)
    og1, k_p, v_p = _l1_prompt(h1.reshape(bsz, seq, dm), row(pre_norm_g[1]), row(kv_norm_g), w_kv_b, w_in_b0,
                               sinks[0], tables_p)
    y_p = _epilogue(og1.reshape(bsz * seq, -1), h1, pp, 1,
                    w_out_b0, row(post_norm_g[1]), w_pe_b[1], w_pg_b[1], tm=EPILOGUE_TILE)

    xs = x_sample.reshape(n_s, dm)
    u0 = _norm_matmul(xs, row(pre_norm_g[0]), w_in_a0)
    st_s, og0s = _l0_sample(u0, lb_logits, row(onorm_a[0]), state_hgrn[0])
    h1s = _epilogue(og0s, xs, ps, 0,
                    w_out_a0, row(post_norm_g[0]), w_pe_b[0], w_pg_b[0], tm=n_s)
    tables_s = _rope_tables(PAST_LEN + jnp.arange(1), n_q)
    q_s, z_s, k_s, v_s = _l1_sample_front(h1s, row(pre_norm_g[1]), row(kv_norm_g), w_in_b0, w_kv_b, tables_s)
    og1s = _l1_sample_attn(q_s.reshape(n_s, n_q, HEAD_DIM), z_s.reshape(n_s, n_q, HEAD_DIM), k_s, v_s,
                           cache_k.transpose(0, 2, 3, 1), cache_v.transpose(0, 2, 3, 1),
                           sinks[0].reshape(n_q, 1))
    y_s = _epilogue(og1s.reshape(n_s, -1), h1s, ps, 1,
                    w_out_b0, row(post_norm_g[1]), w_pe_b[1], w_pg_b[1], tm=n_s)

    return (y_p.reshape(bsz, seq, dm), y_s.reshape(n_s, 1, dm),
            st_p[None], st_s[None],
            k_p.reshape(bsz, WINDOW, kv_heads, HEAD_DIM), v_p.reshape(bsz, WINDOW, kv_heads, HEAD_DIM),
            k_s.reshape(n_s, 1, kv_heads, HEAD_DIM), v_s.reshape(n_s, 1, kv_heads, HEAD_DIM))
```

```python
import jax
import jax.numpy as jnp
from jax import lax
from jax.experimental import pallas as pl
from jax.experimental.pallas import tpu as pltpu

F32 = jnp.float32
BF16 = jnp.bfloat16

EPS = 1e-6
HGRN_HEAD_DIM = 128
HEAD_DIM = 64
Q_PER_KV = 8
WINDOW = 128
ROT_DIM = 16
ROPE_THETA = 500000.0
PAST_LEN = 16384
ATTN_SCALE = HEAD_DIM ** -0.5
LOG2_E = 1.4426950408889634

CHUNK = 128
LEVEL_HALVES = (64, 32, 16, 8, 4, 2, 1)
BF16_ROWS = 16
FAST_TAIL = 32
FAST_MAX_EXPONENT = 40.0
PROJ_COLS = 512
PROMPT_TILE = 512
EPILOGUE_TILE = 1024
SAMPLE_TILE = 8
MASKED = -1e30
V7X_VMEM_BYTES = 64 * 1024 * 1024
VMEM_LIMIT_BYTES = V7X_VMEM_BYTES - 8 * 1024 * 1024


def _sigmoid(x):
    return 1.0 / (1.0 + jnp.exp(-x))


def _silu(x):
    return x * _sigmoid(x)


def _rms(x):
    return x * lax.rsqrt(jnp.mean(x * x, axis=-1, keepdims=True) + EPS)


def _dot(a, b):
    return jnp.dot(a, b, preferred_element_type=F32)


def _dot_nt(a, b):
    return lax.dot_general(a, b, (((1,), (1,)), ((), ())), preferred_element_type=F32)


def _dot_tn(a, b):
    return lax.dot_general(a, b, (((0,), (0,)), ((), ())), preferred_element_type=F32)


def _block_diag2(a):
    c = a.shape[1] // 2
    lane = lax.broadcasted_iota(jnp.int32, a.shape, 1)
    zero = jnp.zeros_like(a)
    return jnp.concatenate([jnp.where(lane < c, a, zero), jnp.where(lane >= c, a, zero)], axis=0)


def _rope(x, cos, sin_lo, sin_hi):
    n = x.shape[-1]
    half = ROT_DIM // 2
    return x * cos + pltpu.roll(x, n - half, 1) * sin_lo + pltpu.roll(x, half, 1) * sin_hi


def _rope_tables(pos, reps):
    half = ROT_DIM // 2
    inv = ROPE_THETA ** (-jnp.arange(0, ROT_DIM, 2, dtype=F32) / ROT_DIM)
    ang = pos.astype(F32)[:, None] * inv[None, :]
    cos, sin = jnp.cos(ang), jnp.sin(ang)
    n = pos.shape[0]
    pad = jnp.zeros((n, HEAD_DIM - ROT_DIM), F32)
    zero = jnp.zeros((n, half), F32)
    c = jnp.concatenate([cos, cos, pad + 1.0], axis=1)
    s_lo = jnp.concatenate([-sin, zero, pad], axis=1)
    s_hi = jnp.concatenate([zero, sin, pad], axis=1)
    return tuple(jnp.tile(t, (1, reps)) for t in (c, s_lo, s_hi))


def _lower_bound(lbl):
    m = jnp.max(lbl, axis=0, keepdims=True)
    e = jnp.exp(lbl - m)
    return e[0:1] / jnp.sum(e, axis=0, keepdims=True)


def _level_ref(b, half, row):
    if half >= 8:
        return _block_rows(b, half - 1, 2 * half)
    n = b.shape[0]
    up = lambda x, d: pltpu.roll(x, n - d, 0)
    down = lambda x, d: pltpu.roll(x, d, 0)
    if half == 1:
        return jnp.where(row % 2 == 1, down(b, 1), b)
    y1 = jnp.where(row % 2 == 0, up(b, 1), b)
    if half == 2:
        return jnp.where(row % 4 >= 2, down(y1, 2), y1)
    y2 = jnp.where(row % 4 < 2, up(y1, 2), y1)
    return jnp.where(row % 8 >= 4, down(y2, 4), y2)


def _block_rows(b, offset, size):
    pieces = [jnp.broadcast_to(b[s + offset:s + offset + 1, :], (size, b.shape[1]))
              for s in range(0, b.shape[0], size)]
    return pieces[0] if len(pieces) == 1 else jnp.concatenate(pieces, axis=0)


def _block_diag2_aligned(a):
    c = a.shape[1] // 2
    zero = jnp.zeros((a.shape[0], c), a.dtype)
    return jnp.concatenate([jnp.concatenate([a[:, :c], zero], axis=1),
                            jnp.concatenate([zero, a[:, c:]], axis=1)], axis=0)


def _score_masks(c, w, halves, tail):
    d = w // 2
    row = lax.broadcasted_iota(jnp.int32, (c, w), 0)
    col_tok = lax.broadcasted_iota(jnp.int32, (c, w), 1) % d
    one = lambda cond: jnp.where(cond, 1.0, 0.0).astype(BF16)
    masks = {"diag": one(row == col_tok)}
    for half in halves:
        masks[half] = one(row // (2 * half) == col_tok // (2 * half))
    if tail is not None:
        masks["tail"] = one((row // tail == col_tok // tail) & (row > col_tok))
    return masks


def _chunk_scores(q, k, q_bf, k_bf, b, halves, tail, masks):
    c, w = q.shape
    d = w // 2
    qk = q * k
    diag = jnp.concatenate(
        [jnp.broadcast_to(jnp.sum(qk[:, :d], axis=-1, keepdims=True), (c, d)),
         jnp.broadcast_to(jnp.sum(qk[:, d:], axis=-1, keepdims=True), (c, d))], axis=1)
    a = diag.astype(BF16) * masks["diag"]
    for half in halves:
        if half >= BF16_ROWS:
            delta = b - _block_rows(b, half - 1, 2 * half)
            zero = jnp.zeros((half, w), BF16)
            qe, ke = [], []
            for s in range(0, c, 2 * half):
                lo, up = slice(s, s + half), slice(s + half, s + 2 * half)
                ke += [k_bf[lo] * jnp.exp2(-delta[lo]).astype(BF16), zero]
                qe += [zero, q_bf[up] * jnp.exp2(delta[up]).astype(BF16)]
            qe, ke = jnp.concatenate(qe, axis=0), jnp.concatenate(ke, axis=0)
        else:
            row = lax.broadcasted_iota(jnp.int32, (c, w), 0)
            upper = row % (2 * half) >= half
            ref = _level_ref(b, half, row)
            e = jnp.exp2(jnp.where(upper, b - ref, ref - b))
            qe = jnp.where(upper, q * e, 0.0).astype(BF16)
            ke = jnp.where(upper, 0.0, k * e).astype(BF16)
        p = _dot_nt(qe, _block_diag2_aligned(ke)).astype(BF16)
        a = a + (p if 2 * half == c else p * masks[half])
    if tail is not None:
        delta = b - _block_rows(b, tail // 2 - 1, tail)
        qe = q_bf * jnp.exp2(delta).astype(BF16)
        ke = k_bf * jnp.exp2(-delta).astype(BF16)
        a = a + _dot_nt(qe, _block_diag2_aligned(ke)).astype(BF16) * masks["tail"]
    return a


def _chunk_apply(q_bf, k_bf, b, v, a, st_a, st_b):
    c, w = b.shape
    d = w // 2
    st_bd = _block_diag2_aligned(jnp.concatenate([st_a, st_b], axis=1).astype(BF16))
    o = _dot_nt(q_bf * jnp.exp2(b).astype(BF16), st_bd) + _dot(a, _block_diag2_aligned(v))
    b_last = b[c - 1:c, :]
    k_hat = k_bf * jnp.exp2(b_last - b).astype(BF16)
    ds = _dot_tn(v, k_hat)
    decay = jnp.exp2(b_last)
    st_a = st_a * decay[:, :d] + ds[:d, :d]
    st_b = st_b * decay[:, d:] + ds[d:, d:]
    return o, st_a, st_b


def _l0_prompt_kernel(x_ref, pg_ref, w_ref, lbl_ref, on_ref, og_ref, st_ref,
                      s_scr, q_scr, k_scr, b_scr, v_scr, z_scr):
    l = pl.program_id(1)
    tm = x_ref.shape[1]
    width = q_scr.shape[1]
    d = HGRN_HEAD_DIM
    n_heads = width // d

    @pl.when(l == 0)
    def _():
        s_scr[...] = jnp.zeros_like(s_scr)

    xn = (_rms(x_ref[0]) * pg_ref[...]).astype(BF16)
    ti = lax.broadcasted_iota(jnp.int32, (CHUNK, CHUNK), 0)
    tj = lax.broadcasted_iota(jnp.int32, (CHUNK, CHUNK), 1)
    tri = jnp.where(tj <= ti, 1.0, 0.0).astype(BF16)
    tri2 = jnp.concatenate([tri, tri], axis=1)
    worst = jnp.zeros((1, PROJ_COLS), F32)
    for j in range(width // PROJ_COLS):
        cols = slice(j * PROJ_COLS, (j + 1) * PROJ_COLS)
        part = lambda i: w_ref[:, i * width + j * PROJ_COLS:i * width + (j + 1) * PROJ_COLS]
        lb = _lower_bound(lbl_ref[:, cols])
        q_scr[:, cols] = _silu(_dot(xn, part(0)))
        f = lb + (1.0 - lb) * _sigmoid(_dot(xn, part(1)))
        k_scr[:, cols] = 1.0 - f
        v_scr[:, cols] = _dot(xn, part(2)).astype(BF16)
        z_scr[:, cols] = _silu(_dot(xn, part(3))) * on_ref[:, cols]
        g = jnp.log2(f)
        g_hi = g.astype(BF16)
        r1 = g - g_hi.astype(F32)
        g_mid = r1.astype(BF16)
        g_lo = (r1 - g_mid.astype(F32)).astype(BF16)
        for c in range(tm // CHUNK):
            rows = slice(c * CHUNK, (c + 1) * CHUNK)
            b = _dot(tri2, jnp.concatenate([g_hi[rows], g_mid[rows]], axis=0)) + _dot(tri, g_lo[rows])
            b_scr[rows, cols] = b
            for s in range(0, CHUNK, FAST_TAIL):
                mid = b[s + FAST_TAIL // 2 - 1:s + FAST_TAIL // 2, :]
                worst = jnp.maximum(
                    worst, jnp.maximum(b[s:s + 1, :] - mid, mid - b[s + FAST_TAIL - 1:s + FAST_TAIL, :]))
    fast = jnp.max(worst) <= FAST_MAX_EXPONENT * LOG2_E

    def scan(halves, tail, group):
        masks = _score_masks(CHUNK, 2 * d, halves, tail)
        chunks = [slice(c * CHUNK, (c + 1) * CHUNK) for c in range(tm // CHUNK)]

        def group_body(gi, carry):
            pairs = [gi * group + p for p in range(group)]
            cols = [pl.multiple_of(pi * 2 * d, 2 * d) for pi in pairs]
            load = lambda ref, rows, col: ref[rows, pl.ds(col, 2 * d)]
            states = [(s_scr[2 * pi], s_scr[2 * pi + 1]) for pi in pairs]
            operands = {}
            for n, rows in enumerate(chunks):
                for p, col in enumerate(cols):
                    q, k, b = load(q_scr, rows, col), load(k_scr, rows, col), load(b_scr, rows, col)
                    q_bf, k_bf = q.astype(BF16), k.astype(BF16)
                    operands[p, n] = (q_bf, k_bf, b, _chunk_scores(q, k, q_bf, k_bf, b, halves, tail, masks))
            for n, rows in enumerate(chunks):
                for p, col in enumerate(cols):
                    q_bf, k_bf, b, a = operands[p, n]
                    o, st_a, st_b = _chunk_apply(q_bf, k_bf, b, load(v_scr, rows, col), a, *states[p])
                    states[p] = (st_a, st_b)
                    on = jnp.concatenate([_rms(o[:, :d]), _rms(o[:, d:])], axis=1)
                    og_ref[0, rows, pl.ds(col, 2 * d)] = (on * load(z_scr, rows, col)).astype(og_ref.dtype)
            for pi, (st_a, st_b) in zip(pairs, states):
                s_scr[2 * pi] = st_a
                s_scr[2 * pi + 1] = st_b
            return carry

        lax.fori_loop(0, n_heads // 2 // group, group_body, 0)

    @pl.when(fast)
    def _():
        scan(tuple(h for h in LEVEL_HALVES if h >= FAST_TAIL), FAST_TAIL, group=4)

    @pl.when(jnp.logical_not(fast))
    def _():
        scan(LEVEL_HALVES, None, group=1)

    @pl.when(l == pl.num_programs(1) - 1)
    def _():
        for h in range(n_heads):
            st_ref[0, h] = s_scr[h].T


def _l0_prompt(x, pre_g, w_in, lb_logits, onorm, tm=PROMPT_TILE):
    bsz, seq, dm = x.shape
    width = w_in.shape[1] // 4
    n_heads = width // HGRN_HEAD_DIM
    d = HGRN_HEAD_DIM
    const = lambda *shape: pl.BlockSpec(shape, lambda b, l: (0,) * len(shape))
    return pl.pallas_call(
        _l0_prompt_kernel,
        grid=(bsz, seq // tm),
        in_specs=[
            pl.BlockSpec((1, tm, dm), lambda b, l: (b, l, 0)),
            const(1, dm),
            pl.BlockSpec((dm, 4 * width), lambda b, l: (0, 0), pipeline_mode=pl.Buffered(1)),
            const(*lb_logits.shape),
            const(1, width),
        ],
        out_specs=[
            pl.BlockSpec((1, tm, width), lambda b, l: (b, l, 0)),
            pl.BlockSpec((1, n_heads, d, d), lambda b, l: (b, 0, 0, 0)),
        ],
        out_shape=[
            jax.ShapeDtypeStruct((bsz, seq, width), BF16),
            jax.ShapeDtypeStruct((bsz, n_heads, d, d), F32),
        ],
        scratch_shapes=[
            pltpu.VMEM((n_heads, d, d), F32),
            pltpu.VMEM((tm, width), F32),
            pltpu.VMEM((tm, width), F32),
            pltpu.VMEM((tm, width), F32),
            pltpu.VMEM((tm, width), BF16),
            pltpu.VMEM((tm, width), F32),
        ],
        compiler_params=pltpu.CompilerParams(
            dimension_semantics=("arbitrary", "arbitrary"), vmem_limit_bytes=VMEM_LIMIT_BYTES),
        name="l0_prompt",
    )(x, pre_g, w_in, lb_logits, onorm)


def _epilogue_kernel(og_ref, h_ref, p_ref, wo_ref, pg_ref, wpe_ref, wpg_ref, out_ref):
    mix = _dot(og_ref[...].astype(BF16), wo_ref[...])
    hn = h_ref[...] + _rms(mix) * pg_ref[...]
    pe = _dot(p_ref[...].astype(BF16), wpe_ref[...])
    gate = _sigmoid(_dot(hn.astype(BF16), wpg_ref[...]))
    out_ref[...] = hn + pe * gate


def _epilogue(og, h, p_all, layer, w_out, post_g, w_pe, w_pg, tm):
    t, dm = h.shape
    width = og.shape[1]
    ple = p_all.shape[2]
    tm = min(tm, t)
    const = lambda *shape: pl.BlockSpec(shape, lambda i: (0,) * len(shape))
    return pl.pallas_call(
        _epilogue_kernel,
        grid=(t // tm,),
        in_specs=[
            pl.BlockSpec((tm, width), lambda i: (i, 0)),
            pl.BlockSpec((tm, dm), lambda i: (i, 0)),
            pl.BlockSpec((None, tm, ple), lambda i: (layer, i, 0)),
            const(width, dm), const(1, dm), const(ple, dm), const(dm, dm),
        ],
        out_specs=pl.BlockSpec((tm, dm), lambda i: (i, 0)),
        out_shape=jax.ShapeDtypeStruct((t, dm), F32),
        compiler_params=pltpu.CompilerParams(
            dimension_semantics=("arbitrary",), vmem_limit_bytes=VMEM_LIMIT_BYTES),
        name="epilogue",
    )(og, h, p_all, w_out, post_g, w_pe, w_pg)


def _dup_heads(x):
    pair_w = 2 * HEAD_DIM
    lane = lax.broadcasted_iota(jnp.int32, (x.shape[0], pair_w), 1)
    out = []
    for c in range(x.shape[1] // pair_w):
        both = x[:, c * pair_w:(c + 1) * pair_w]
        swapped = pltpu.roll(both, HEAD_DIM, 1)
        out += [jnp.where(lane < HEAD_DIM, both, swapped), jnp.where(lane < HEAD_DIM, swapped, both)]
    return jnp.concatenate(out, axis=1)


def _l1_prompt_kernel(sinks_ref, h_ref, g1_ref, kvg_ref, wkv_ref, win_ref, cos_ref, slo_ref, shi_ref,
                      og_ref, kc_ref, vc_ref, kx_scr, vx_scr, q_scr, z_scr):
    l = pl.program_id(1)
    tm = h_ref.shape[1]
    width = z_scr.shape[1]
    kvw = kx_scr.shape[1]
    kv_cols = kc_ref.shape[2]
    pair_w = 2 * HEAD_DIM
    n_pairs = width // pair_w
    pairs_per_kv = Q_PER_KV // 2

    @pl.when(l == 0)
    def _():
        kx_scr[0:WINDOW, :] = jnp.zeros((WINDOW, kvw), kx_scr.dtype)
        vx_scr[0:WINDOW, :] = jnp.zeros((WINDOW, kvw), vx_scr.dtype)

    @pl.when(l > 0)
    def _():
        kx_scr[0:WINDOW, :] = kx_scr[tm:tm + WINDOW, :]
        vx_scr[0:WINDOW, :] = vx_scr[tm:tm + WINDOW, :]

    h0 = _rms(h_ref[0])
    xn = (h0 * g1_ref[...]).astype(BF16)
    hk = (h0 * kvg_ref[...]).astype(BF16)
    cos, slo, shi = cos_ref[...], slo_ref[...], shi_ref[...]
    tile = lambda t, n: jnp.concatenate([t] * n, axis=1)

    q = _rope(_dot(xn, win_ref[:, 0:width]), tile(cos, n_pairs), tile(slo, n_pairs), tile(shi, n_pairs))
    q_scr[...] = (q * (ATTN_SCALE * LOG2_E)).astype(q_scr.dtype)
    z_scr[...] = _silu(_dot(xn, win_ref[:, width:2 * width]))

    kv = _dot(hk, wkv_ref[...])
    reps = kv_cols // pair_w
    k_rot = _rope(kv[:, :kv_cols], tile(cos, reps), tile(slo, reps), tile(shi, reps))
    v_new = kv[:, kv_cols:]
    kx_scr[WINDOW:WINDOW + tm, :] = _dup_heads(k_rot).astype(kx_scr.dtype)
    vx_scr[WINDOW:WINDOW + tm, :] = _dup_heads(v_new).astype(vx_scr.dtype)

    @pl.when(l == pl.num_programs(1) - 1)
    def _():
        kc_ref[0] = k_rot[tm - WINDOW:, :]
        vc_ref[0] = v_new[tm - WINDOW:, :]

    qi = lax.broadcasted_iota(jnp.int32, (WINDOW, 2 * WINDOW), 0)
    ki = lax.broadcasted_iota(jnp.int32, (WINDOW, 2 * WINDOW), 1)
    band = (ki > qi) & (ki <= qi + WINDOW)
    first_head = lax.broadcasted_iota(jnp.int32, (WINDOW, pair_w), 1) < HEAD_DIM

    def group_body(g, carry):
        kcol = pl.multiple_of(g * pair_w, pair_w)
        cols = [pl.multiple_of((g * pairs_per_kv + p) * pair_w, pair_w) for p in range(pairs_per_kv)]
        for c in range(tm // WINDOW):
            rows = slice(c * WINDOW, (c + 1) * WINDOW)
            keys = slice(c * WINDOW, (c + 2) * WINDOW)
            k_bd = _block_diag2(kx_scr[keys, pl.ds(kcol, pair_w)])
            v_bd = _block_diag2(vx_scr[keys, pl.ds(kcol, pair_w)])
            q_all = jnp.concatenate([q_scr[rows, pl.ds(col, pair_w)] for col in cols], axis=0)
            s_all = _dot_nt(q_all, k_bd)
            mask = band & (ki >= jnp.where(l == 0, WINDOW, 0)) if c == 0 else band
            probs, inv = [], []
            for p in range(pairs_per_kv):
                pair_probs, maxes, sums, pair_sinks = [], [], [], []
                for hh in range(2):
                    sink = sinks_ref[(g * pairs_per_kv + p) * 2 + hh] * LOG2_E
                    s = s_all[p * WINDOW:(p + 1) * WINDOW, hh * 2 * WINDOW:(hh + 1) * 2 * WINDOW]
                    s = jnp.where(mask, s, MASKED)
                    mx = jnp.maximum(jnp.max(s, axis=-1, keepdims=True), sink)
                    e = jnp.exp2(s - mx)
                    pair_probs.append(e.astype(BF16))
                    maxes.append(mx)
                    sums.append(jnp.sum(e, axis=-1, keepdims=True))
                    pair_sinks.append(sink)
                probs.append(jnp.concatenate(pair_probs, axis=1))
                mx2 = jnp.where(first_head, maxes[0], maxes[1])
                sum2 = jnp.where(first_head, sums[0], sums[1])
                sink2 = jnp.where(first_head, pair_sinks[0], pair_sinks[1])
                inv.append(1.0 / (sum2 + jnp.exp2(sink2 - mx2)))
            o_all = _dot(jnp.concatenate(probs, axis=0), v_bd)
            for p, col in enumerate(cols):
                o2 = o_all[p * WINDOW:(p + 1) * WINDOW] * inv[p]
                og_ref[0, rows, pl.ds(col, pair_w)] = (o2 * z_scr[rows, pl.ds(col, pair_w)]).astype(og_ref.dtype)
        return carry

    lax.fori_loop(0, n_pairs // pairs_per_kv, group_body, 0, unroll=2)


def _l1_prompt(h, g1, kvg, w_kv, w_in, sinks, tables, tm=PROMPT_TILE):
    bsz, seq, dm = h.shape
    width = w_in.shape[1] // 2
    kv_cols = w_kv.shape[1] // 2
    kvw = 2 * kv_cols
    pair_w = 2 * HEAD_DIM
    const = lambda *shape: pl.BlockSpec(shape, lambda b, l: (0,) * len(shape))
    table = pl.BlockSpec((tm, pair_w), lambda b, l: (l, 0))
    return pl.pallas_call(
        _l1_prompt_kernel,
        grid=(bsz, seq // tm),
        in_specs=[
            pl.BlockSpec(memory_space=pltpu.SMEM),
            pl.BlockSpec((1, tm, dm), lambda b, l: (b, l, 0)),
            const(1, dm), const(1, dm), const(dm, 2 * kv_cols), const(dm, 2 * width),
            table, table, table,
        ],
        out_specs=[
            pl.BlockSpec((1, tm, width), lambda b, l: (b, l, 0)),
            pl.BlockSpec((1, WINDOW, kv_cols), lambda b, l: (b, 0, 0)),
            pl.BlockSpec((1, WINDOW, kv_cols), lambda b, l: (b, 0, 0)),
        ],
        out_shape=[
            jax.ShapeDtypeStruct((bsz, seq, width), BF16),
            jax.ShapeDtypeStruct((bsz, WINDOW, kv_cols), F32),
            jax.ShapeDtypeStruct((bsz, WINDOW, kv_cols), F32),
        ],
        scratch_shapes=[
            pltpu.VMEM((WINDOW + tm, kvw), BF16),
            pltpu.VMEM((WINDOW + tm, kvw), BF16),
            pltpu.VMEM((tm, width), BF16),
            pltpu.VMEM((tm, width), F32),
        ],
        compiler_params=pltpu.CompilerParams(
            dimension_semantics=("arbitrary", "arbitrary"), vmem_limit_bytes=VMEM_LIMIT_BYTES),
        name="l1_prompt",
    )(sinks, h, g1, kvg, w_kv, w_in, *tables)


def _norm_matmul_kernel(x_ref, g_ref, w_ref, o_ref):
    xn = (_rms(x_ref[...]) * g_ref[...]).astype(BF16)
    o_ref[...] = _dot(xn, w_ref[...])


def _norm_matmul(x, g, w, tn=2048):
    t, dm = x.shape
    n = w.shape[1]
    tn = min(tn, n)
    return pl.pallas_call(
        _norm_matmul_kernel,
        grid=(n // tn,),
        in_specs=[
            pl.BlockSpec((t, dm), lambda j: (0, 0)),
            pl.BlockSpec((1, dm), lambda j: (0, 0)),
            pl.BlockSpec((dm, tn), lambda j: (0, j)),
        ],
        out_specs=pl.BlockSpec((t, tn), lambda j: (0, j)),
        out_shape=jax.ShapeDtypeStruct((t, n), F32),
        compiler_params=pltpu.CompilerParams(
            dimension_semantics=("arbitrary",), vmem_limit_bytes=VMEM_LIMIT_BYTES),
        name="norm_matmul",
    )(x, g, w)


def _l0_sample_kernel(u_ref, lbl_ref, on_ref, s0_ref, s1_ref, og_ref):
    bt = u_ref.shape[0]
    width = og_ref.shape[1]
    d = HGRN_HEAD_DIM
    lb_all = _lower_bound(lbl_ref[...])

    for h in range(width // d):
        cols = slice(h * d, (h + 1) * d)
        lb = lb_all[:, cols]
        q = _silu(u_ref[:, h * d:(h + 1) * d])
        f = lb + (1.0 - lb) * _sigmoid(u_ref[:, width + h * d:width + (h + 1) * d])
        v = u_ref[:, 2 * width + h * d:2 * width + (h + 1) * d]
        z = _silu(u_ref[:, 3 * width + h * d:3 * width + (h + 1) * d])
        f_t = f.T
        q_b = q.astype(BF16)
        outs = []
        for j in range(bt):
            v_row = v[j:j + 1, :]
            s_new = v_row + f_t[:, j:j + 1] * (s0_ref[j, h] - v_row)
            s1_ref[j, h] = s_new
            outs.append(_dot(q_b[j:j + 1, :], s_new.astype(BF16)))
        o = jnp.concatenate(outs, axis=0)
        og_ref[:, cols] = _rms(o) * on_ref[:, cols] * z


def _l0_sample(u, lb_logits, onorm, state, bt=SAMPLE_TILE):
    n, heads, d, _ = state.shape
    width = heads * d
    return pl.pallas_call(
        _l0_sample_kernel,
        grid=(n // bt,),
        in_specs=[
            pl.BlockSpec((bt, 4 * width), lambda i: (i, 0)),
            pl.BlockSpec(lb_logits.shape, lambda i: (0, 0)),
            pl.BlockSpec((1, width), lambda i: (0, 0)),
            pl.BlockSpec((bt, heads, d, d), lambda i: (i, 0, 0, 0)),
        ],
        out_specs=[
            pl.BlockSpec((bt, heads, d, d), lambda i: (i, 0, 0, 0)),
            pl.BlockSpec((bt, width), lambda i: (i, 0)),
        ],
        out_shape=[
            jax.ShapeDtypeStruct(state.shape, F32),
            jax.ShapeDtypeStruct((n, width), F32),
        ],
        compiler_params=pltpu.CompilerParams(
            dimension_semantics=("arbitrary",), vmem_limit_bytes=VMEM_LIMIT_BYTES),
        name="l0_sample",
    )(u, lb_logits, onorm, state)


def _l1_sample_front_kernel(h_ref, g1_ref, kvg_ref, win_ref, wkv_ref, cos_ref, slo_ref, shi_ref,
                            q_ref, z_ref, k_ref, v_ref):
    width = q_ref.shape[1]
    kvw = k_ref.shape[1]
    h0 = _rms(h_ref[...])
    xn = (h0 * g1_ref[...]).astype(BF16)
    hk = (h0 * kvg_ref[...]).astype(BF16)
    cos, slo, shi = cos_ref[...], slo_ref[...], shi_ref[...]
    q = _rope(_dot(xn, win_ref[:, 0:width]), cos, slo, shi)
    q_ref[...] = q * ATTN_SCALE
    z_ref[...] = _silu(_dot(xn, win_ref[:, width:2 * width]))
    kv = _dot(hk, wkv_ref[...])
    k_ref[...] = _rope(kv[:, :kvw], cos[:, :kvw], slo[:, :kvw], shi[:, :kvw])
    v_ref[...] = kv[:, kvw:]


def _l1_sample_front(h, g1, kvg, w_in, w_kv, tables):
    t, dm = h.shape
    width = w_in.shape[1] // 2
    kvw = w_kv.shape[1] // 2
    full = lambda a: pl.BlockSpec(a.shape, lambda i: (0,) * a.ndim)
    args = (h, g1, kvg, w_in, w_kv, *tables)
    return pl.pallas_call(
        _l1_sample_front_kernel,
        grid=(1,),
        in_specs=[full(a) for a in args],
        out_specs=[pl.BlockSpec((t, width), lambda i: (0, 0)), pl.BlockSpec((t, width), lambda i: (0, 0)),
                   pl.BlockSpec((t, kvw), lambda i: (0, 0)), pl.BlockSpec((t, kvw), lambda i: (0, 0))],
        out_shape=[jax.ShapeDtypeStruct((t, width), F32), jax.ShapeDtypeStruct((t, width), F32),
                   jax.ShapeDtypeStruct((t, kvw), F32), jax.ShapeDtypeStruct((t, kvw), F32)],
        compiler_params=pltpu.CompilerParams(
            dimension_semantics=("arbitrary",), vmem_limit_bytes=VMEM_LIMIT_BYTES),
        name="l1_sample_front",
    )(*args)


def _l1_sample_attn_kernel(q_ref, z_ref, kn_ref, vn_ref, ckt_ref, cvt_ref, sink_ref, og_ref):
    bt, kv_heads, hd, wb = ckt_ref.shape
    n_q = q_ref.shape[1]
    kvw = kv_heads * hd
    valid = lax.broadcasted_iota(jnp.int32, (n_q, wb), 1) > wb - WINDOW
    head_kv = lax.broadcasted_iota(jnp.int32, (n_q, kvw), 0) // Q_PER_KV
    own_lanes = lax.broadcasted_iota(jnp.int32, (n_q, kvw), 1) // hd == head_kv
    row_kv = lax.broadcasted_iota(jnp.int32, (n_q, hd), 0) // Q_PER_KV
    sink = sink_ref[...]
    scored = []
    for j in range(bt):
        q = q_ref[j]
        q_bd = jnp.where(own_lanes, jnp.concatenate([q] * kv_heads, axis=1), 0.0)
        scored.append((q_bd, _dot(q_bd.astype(BF16), ckt_ref[j].reshape(kvw, wb).astype(BF16))))
    weighted = []
    for j, (q_bd, s) in enumerate(scored):
        s = jnp.where(valid, s, MASKED)
        s_new = jnp.sum(q_bd * kn_ref[j:j + 1, :], axis=-1, keepdims=True)
        mx = jnp.maximum(jnp.maximum(jnp.max(s, axis=-1, keepdims=True), s_new), sink)
        e = jnp.exp(s - mx)
        e_new = jnp.exp(s_new - mx)
        weighted.append((e.astype(BF16), e_new, jnp.sum(e, axis=-1, keepdims=True) + e_new + jnp.exp(sink - mx)))
    for j, (e, e_new, den) in enumerate(weighted):
        o_all = _dot_nt(e, cvt_ref[j].reshape(kvw, wb).astype(BF16)) + e_new * vn_ref[j:j + 1, :]
        o = jnp.zeros((n_q, hd), F32)
        for g in range(kv_heads):
            o = o + jnp.where(row_kv == g, o_all[:, g * hd:(g + 1) * hd], 0.0)
        og_ref[j] = o / den * z_ref[j]


def _l1_sample_attn(q3, z3, k_new, v_new, cache_kt, cache_vt, sinks_col, bt=SAMPLE_TILE):
    n, heads, hd = q3.shape
    kv_heads, wb = cache_kt.shape[1], cache_kt.shape[3]
    qspec = pl.BlockSpec((bt, heads, hd), lambda i: (i, 0, 0))
    nspec = pl.BlockSpec((bt, kv_heads * hd), lambda i: (i, 0))
    cspec = pl.BlockSpec((bt, kv_heads, hd, wb), lambda i: (i, 0, 0, 0))
    return pl.pallas_call(
        _l1_sample_attn_kernel,
        grid=(n // bt,),
        in_specs=[qspec, qspec, nspec, nspec, cspec, cspec, pl.BlockSpec((heads, 1), lambda i: (0, 0))],
        out_specs=qspec,
        out_shape=jax.ShapeDtypeStruct((n, heads, hd), F32),
        compiler_params=pltpu.CompilerParams(
            dimension_semantics=("arbitrary",), vmem_limit_bytes=VMEM_LIMIT_BYTES),
        name="l1_sample_attn",
    )(q3, z3, k_new, v_new, cache_kt, cache_vt, sinks_col)


def kernel(x_prompt, x_sample, p_prompt, p_sample, state_hgrn, cache_k, cache_v, pre_norm_g, post_norm_g,
           w_in_a, lb_logits, onorm_a, w_out_a, kv_norm_g, w_kv, w_in_b, sinks, w_out_b, w_pe, w_pg):
    bsz, seq, dm = x_prompt.shape
    n_s = x_sample.shape[0]
    assert x_sample.shape[1] == 1 and w_in_a.shape[0] == 1 and w_in_b.shape[0] == 1
    assert seq % PROMPT_TILE == 0 and n_s % SAMPLE_TILE == 0
    kv_heads = w_kv.shape[1] // (2 * HEAD_DIM)
    n_q = w_in_b.shape[2] // (2 * HEAD_DIM)
    assert n_q == kv_heads * Q_PER_KV

    row = lambda a: a.reshape(1, -1)
    bf = lambda a: a.astype(BF16)
    w_in_a0, w_out_a0 = bf(w_in_a[0]), bf(w_out_a[0])
    w_in_b0, w_out_b0 = bf(w_in_b[0]), bf(w_out_b[0])
    w_pe_b, w_pg_b = bf(w_pe), bf(w_pg)
    w_kv_b = bf(w_kv)

    pp = p_prompt.reshape(p_prompt.shape[0], bsz * seq, -1)
    ps = p_sample.reshape(p_sample.shape[0], n_s, -1)

    og0, st_p = _l0_prompt(x_prompt, row(pre_norm_g[0]), w_in_a0, lb_logits, row(onorm_a[0]))
    h1 = _epilogue(og0.reshape(bsz * seq, -1), x_prompt.reshape(bsz * seq, dm), pp, 0,
                   w_out_a0, row(post_norm_g[0]), w_pe_b[0], w_pg_b[0], tm=EPILOGUE_TILE)
    tables_p = _rope_tables(jnp.arange(seq), 2)
    og1, k_p, v_p = _l1_prompt(h1.reshape(bsz, seq, dm), row(pre_norm_g[1]), row(kv_norm_g), w_kv_b, w_in_b0,
                               sinks[0], tables_p)
    y_p = _epilogue(og1.reshape(bsz * seq, -1), h1, pp, 1,
                    w_out_b0, row(post_norm_g[1]), w_pe_b[1], w_pg_b[1], tm=EPILOGUE_TILE)

    xs = x_sample.reshape(n_s, dm)
    u0 = _norm_matmul(xs, row(pre_norm_g[0]), w_in_a0)
    st_s, og0s = _l0_sample(u0, lb_logits, row(onorm_a[0]), state_hgrn[0])
    h1s = _epilogue(og0s, xs, ps, 0,
                    w_out_a0, row(post_norm_g[0]), w_pe_b[0], w_pg_b[0], tm=n_s)
    tables_s = _rope_tables(PAST_LEN + jnp.arange(1), n_q)
    q_s, z_s, k_s, v_s = _l1_sample_front(h1s, row(pre_norm_g[1]), row(kv_norm_g), w_in_b0, w_kv_b, tables_s)
    og1s = _l1_sample_attn(q_s.reshape(n_s, n_q, HEAD_DIM), z_s.reshape(n_s, n_q, HEAD_DIM), k_s, v_s,
                           cache_k.transpose(0, 2, 3, 1), cache_v.transpose(0, 2, 3, 1),
                           sinks[0].reshape(n_q, 1))
    y_s = _epilogue(og1s.reshape(n_s, -1), h1s, ps, 1,
                    w_out_b0, row(post_norm_g[1]), w_pe_b[1], w_pg_b[1], tm=n_s)

    return (y_p.reshape(bsz, seq, dm), y_s.reshape(n_s, 1, dm),
            st_p[None], st_s[None],
            k_p.reshape(bsz, WINDOW, kv_heads, HEAD_DIM), v_p.reshape(bsz, WINDOW, kv_heads, HEAD_DIM),
            k_s.reshape(n_s, 1, kv_heads, HEAD_DIM), v_s.reshape(n_s, 1, kv_heads, HEAD_DIM))
```

```python
import jax
import jax.numpy as jnp
from jax import lax
from jax.experimental import pallas as pl
from jax.experimental.pallas import tpu as pltpu

F32 = jnp.float32
BF16 = jnp.bfloat16

EPS = 1e-6
HGRN_HEAD_DIM = 128
HEAD_DIM = 64
Q_PER_KV = 8
WINDOW = 128
ROT_DIM = 16
ROPE_THETA = 500000.0
PAST_LEN = 16384
ATTN_SCALE = HEAD_DIM ** -0.5
LOG2_E = 1.4426950408889634

CHUNK = 128
LEVEL_HALVES = (64, 32, 16, 8, 4, 2, 1)
BF16_ROWS = 16
FAST_TAIL = 32
FAST_MAX_EXPONENT = 40.0
PROJ_COLS = 512
L0_TILE = 512
L1_TILE = 1024
EPILOGUE_TILE = 1024
SAMPLE_TILE = 8
MASKED = -1e30
V7X_VMEM_BYTES = 64 * 1024 * 1024
VMEM_LIMIT_BYTES = V7X_VMEM_BYTES - 4 * 1024 * 1024


def _sigmoid(x):
    return 1.0 / (1.0 + jnp.exp(-x))


def _silu(x):
    return x * _sigmoid(x)


def _rms(x):
    return x * lax.rsqrt(jnp.mean(x * x, axis=-1, keepdims=True) + EPS)


def _dot(a, b):
    return jnp.dot(a, b, preferred_element_type=F32)


def _dot_nt(a, b):
    return lax.dot_general(a, b, (((1,), (1,)), ((), ())), preferred_element_type=F32)


def _dot_tn(a, b):
    return lax.dot_general(a, b, (((0,), (0,)), ((), ())), preferred_element_type=F32)


def _block_diag2(a):
    c = a.shape[1] // 2
    lane = lax.broadcasted_iota(jnp.int32, a.shape, 1)
    zero = jnp.zeros_like(a)
    return jnp.concatenate([jnp.where(lane < c, a, zero), jnp.where(lane >= c, a, zero)], axis=0)


def _rope(x, cos, sin_lo, sin_hi):
    n = x.shape[-1]
    half = ROT_DIM // 2
    return x * cos + pltpu.roll(x, n - half, 1) * sin_lo + pltpu.roll(x, half, 1) * sin_hi


def _rope_tables(pos, reps):
    half = ROT_DIM // 2
    inv = ROPE_THETA ** (-jnp.arange(0, ROT_DIM, 2, dtype=F32) / ROT_DIM)
    ang = pos.astype(F32)[:, None] * inv[None, :]
    cos, sin = jnp.cos(ang), jnp.sin(ang)
    n = pos.shape[0]
    pad = jnp.zeros((n, HEAD_DIM - ROT_DIM), F32)
    zero = jnp.zeros((n, half), F32)
    c = jnp.concatenate([cos, cos, pad + 1.0], axis=1)
    s_lo = jnp.concatenate([-sin, zero, pad], axis=1)
    s_hi = jnp.concatenate([zero, sin, pad], axis=1)
    return tuple(jnp.tile(t, (1, reps)) for t in (c, s_lo, s_hi))


def _lower_bound(lbl):
    m = jnp.max(lbl, axis=0, keepdims=True)
    e = jnp.exp(lbl - m)
    return e[0:1] / jnp.sum(e, axis=0, keepdims=True)


def _level_ref(b, half, row):
    if half >= 8:
        return _block_rows(b, half - 1, 2 * half)
    n = b.shape[0]
    up = lambda x, d: pltpu.roll(x, n - d, 0)
    down = lambda x, d: pltpu.roll(x, d, 0)
    if half == 1:
        return jnp.where(row % 2 == 1, down(b, 1), b)
    y1 = jnp.where(row % 2 == 0, up(b, 1), b)
    if half == 2:
        return jnp.where(row % 4 >= 2, down(y1, 2), y1)
    y2 = jnp.where(row % 4 < 2, up(y1, 2), y1)
    return jnp.where(row % 8 >= 4, down(y2, 4), y2)


def _block_rows(b, offset, size):
    pieces = [jnp.broadcast_to(b[s + offset:s + offset + 1, :], (size, b.shape[1]))
              for s in range(0, b.shape[0], size)]
    return pieces[0] if len(pieces) == 1 else jnp.concatenate(pieces, axis=0)


def _block_diag2_aligned(a):
    c = a.shape[1] // 2
    zero = jnp.zeros((a.shape[0], c), a.dtype)
    return jnp.concatenate([jnp.concatenate([a[:, :c], zero], axis=1),
                            jnp.concatenate([zero, a[:, c:]], axis=1)], axis=0)


def _score_masks(c, w, halves, tail):
    d = w // 2
    row = lax.broadcasted_iota(jnp.int32, (c, w), 0)
    col_tok = lax.broadcasted_iota(jnp.int32, (c, w), 1) % d
    one = lambda cond: jnp.where(cond, 1.0, 0.0).astype(BF16)
    masks = {"diag": one(row == col_tok)}
    for half in halves:
        masks[half] = one(row // (2 * half) == col_tok // (2 * half))
    if tail is not None:
        masks["tail"] = one((row // tail == col_tok // tail) & (row > col_tok))
    return masks


def _chunk_scores(q, k, q_bf, k_bf, b, halves, tail, masks):
    c, w = q.shape
    d = w // 2
    qk = q * k
    diag = jnp.concatenate(
        [jnp.broadcast_to(jnp.sum(qk[:, :d], axis=-1, keepdims=True), (c, d)),
         jnp.broadcast_to(jnp.sum(qk[:, d:], axis=-1, keepdims=True), (c, d))], axis=1)
    a = diag.astype(BF16) * masks["diag"]
    for half in halves:
        if half >= BF16_ROWS:
            delta = b - _block_rows(b, half - 1, 2 * half)
            zero = jnp.zeros((half, w), BF16)
            qe, ke = [], []
            for s in range(0, c, 2 * half):
                lo, up = slice(s, s + half), slice(s + half, s + 2 * half)
                ke += [k_bf[lo] * jnp.exp2(-delta[lo]).astype(BF16), zero]
                qe += [zero, q_bf[up] * jnp.exp2(delta[up]).astype(BF16)]
            qe, ke = jnp.concatenate(qe, axis=0), jnp.concatenate(ke, axis=0)
        else:
            row = lax.broadcasted_iota(jnp.int32, (c, w), 0)
            upper = row % (2 * half) >= half
            ref = _level_ref(b, half, row)
            e = jnp.exp2(jnp.where(upper, b - ref, ref - b))
            qe = jnp.where(upper, q * e, 0.0).astype(BF16)
            ke = jnp.where(upper, 0.0, k * e).astype(BF16)
        p = _dot_nt(qe, _block_diag2_aligned(ke)).astype(BF16)
        a = a + (p if 2 * half == c else p * masks[half])
    if tail is not None:
        delta = b - _block_rows(b, tail // 2 - 1, tail)
        qe = q_bf * jnp.exp2(delta).astype(BF16)
        ke = k_bf * jnp.exp2(-delta).astype(BF16)
        a = a + _dot_nt(qe, _block_diag2_aligned(ke)).astype(BF16) * masks["tail"]
    return a


def _chunk_apply(q_bf, k_bf, b, v, a, st_a, st_b):
    c, w = b.shape
    d = w // 2
    st_bd = _block_diag2_aligned(jnp.concatenate([st_a, st_b], axis=1).astype(BF16))
    o = _dot_nt(q_bf * jnp.exp2(b).astype(BF16), st_bd) + _dot(a, _block_diag2_aligned(v))
    b_last = b[c - 1:c, :]
    k_hat = k_bf * jnp.exp2(b_last - b).astype(BF16)
    ds = _dot_tn(v, k_hat)
    decay = jnp.exp2(b_last)
    st_a = st_a * decay[:, :d] + ds[:d, :d]
    st_b = st_b * decay[:, d:] + ds[d:, d:]
    return o, st_a, st_b


def _l0_prompt_kernel(x_ref, pg_ref, w_ref, lbl_ref, on_ref, og_ref, st_ref,
                      s_scr, q_scr, k_scr, b_scr, v_scr, z_scr):
    l = pl.program_id(1)
    tm = x_ref.shape[1]
    width = q_scr.shape[1]
    d = HGRN_HEAD_DIM
    n_heads = width // d

    @pl.when(l == 0)
    def _():
        s_scr[...] = jnp.zeros_like(s_scr)

    xn = (_rms(x_ref[0]) * pg_ref[...]).astype(BF16)
    ti = lax.broadcasted_iota(jnp.int32, (CHUNK, CHUNK), 0)
    tj = lax.broadcasted_iota(jnp.int32, (CHUNK, CHUNK), 1)
    tri = jnp.where(tj <= ti, 1.0, 0.0).astype(BF16)
    tri2 = jnp.concatenate([tri, tri], axis=1)
    worst = jnp.zeros((1, PROJ_COLS), F32)
    for j in range(width // PROJ_COLS):
        cols = slice(j * PROJ_COLS, (j + 1) * PROJ_COLS)
        part = lambda i: w_ref[:, i * width + j * PROJ_COLS:i * width + (j + 1) * PROJ_COLS]
        lb = _lower_bound(lbl_ref[:, cols])
        q_scr[:, cols] = _silu(_dot(xn, part(0)))
        f = lb + (1.0 - lb) * _sigmoid(_dot(xn, part(1)))
        k_scr[:, cols] = 1.0 - f
        v_scr[:, cols] = _dot(xn, part(2)).astype(BF16)
        z_scr[:, cols] = _silu(_dot(xn, part(3))) * on_ref[:, cols]
        g = jnp.log2(f)
        g_hi = g.astype(BF16)
        r1 = g - g_hi.astype(F32)
        g_mid = r1.astype(BF16)
        g_lo = (r1 - g_mid.astype(F32)).astype(BF16)
        for c in range(tm // CHUNK):
            rows = slice(c * CHUNK, (c + 1) * CHUNK)
            b = _dot(tri2, jnp.concatenate([g_hi[rows], g_mid[rows]], axis=0)) + _dot(tri, g_lo[rows])
            b_scr[rows, cols] = b
            for s in range(0, CHUNK, FAST_TAIL):
                mid = b[s + FAST_TAIL // 2 - 1:s + FAST_TAIL // 2, :]
                worst = jnp.maximum(
                    worst, jnp.maximum(b[s:s + 1, :] - mid, mid - b[s + FAST_TAIL - 1:s + FAST_TAIL, :]))
    fast = jnp.max(worst) <= FAST_MAX_EXPONENT * LOG2_E

    def scan(halves, tail, group):
        masks = _score_masks(CHUNK, 2 * d, halves, tail)
        chunks = [slice(c * CHUNK, (c + 1) * CHUNK) for c in range(tm // CHUNK)]

        def group_body(gi, carry):
            pairs = [gi * group + p for p in range(group)]
            cols = [pl.multiple_of(pi * 2 * d, 2 * d) for pi in pairs]
            load = lambda ref, rows, col: ref[rows, pl.ds(col, 2 * d)]
            states = [(s_scr[2 * pi], s_scr[2 * pi + 1]) for pi in pairs]
            operands = {}
            for n, rows in enumerate(chunks):
                for p, col in enumerate(cols):
                    q, k, b = load(q_scr, rows, col), load(k_scr, rows, col), load(b_scr, rows, col)
                    q_bf, k_bf = q.astype(BF16), k.astype(BF16)
                    operands[p, n] = (q_bf, k_bf, b, _chunk_scores(q, k, q_bf, k_bf, b, halves, tail, masks))
            for n, rows in enumerate(chunks):
                for p, col in enumerate(cols):
                    q_bf, k_bf, b, a = operands[p, n]
                    o, st_a, st_b = _chunk_apply(q_bf, k_bf, b, load(v_scr, rows, col), a, *states[p])
                    states[p] = (st_a, st_b)
                    on = jnp.concatenate([_rms(o[:, :d]), _rms(o[:, d:])], axis=1)
                    og_ref[0, rows, pl.ds(col, 2 * d)] = (on * load(z_scr, rows, col)).astype(og_ref.dtype)
            for pi, (st_a, st_b) in zip(pairs, states):
                s_scr[2 * pi] = st_a
                s_scr[2 * pi + 1] = st_b
            return carry

        lax.fori_loop(0, n_heads // 2 // group, group_body, 0)

    @pl.when(fast)
    def _():
        scan(tuple(h for h in LEVEL_HALVES if h >= FAST_TAIL), FAST_TAIL, group=4)

    @pl.when(jnp.logical_not(fast))
    def _():
        scan(LEVEL_HALVES, None, group=1)

    @pl.when(l == pl.num_programs(1) - 1)
    def _():
        for h in range(n_heads):
            st_ref[0, h] = s_scr[h].T


def _l0_prompt(x, pre_g, w_in, lb_logits, onorm, tm=L0_TILE):
    bsz, seq, dm = x.shape
    width = w_in.shape[1] // 4
    n_heads = width // HGRN_HEAD_DIM
    d = HGRN_HEAD_DIM
    const = lambda *shape: pl.BlockSpec(shape, lambda b, l: (0,) * len(shape))
    return pl.pallas_call(
        _l0_prompt_kernel,
        grid=(bsz, seq // tm),
        in_specs=[
            pl.BlockSpec((1, tm, dm), lambda b, l: (b, l, 0)),
            const(1, dm),
            pl.BlockSpec((dm, 4 * width), lambda b, l: (0, 0), pipeline_mode=pl.Buffered(1)),
            const(*lb_logits.shape),
            const(1, width),
        ],
        out_specs=[
            pl.BlockSpec((1, tm, width), lambda b, l: (b, l, 0)),
            pl.BlockSpec((1, n_heads, d, d), lambda b, l: (b, 0, 0, 0)),
        ],
        out_shape=[
            jax.ShapeDtypeStruct((bsz, seq, width), BF16),
            jax.ShapeDtypeStruct((bsz, n_heads, d, d), F32),
        ],
        scratch_shapes=[
            pltpu.VMEM((n_heads, d, d), F32),
            pltpu.VMEM((tm, width), F32),
            pltpu.VMEM((tm, width), F32),
            pltpu.VMEM((tm, width), F32),
            pltpu.VMEM((tm, width), BF16),
            pltpu.VMEM((tm, width), F32),
        ],
        compiler_params=pltpu.CompilerParams(
            dimension_semantics=("arbitrary", "arbitrary"), vmem_limit_bytes=VMEM_LIMIT_BYTES),
        name="l0_prompt",
    )(x, pre_g, w_in, lb_logits, onorm)


def _epilogue_kernel(og_ref, h_ref, p_ref, wo_ref, pg_ref, wpe_ref, wpg_ref, out_ref):
    mix = _dot(og_ref[...].astype(BF16), wo_ref[...])
    hn = h_ref[...] + _rms(mix) * pg_ref[...]
    pe = _dot(p_ref[...].astype(BF16), wpe_ref[...])
    gate = _sigmoid(_dot(hn.astype(BF16), wpg_ref[...]))
    out_ref[...] = hn + pe * gate


def _epilogue(og, h, p_all, layer, w_out, post_g, w_pe, w_pg, tm):
    t, dm = h.shape
    width = og.shape[1]
    ple = p_all.shape[2]
    tm = min(tm, t)
    const = lambda *shape: pl.BlockSpec(shape, lambda i: (0,) * len(shape))
    return pl.pallas_call(
        _epilogue_kernel,
        grid=(t // tm,),
        in_specs=[
            pl.BlockSpec((tm, width), lambda i: (i, 0)),
            pl.BlockSpec((tm, dm), lambda i: (i, 0)),
            pl.BlockSpec((None, tm, ple), lambda i: (layer, i, 0)),
            const(width, dm), const(1, dm), const(ple, dm), const(dm, dm),
        ],
        out_specs=pl.BlockSpec((tm, dm), lambda i: (i, 0)),
        out_shape=jax.ShapeDtypeStruct((t, dm), F32),
        compiler_params=pltpu.CompilerParams(
            dimension_semantics=("arbitrary",), vmem_limit_bytes=VMEM_LIMIT_BYTES),
        name="epilogue",
    )(og, h, p_all, w_out, post_g, w_pe, w_pg)


def _dup_heads(x):
    pair_w = 2 * HEAD_DIM
    lane = lax.broadcasted_iota(jnp.int32, (x.shape[0], pair_w), 1)
    out = []
    for c in range(x.shape[1] // pair_w):
        both = x[:, c * pair_w:(c + 1) * pair_w]
        swapped = pltpu.roll(both, HEAD_DIM, 1)
        out += [jnp.where(lane < HEAD_DIM, both, swapped), jnp.where(lane < HEAD_DIM, swapped, both)]
    return jnp.concatenate(out, axis=1)


def _l1_prompt_kernel(sinks_ref, h_ref, g1_ref, kvg_ref, wkv_ref, win_ref, cos_ref, slo_ref, shi_ref,
                      og_ref, kc_ref, vc_ref, kx_scr, vx_scr, q_scr, z_scr):
    l = pl.program_id(1)
    tm = h_ref.shape[1]
    width = z_scr.shape[1]
    kvw = kx_scr.shape[1]
    kv_cols = kc_ref.shape[2]
    pair_w = 2 * HEAD_DIM
    n_pairs = width // pair_w
    pairs_per_kv = Q_PER_KV // 2

    @pl.when(l == 0)
    def _():
        kx_scr[0:WINDOW, :] = jnp.zeros((WINDOW, kvw), kx_scr.dtype)
        vx_scr[0:WINDOW, :] = jnp.zeros((WINDOW, kvw), vx_scr.dtype)

    @pl.when(l > 0)
    def _():
        kx_scr[0:WINDOW, :] = kx_scr[tm:tm + WINDOW, :]
        vx_scr[0:WINDOW, :] = vx_scr[tm:tm + WINDOW, :]

    h0 = _rms(h_ref[0])
    xn = (h0 * g1_ref[...]).astype(BF16)
    hk = (h0 * kvg_ref[...]).astype(BF16)
    cos, slo, shi = cos_ref[...], slo_ref[...], shi_ref[...]
    tile = lambda t, n: jnp.concatenate([t] * n, axis=1)

    q = _rope(_dot(xn, win_ref[:, 0:width]), tile(cos, n_pairs), tile(slo, n_pairs), tile(shi, n_pairs))
    q_scr[...] = (q * (ATTN_SCALE * LOG2_E)).astype(q_scr.dtype)
    z_scr[...] = _silu(_dot(xn, win_ref[:, width:2 * width]))

    kv = _dot(hk, wkv_ref[...])
    reps = kv_cols // pair_w
    k_rot = _rope(kv[:, :kv_cols], tile(cos, reps), tile(slo, reps), tile(shi, reps))
    v_new = kv[:, kv_cols:]
    kx_scr[WINDOW:WINDOW + tm, :] = _dup_heads(k_rot).astype(kx_scr.dtype)
    vx_scr[WINDOW:WINDOW + tm, :] = _dup_heads(v_new).astype(vx_scr.dtype)

    @pl.when(l == pl.num_programs(1) - 1)
    def _():
        kc_ref[0] = k_rot[tm - WINDOW:, :]
        vc_ref[0] = v_new[tm - WINDOW:, :]

    qi = lax.broadcasted_iota(jnp.int32, (WINDOW, 2 * WINDOW), 0)
    ki = lax.broadcasted_iota(jnp.int32, (WINDOW, 2 * WINDOW), 1)
    band = (ki > qi) & (ki <= qi + WINDOW)
    first_head = lax.broadcasted_iota(jnp.int32, (WINDOW, pair_w), 1) < HEAD_DIM

    def group_body(g, carry):
        kcol = pl.multiple_of(g * pair_w, pair_w)
        cols = [pl.multiple_of((g * pairs_per_kv + p) * pair_w, pair_w) for p in range(pairs_per_kv)]
        for c in range(tm // WINDOW):
            rows = slice(c * WINDOW, (c + 1) * WINDOW)
            keys = slice(c * WINDOW, (c + 2) * WINDOW)
            k_bd = _block_diag2(kx_scr[keys, pl.ds(kcol, pair_w)])
            v_bd = _block_diag2(vx_scr[keys, pl.ds(kcol, pair_w)])
            q_all = jnp.concatenate([q_scr[rows, pl.ds(col, pair_w)] for col in cols], axis=0)
            s_all = _dot_nt(q_all, k_bd)
            mask = band & (ki >= jnp.where(l == 0, WINDOW, 0)) if c == 0 else band
            probs, inv = [], []
            for p in range(pairs_per_kv):
                pair_probs, maxes, sums, pair_sinks = [], [], [], []
                for hh in range(2):
                    sink = sinks_ref[(g * pairs_per_kv + p) * 2 + hh] * LOG2_E
                    s = s_all[p * WINDOW:(p + 1) * WINDOW, hh * 2 * WINDOW:(hh + 1) * 2 * WINDOW]
                    s = jnp.where(mask, s, MASKED)
                    mx = jnp.maximum(jnp.max(s, axis=-1, keepdims=True), sink)
                    e = jnp.exp2(s - mx)
                    pair_probs.append(e.astype(BF16))
                    maxes.append(mx)
                    sums.append(jnp.sum(e, axis=-1, keepdims=True))
                    pair_sinks.append(sink)
                probs.append(jnp.concatenate(pair_probs, axis=1))
                mx2 = jnp.where(first_head, maxes[0], maxes[1])
                sum2 = jnp.where(first_head, sums[0], sums[1])
                sink2 = jnp.where(first_head, pair_sinks[0], pair_sinks[1])
                inv.append(1.0 / (sum2 + jnp.exp2(sink2 - mx2)))
            o_all = _dot(jnp.concatenate(probs, axis=0), v_bd)
            for p, col in enumerate(cols):
                o2 = o_all[p * WINDOW:(p + 1) * WINDOW] * inv[p]
                og_ref[0, rows, pl.ds(col, pair_w)] = (o2 * z_scr[rows, pl.ds(col, pair_w)]).astype(og_ref.dtype)
        return carry

    lax.fori_loop(0, n_pairs // pairs_per_kv, group_body, 0, unroll=2)


def _l1_prompt(h, g1, kvg, w_kv, w_in, sinks, tables, tm=L1_TILE):
    bsz, seq, dm = h.shape
    width = w_in.shape[1] // 2
    kv_cols = w_kv.shape[1] // 2
    kvw = 2 * kv_cols
    pair_w = 2 * HEAD_DIM
    const = lambda *shape: pl.BlockSpec(shape, lambda b, l: (0,) * len(shape))
    table = pl.BlockSpec((tm, pair_w), lambda b, l: (l, 0))
    weight = lambda *shape: pl.BlockSpec(shape, lambda b, l: (0, 0), pipeline_mode=pl.Buffered(1))
    return pl.pallas_call(
        _l1_prompt_kernel,
        grid=(bsz, seq // tm),
        in_specs=[
            pl.BlockSpec(memory_space=pltpu.SMEM),
            pl.BlockSpec((1, tm, dm), lambda b, l: (b, l, 0)),
            const(1, dm), const(1, dm), weight(dm, 2 * kv_cols), weight(dm, 2 * width),
            table, table, table,
        ],
        out_specs=[
            pl.BlockSpec((1, tm, width), lambda b, l: (b, l, 0)),
            pl.BlockSpec((1, WINDOW, kv_cols), lambda b, l: (b, 0, 0)),
            pl.BlockSpec((1, WINDOW, kv_cols), lambda b, l: (b, 0, 0)),
        ],
        out_shape=[
            jax.ShapeDtypeStruct((bsz, seq, width), BF16),
            jax.ShapeDtypeStruct((bsz, WINDOW, kv_cols), F32),
            jax.ShapeDtypeStruct((bsz, WINDOW, kv_cols), F32),
        ],
        scratch_shapes=[
            pltpu.VMEM((WINDOW + tm, kvw), BF16),
            pltpu.VMEM((WINDOW + tm, kvw), BF16),
            pltpu.VMEM((tm, width), BF16),
            pltpu.VMEM((tm, width), F32),
        ],
        compiler_params=pltpu.CompilerParams(
            dimension_semantics=("arbitrary", "arbitrary"), vmem_limit_bytes=VMEM_LIMIT_BYTES),
        name="l1_prompt",
    )(sinks, h, g1, kvg, w_kv, w_in, *tables)


def _norm_matmul_kernel(x_ref, g_ref, w_ref, o_ref):
    xn = (_rms(x_ref[...]) * g_ref[...]).astype(BF16)
    o_ref[...] = _dot(xn, w_ref[...])


def _norm_matmul(x, g, w, tn=2048):
    t, dm = x.shape
    n = w.shape[1]
    tn = min(tn, n)
    return pl.pallas_call(
        _norm_matmul_kernel,
        grid=(n // tn,),
        in_specs=[
            pl.BlockSpec((t, dm), lambda j: (0, 0)),
            pl.BlockSpec((1, dm), lambda j: (0, 0)),
            pl.BlockSpec((dm, tn), lambda j: (0, j)),
        ],
        out_specs=pl.BlockSpec((t, tn), lambda j: (0, j)),
        out_shape=jax.ShapeDtypeStruct((t, n), F32),
        compiler_params=pltpu.CompilerParams(
            dimension_semantics=("arbitrary",), vmem_limit_bytes=VMEM_LIMIT_BYTES),
        name="norm_matmul",
    )(x, g, w)


def _l0_sample_kernel(u_ref, lbl_ref, on_ref, s0_ref, s1_ref, og_ref):
    bt = u_ref.shape[0]
    width = og_ref.shape[1]
    d = HGRN_HEAD_DIM
    lb_all = _lower_bound(lbl_ref[...])

    for h in range(width // d):
        cols = slice(h * d, (h + 1) * d)
        lb = lb_all[:, cols]
        q = _silu(u_ref[:, h * d:(h + 1) * d])
        f = lb + (1.0 - lb) * _sigmoid(u_ref[:, width + h * d:width + (h + 1) * d])
        v = u_ref[:, 2 * width + h * d:2 * width + (h + 1) * d]
        z = _silu(u_ref[:, 3 * width + h * d:3 * width + (h + 1) * d])
        f_t = f.T
        q_b = q.astype(BF16)
        outs = []
        for j in range(bt):
            v_row = v[j:j + 1, :]
            s_new = v_row + f_t[:, j:j + 1] * (s0_ref[j, h] - v_row)
            s1_ref[j, h] = s_new
            outs.append(_dot(q_b[j:j + 1, :], s_new.astype(BF16)))
        o = jnp.concatenate(outs, axis=0)
        og_ref[:, cols] = _rms(o) * on_ref[:, cols] * z


def _l0_sample(u, lb_logits, onorm, state, bt=SAMPLE_TILE):
    n, heads, d, _ = state.shape
    width = heads * d
    return pl.pallas_call(
        _l0_sample_kernel,
        grid=(n // bt,),
        in_specs=[
            pl.BlockSpec((bt, 4 * width), lambda i: (i, 0)),
            pl.BlockSpec(lb_logits.shape, lambda i: (0, 0)),
            pl.BlockSpec((1, width), lambda i: (0, 0)),
            pl.BlockSpec((bt, heads, d, d), lambda i: (i, 0, 0, 0)),
        ],
        out_specs=[
            pl.BlockSpec((bt, heads, d, d), lambda i: (i, 0, 0, 0)),
            pl.BlockSpec((bt, width), lambda i: (i, 0)),
        ],
        out_shape=[
            jax.ShapeDtypeStruct(state.shape, F32),
            jax.ShapeDtypeStruct((n, width), F32),
        ],
        compiler_params=pltpu.CompilerParams(
            dimension_semantics=("arbitrary",), vmem_limit_bytes=VMEM_LIMIT_BYTES),
        name="l0_sample",
    )(u, lb_logits, onorm, state)


def _l1_sample_front_kernel(h_ref, g1_ref, kvg_ref, win_ref, wkv_ref, cos_ref, slo_ref, shi_ref,
                            q_ref, z_ref, k_ref, v_ref):
    width = q_ref.shape[1]
    kvw = k_ref.shape[1]
    h0 = _rms(h_ref[...])
    xn = (h0 * g1_ref[...]).astype(BF16)
    hk = (h0 * kvg_ref[...]).astype(BF16)
    cos, slo, shi = cos_ref[...], slo_ref[...], shi_ref[...]
    q = _rope(_dot(xn, win_ref[:, 0:width]), cos, slo, shi)
    q_ref[...] = q * ATTN_SCALE
    z_ref[...] = _silu(_dot(xn, win_ref[:, width:2 * width]))
    kv = _dot(hk, wkv_ref[...])
    k_ref[...] = _rope(kv[:, :kvw], cos[:, :kvw], slo[:, :kvw], shi[:, :kvw])
    v_ref[...] = kv[:, kvw:]


def _l1_sample_front(h, g1, kvg, w_in, w_kv, tables):
    t, dm = h.shape
    width = w_in.shape[1] // 2
    kvw = w_kv.shape[1] // 2
    full = lambda a: pl.BlockSpec(a.shape, lambda i: (0,) * a.ndim)
    args = (h, g1, kvg, w_in, w_kv, *tables)
    return pl.pallas_call(
        _l1_sample_front_kernel,
        grid=(1,),
        in_specs=[full(a) for a in args],
        out_specs=[pl.BlockSpec((t, width), lambda i: (0, 0)), pl.BlockSpec((t, width), lambda i: (0, 0)),
                   pl.BlockSpec((t, kvw), lambda i: (0, 0)), pl.BlockSpec((t, kvw), lambda i: (0, 0))],
        out_shape=[jax.ShapeDtypeStruct((t, width), F32), jax.ShapeDtypeStruct((t, width), F32),
                   jax.ShapeDtypeStruct((t, kvw), F32), jax.ShapeDtypeStruct((t, kvw), F32)],
        compiler_params=pltpu.CompilerParams(
            dimension_semantics=("arbitrary",), vmem_limit_bytes=VMEM_LIMIT_BYTES),
        name="l1_sample_front",
    )(*args)


def _l1_sample_attn_kernel(q_ref, z_ref, kn_ref, vn_ref, ckt_ref, cvt_ref, sink_ref, og_ref):
    bt, kv_heads, hd, wb = ckt_ref.shape
    n_q = q_ref.shape[1]
    kvw = kv_heads * hd
    valid = lax.broadcasted_iota(jnp.int32, (n_q, wb), 1) > wb - WINDOW
    head_kv = lax.broadcasted_iota(jnp.int32, (n_q, kvw), 0) // Q_PER_KV
    own_lanes = lax.broadcasted_iota(jnp.int32, (n_q, kvw), 1) // hd == head_kv
    row_kv = lax.broadcasted_iota(jnp.int32, (n_q, hd), 0) // Q_PER_KV
    sink = sink_ref[...]
    scored = []
    for j in range(bt):
        q = q_ref[j]
        q_bd = jnp.where(own_lanes, jnp.concatenate([q] * kv_heads, axis=1), 0.0)
        scored.append((q_bd, _dot(q_bd.astype(BF16), ckt_ref[j].reshape(kvw, wb).astype(BF16))))
    weighted = []
    for j, (q_bd, s) in enumerate(scored):
        s = jnp.where(valid, s, MASKED)
        s_new = jnp.sum(q_bd * kn_ref[j:j + 1, :], axis=-1, keepdims=True)
        mx = jnp.maximum(jnp.maximum(jnp.max(s, axis=-1, keepdims=True), s_new), sink)
        e = jnp.exp(s - mx)
        e_new = jnp.exp(s_new - mx)
        weighted.append((e.astype(BF16), e_new, jnp.sum(e, axis=-1, keepdims=True) + e_new + jnp.exp(sink - mx)))
    for j, (e, e_new, den) in enumerate(weighted):
        o_all = _dot_nt(e, cvt_ref[j].reshape(kvw, wb).astype(BF16)) + e_new * vn_ref[j:j + 1, :]
        o = jnp.zeros((n_q, hd), F32)
        for g in range(kv_heads):
            o = o + jnp.where(row_kv == g, o_all[:, g * hd:(g + 1) * hd], 0.0)
        og_ref[j] = o / den * z_ref[j]


def _l1_sample_attn(q3, z3, k_new, v_new, cache_kt, cache_vt, sinks_col, bt=SAMPLE_TILE):
    n, heads, hd = q3.shape
    kv_heads, wb = cache_kt.shape[1], cache_kt.shape[3]
    qspec = pl.BlockSpec((bt, heads, hd), lambda i: (i, 0, 0))
    nspec = pl.BlockSpec((bt, kv_heads * hd), lambda i: (i, 0))
    cspec = pl.BlockSpec((bt, kv_heads, hd, wb), lambda i: (i, 0, 0, 0))
    return pl.pallas_call(
        _l1_sample_attn_kernel,
        grid=(n // bt,),
        in_specs=[qspec, qspec, nspec, nspec, cspec, cspec, pl.BlockSpec((heads, 1), lambda i: (0, 0))],
        out_specs=qspec,
        out_shape=jax.ShapeDtypeStruct((n, heads, hd), F32),
        compiler_params=pltpu.CompilerParams(
            dimension_semantics=("arbitrary",), vmem_limit_bytes=VMEM_LIMIT_BYTES),
        name="l1_sample_attn",
    )(q3, z3, k_new, v_new, cache_kt, cache_vt, sinks_col)


def kernel(x_prompt, x_sample, p_prompt, p_sample, state_hgrn, cache_k, cache_v, pre_norm_g, post_norm_g,
           w_in_a, lb_logits, onorm_a, w_out_a, kv_norm_g, w_kv, w_in_b, sinks, w_out_b, w_pe, w_pg):
    bsz, seq, dm = x_prompt.shape
    n_s = x_sample.shape[0]
    assert x_sample.shape[1] == 1 and w_in_a.shape[0] == 1 and w_in_b.shape[0] == 1
    assert seq % L0_TILE == 0 and seq % L1_TILE == 0 and n_s % SAMPLE_TILE == 0
    kv_heads = w_kv.shape[1] // (2 * HEAD_DIM)
    n_q = w_in_b.shape[2] // (2 * HEAD_DIM)
    assert n_q == kv_heads * Q_PER_KV

    row = lambda a: a.reshape(1, -1)
    bf = lambda a: a.astype(BF16)
    w_in_a0, w_out_a0 = bf(w_in_a[0]), bf(w_out_a[0])
    w_in_b0, w_out_b0 = bf(w_in_b[0]), bf(w_out_b[0])
    w_pe_b, w_pg_b = bf(w_pe), bf(w_pg)
    w_kv_b = bf(w_kv)

    pp = p_prompt.reshape(p_prompt.shape[0], bsz * seq, -1)
    ps = p_sample.reshape(p_sample.shape[0], n_s, -1)

    og0, st_p = _l0_prompt(x_prompt, row(pre_norm_g[0]), w_in_a0, lb_logits, row(onorm_a[0]))
    h1 = _epilogue(og0.reshape(bsz * seq, -1), x_prompt.reshape(bsz * seq, dm), pp, 0,
                   w_out_a0, row(post_norm_g[0]), w_pe_b[0], w_pg_b[0], tm=EPILOGUE_TILE)
    tables_p = _rope_tables(jnp.arange(seq), 2)
    og1, k_p, v_p = _l1_prompt(h1.reshape(bsz, seq, dm), row(pre_norm_g[1]), row(kv_norm_g), w_kv_b, w_in_b0,
                               sinks[0], tables_p)
    y_p = _epilogue(og1.reshape(bsz * seq, -1), h1, pp, 1,
                    w_out_b0, row(post_norm_g[1]), w_pe_b[1], w_pg_b[1], tm=EPILOGUE_TILE)

    xs = x_sample.reshape(n_s, dm)
    u0 = _norm_matmul(xs, row(pre_norm_g[0]), w_in_a0)
    st_s, og0s = _l0_sample(u0, lb_logits, row(onorm_a[0]), state_hgrn[0])
    h1s = _epilogue(og0s, xs, ps, 0,
                    w_out_a0, row(post_norm_g[0]), w_pe_b[0], w_pg_b[0], tm=n_s)
    tables_s = _rope_tables(PAST_LEN + jnp.arange(1), n_q)
    q_s, z_s, k_s, v_s = _l1_sample_front(h1s, row(pre_norm_g[1]), row(kv_norm_g), w_in_b0, w_kv_b, tables_s)
    og1s = _l1_sample_attn(q_s.reshape(n_s, n_q, HEAD_DIM), z_s.reshape(n_s, n_q, HEAD_DIM), k_s, v_s,
                           cache_k.transpose(0, 2, 3, 1), cache_v.transpose(0, 2, 3, 1),
                           sinks[0].reshape(n_q, 1))
    y_s = _epilogue(og1s.reshape(n_s, -1), h1s, ps, 1,
                    w_out_b0, row(post_norm_g[1]), w_pe_b[1], w_pg_b[1], tm=n_s)

    return (y_p.reshape(bsz, seq, dm), y_s.reshape(n_s, 1, dm),
            st_p[None], st_s[None],
            k_p.reshape(bsz, WINDOW, kv_heads, HEAD_DIM), v_p.reshape(bsz, WINDOW, kv_heads, HEAD_DIM),
            k_s.reshape(n_s, 1, kv_heads, HEAD_DIM), v_s.reshape(n_s, 1, kv_heads, HEAD_DIM))
```

```python
import jax
import jax.numpy as jnp
from jax import lax
from jax.experimental import pallas as pl
from jax.experimental.pallas import tpu as pltpu

F32 = jnp.float32
BF16 = jnp.bfloat16

EPS = 1e-6
HGRN_HEAD_DIM = 128
HEAD_DIM = 64
Q_PER_KV = 8
WINDOW = 128
ROT_DIM = 16
ROPE_THETA = 500000.0
PAST_LEN = 16384
ATTN_SCALE = HEAD_DIM ** -0.5
LOG2_E = 1.4426950408889634

CHUNK = 128
LEVEL_HALVES = (64, 32, 16, 8, 4, 2, 1)
BF16_ROWS = 16
FAST_TAIL = 32
FAST_LOG2_BUDGET = 127.0 - 7.0 - 1.0
PROJ_COLS = 512
L0_TILE = 512
L1_TILE = 1024
EPILOGUE_TILE = 1024
SAMPLE_TILE = 8
MASKED = -1e30
V7X_VMEM_BYTES = 64 * 1024 * 1024
VMEM_LIMIT_BYTES = V7X_VMEM_BYTES - 4 * 1024 * 1024


def _sigmoid(x):
    return 1.0 / (1.0 + jnp.exp(-x))


def _silu(x):
    return x * _sigmoid(x)


def _rms(x):
    return x * lax.rsqrt(jnp.mean(x * x, axis=-1, keepdims=True) + EPS)


def _dot(a, b):
    return jnp.dot(a, b, preferred_element_type=F32)


def _dot_nt(a, b):
    return lax.dot_general(a, b, (((1,), (1,)), ((), ())), preferred_element_type=F32)


def _dot_tn(a, b):
    return lax.dot_general(a, b, (((0,), (0,)), ((), ())), preferred_element_type=F32)


def _block_diag2(a):
    c = a.shape[1] // 2
    lane = lax.broadcasted_iota(jnp.int32, a.shape, 1)
    zero = jnp.zeros_like(a)
    return jnp.concatenate([jnp.where(lane < c, a, zero), jnp.where(lane >= c, a, zero)], axis=0)


def _rope(x, cos, sin_lo, sin_hi):
    n = x.shape[-1]
    half = ROT_DIM // 2
    return x * cos + pltpu.roll(x, n - half, 1) * sin_lo + pltpu.roll(x, half, 1) * sin_hi


def _rope_tables(pos, reps):
    half = ROT_DIM // 2
    inv = ROPE_THETA ** (-jnp.arange(0, ROT_DIM, 2, dtype=F32) / ROT_DIM)
    ang = pos.astype(F32)[:, None] * inv[None, :]
    cos, sin = jnp.cos(ang), jnp.sin(ang)
    n = pos.shape[0]
    pad = jnp.zeros((n, HEAD_DIM - ROT_DIM), F32)
    zero = jnp.zeros((n, half), F32)
    c = jnp.concatenate([cos, cos, pad + 1.0], axis=1)
    s_lo = jnp.concatenate([-sin, zero, pad], axis=1)
    s_hi = jnp.concatenate([zero, sin, pad], axis=1)
    return tuple(jnp.tile(t, (1, reps)) for t in (c, s_lo, s_hi))


def _lower_bound(lbl):
    m = jnp.max(lbl, axis=0, keepdims=True)
    e = jnp.exp(lbl - m)
    return e[0:1] / jnp.sum(e, axis=0, keepdims=True)


def _level_ref(b, half, row):
    if half >= 8:
        return _block_rows(b, half - 1, 2 * half)
    n = b.shape[0]
    up = lambda x, d: pltpu.roll(x, n - d, 0)
    down = lambda x, d: pltpu.roll(x, d, 0)
    if half == 1:
        return jnp.where(row % 2 == 1, down(b, 1), b)
    y1 = jnp.where(row % 2 == 0, up(b, 1), b)
    if half == 2:
        return jnp.where(row % 4 >= 2, down(y1, 2), y1)
    y2 = jnp.where(row % 4 < 2, up(y1, 2), y1)
    return jnp.where(row % 8 >= 4, down(y2, 4), y2)


def _block_rows(b, offset, size):
    pieces = [jnp.broadcast_to(b[s + offset:s + offset + 1, :], (size, b.shape[1]))
              for s in range(0, b.shape[0], size)]
    return pieces[0] if len(pieces) == 1 else jnp.concatenate(pieces, axis=0)


def _block_diag2_aligned(a):
    c = a.shape[1] // 2
    zero = jnp.zeros((a.shape[0], c), a.dtype)
    return jnp.concatenate([jnp.concatenate([a[:, :c], zero], axis=1),
                            jnp.concatenate([zero, a[:, c:]], axis=1)], axis=0)


def _score_masks(c, w, halves, tail):
    d = w // 2
    row = lax.broadcasted_iota(jnp.int32, (c, w), 0)
    col_tok = lax.broadcasted_iota(jnp.int32, (c, w), 1) % d
    one = lambda cond: jnp.where(cond, 1.0, 0.0).astype(BF16)
    masks = {"diag": one(row == col_tok)}
    for half in halves:
        masks[half] = one(row // (2 * half) == col_tok // (2 * half))
    if tail is not None:
        masks["tail"] = one((row // tail == col_tok // tail) & (row > col_tok))
    return masks


def _chunk_scores(q, k, q_bf, k_bf, b, halves, tail, masks):
    c, w = q.shape
    d = w // 2
    qk = q * k
    diag = jnp.concatenate(
        [jnp.broadcast_to(jnp.sum(qk[:, :d], axis=-1, keepdims=True), (c, d)),
         jnp.broadcast_to(jnp.sum(qk[:, d:], axis=-1, keepdims=True), (c, d))], axis=1)
    a = diag.astype(BF16) * masks["diag"]
    for half in halves:
        if half >= BF16_ROWS:
            delta = b - _block_rows(b, half - 1, 2 * half)
            zero = jnp.zeros((half, w), BF16)
            qe, ke = [], []
            for s in range(0, c, 2 * half):
                lo, up = slice(s, s + half), slice(s + half, s + 2 * half)
                ke += [k_bf[lo] * jnp.exp2(-delta[lo]).astype(BF16), zero]
                qe += [zero, q_bf[up] * jnp.exp2(delta[up]).astype(BF16)]
            qe, ke = jnp.concatenate(qe, axis=0), jnp.concatenate(ke, axis=0)
        else:
            row = lax.broadcasted_iota(jnp.int32, (c, w), 0)
            upper = row % (2 * half) >= half
            ref = _level_ref(b, half, row)
            e = jnp.exp2(jnp.where(upper, b - ref, ref - b))
            qe = jnp.where(upper, q * e, 0.0).astype(BF16)
            ke = jnp.where(upper, 0.0, k * e).astype(BF16)
        p = _dot_nt(qe, _block_diag2_aligned(ke)).astype(BF16)
        a = a + (p if 2 * half == c else p * masks[half])
    if tail is not None:
        delta = b - _block_rows(b, tail // 2 - 1, tail)
        qe = q_bf * jnp.exp2(delta).astype(BF16)
        ke = k_bf * jnp.exp2(-delta).astype(BF16)
        a = a + _dot_nt(qe, _block_diag2_aligned(ke)).astype(BF16) * masks["tail"]
    return a


def _chunk_apply(q_bf, k_bf, b, v, a, st_a, st_b):
    c, w = b.shape
    d = w // 2
    st_bd = _block_diag2_aligned(jnp.concatenate([st_a, st_b], axis=1).astype(BF16))
    o = _dot_nt(q_bf * jnp.exp2(b).astype(BF16), st_bd) + _dot(a, _block_diag2_aligned(v))
    b_last = b[c - 1:c, :]
    k_hat = k_bf * jnp.exp2(b_last - b).astype(BF16)
    ds = _dot_tn(v, k_hat)
    decay = jnp.exp2(b_last)
    st_a = st_a * decay[:, :d] + ds[:d, :d]
    st_b = st_b * decay[:, d:] + ds[d:, d:]
    return o, st_a, st_b


def _l0_prompt_kernel(budget_ref, x_ref, pg_ref, w_ref, lbl_ref, on_ref, og_ref, st_ref,
                      s_scr, q_scr, k_scr, b_scr, v_scr, z_scr):
    l = pl.program_id(1)
    tm = x_ref.shape[1]
    width = q_scr.shape[1]
    d = HGRN_HEAD_DIM
    n_heads = width // d

    @pl.when(l == 0)
    def _():
        s_scr[...] = jnp.zeros_like(s_scr)

    xn = (_rms(x_ref[0]) * pg_ref[...]).astype(BF16)
    ti = lax.broadcasted_iota(jnp.int32, (CHUNK, CHUNK), 0)
    tj = lax.broadcasted_iota(jnp.int32, (CHUNK, CHUNK), 1)
    tri = jnp.where(tj <= ti, 1.0, 0.0).astype(BF16)
    tri2 = jnp.concatenate([tri, tri], axis=1)
    worst = jnp.zeros((1, PROJ_COLS), F32)
    for j in range(width // PROJ_COLS):
        cols = slice(j * PROJ_COLS, (j + 1) * PROJ_COLS)
        part = lambda i: w_ref[:, i * width + j * PROJ_COLS:i * width + (j + 1) * PROJ_COLS]
        lb = _lower_bound(lbl_ref[:, cols])
        q_scr[:, cols] = _silu(_dot(xn, part(0)))
        f = lb + (1.0 - lb) * _sigmoid(_dot(xn, part(1)))
        k_scr[:, cols] = 1.0 - f
        v_scr[:, cols] = _dot(xn, part(2)).astype(BF16)
        z_scr[:, cols] = _silu(_dot(xn, part(3))) * on_ref[:, cols]
        g = jnp.log2(f)
        g_hi = g.astype(BF16)
        r1 = g - g_hi.astype(F32)
        g_mid = r1.astype(BF16)
        g_lo = (r1 - g_mid.astype(F32)).astype(BF16)
        for c in range(tm // CHUNK):
            rows = slice(c * CHUNK, (c + 1) * CHUNK)
            b = _dot(tri2, jnp.concatenate([g_hi[rows], g_mid[rows]], axis=0)) + _dot(tri, g_lo[rows])
            b_scr[rows, cols] = b
            for s in range(0, CHUNK, FAST_TAIL):
                mid = b[s + FAST_TAIL // 2 - 1:s + FAST_TAIL // 2, :]
                worst = jnp.maximum(
                    worst, jnp.maximum(b[s:s + 1, :] - mid, mid - b[s + FAST_TAIL - 1:s + FAST_TAIL, :]))
    fast = 2.0 * jnp.max(worst) <= budget_ref[0]

    def scan(halves, tail, group):
        masks = _score_masks(CHUNK, 2 * d, halves, tail)
        chunks = [slice(c * CHUNK, (c + 1) * CHUNK) for c in range(tm // CHUNK)]

        def group_body(gi, carry):
            pairs = [gi * group + p for p in range(group)]
            cols = [pl.multiple_of(pi * 2 * d, 2 * d) for pi in pairs]
            load = lambda ref, rows, col: ref[rows, pl.ds(col, 2 * d)]
            states = [(s_scr[2 * pi], s_scr[2 * pi + 1]) for pi in pairs]
            operands = {}
            for n, rows in enumerate(chunks):
                for p, col in enumerate(cols):
                    q, k, b = load(q_scr, rows, col), load(k_scr, rows, col), load(b_scr, rows, col)
                    q_bf, k_bf = q.astype(BF16), k.astype(BF16)
                    operands[p, n] = (q_bf, k_bf, b, _chunk_scores(q, k, q_bf, k_bf, b, halves, tail, masks))
            for n, rows in enumerate(chunks):
                for p, col in enumerate(cols):
                    q_bf, k_bf, b, a = operands[p, n]
                    o, st_a, st_b = _chunk_apply(q_bf, k_bf, b, load(v_scr, rows, col), a, *states[p])
                    states[p] = (st_a, st_b)
                    on = jnp.concatenate([_rms(o[:, :d]), _rms(o[:, d:])], axis=1)
                    og_ref[0, rows, pl.ds(col, 2 * d)] = (on * load(z_scr, rows, col)).astype(og_ref.dtype)
            for pi, (st_a, st_b) in zip(pairs, states):
                s_scr[2 * pi] = st_a
                s_scr[2 * pi + 1] = st_b
            return carry

        lax.fori_loop(0, n_heads // 2 // group, group_body, 0)

    @pl.when(fast)
    def _():
        scan(tuple(h for h in LEVEL_HALVES if h >= FAST_TAIL), FAST_TAIL, group=4)

    @pl.when(jnp.logical_not(fast))
    def _():
        scan(LEVEL_HALVES, None, group=1)

    @pl.when(l == pl.num_programs(1) - 1)
    def _():
        for h in range(n_heads):
            st_ref[0, h] = s_scr[h].T


def _l0_prompt(x, pre_g, w_in, lb_logits, onorm, tm=L0_TILE):
    q_cols = w_in[:, :w_in.shape[1] // 4].astype(F32)
    q_bound = (x.shape[-1] ** 0.5) * jnp.max(jnp.abs(pre_g)) * jnp.sqrt(jnp.max(jnp.sum(q_cols * q_cols, axis=0)))
    budget = (FAST_LOG2_BUDGET - 1.0 - jnp.log2(q_bound)).astype(F32).reshape(1)
    bsz, seq, dm = x.shape
    width = w_in.shape[1] // 4
    n_heads = width // HGRN_HEAD_DIM
    d = HGRN_HEAD_DIM
    const = lambda *shape: pl.BlockSpec(shape, lambda b, l: (0,) * len(shape))
    return pl.pallas_call(
        _l0_prompt_kernel,
        grid=(bsz, seq // tm),
        in_specs=[
            pl.BlockSpec(memory_space=pltpu.SMEM),
            pl.BlockSpec((1, tm, dm), lambda b, l: (b, l, 0)),
            const(1, dm),
            pl.BlockSpec((dm, 4 * width), lambda b, l: (0, 0), pipeline_mode=pl.Buffered(1)),
            const(*lb_logits.shape),
            const(1, width),
        ],
        out_specs=[
            pl.BlockSpec((1, tm, width), lambda b, l: (b, l, 0)),
            pl.BlockSpec((1, n_heads, d, d), lambda b, l: (b, 0, 0, 0)),
        ],
        out_shape=[
            jax.ShapeDtypeStruct((bsz, seq, width), BF16),
            jax.ShapeDtypeStruct((bsz, n_heads, d, d), F32),
        ],
        scratch_shapes=[
            pltpu.VMEM((n_heads, d, d), F32),
            pltpu.VMEM((tm, width), F32),
            pltpu.VMEM((tm, width), F32),
            pltpu.VMEM((tm, width), F32),
            pltpu.VMEM((tm, width), BF16),
            pltpu.VMEM((tm, width), F32),
        ],
        compiler_params=pltpu.CompilerParams(
            dimension_semantics=("arbitrary", "arbitrary"), vmem_limit_bytes=VMEM_LIMIT_BYTES),
        name="l0_prompt",
    )(budget, x, pre_g, w_in, lb_logits, onorm)


def _epilogue_kernel(og_ref, h_ref, p_ref, wo_ref, pg_ref, wpe_ref, wpg_ref, out_ref):
    mix = _dot(og_ref[...].astype(BF16), wo_ref[...])
    hn = h_ref[...] + _rms(mix) * pg_ref[...]
    pe = _dot(p_ref[...].astype(BF16), wpe_ref[...])
    gate = _sigmoid(_dot(hn.astype(BF16), wpg_ref[...]))
    out_ref[...] = hn + pe * gate


def _epilogue(og, h, p_all, layer, w_out, post_g, w_pe, w_pg, tm):
    t, dm = h.shape
    width = og.shape[1]
    ple = p_all.shape[2]
    tm = min(tm, t)
    const = lambda *shape: pl.BlockSpec(shape, lambda i: (0,) * len(shape))
    return pl.pallas_call(
        _epilogue_kernel,
        grid=(t // tm,),
        in_specs=[
            pl.BlockSpec((tm, width), lambda i: (i, 0)),
            pl.BlockSpec((tm, dm), lambda i: (i, 0)),
            pl.BlockSpec((None, tm, ple), lambda i: (layer, i, 0)),
            const(width, dm), const(1, dm), const(ple, dm), const(dm, dm),
        ],
        out_specs=pl.BlockSpec((tm, dm), lambda i: (i, 0)),
        out_shape=jax.ShapeDtypeStruct((t, dm), F32),
        compiler_params=pltpu.CompilerParams(
            dimension_semantics=("arbitrary",), vmem_limit_bytes=VMEM_LIMIT_BYTES),
        name="epilogue",
    )(og, h, p_all, w_out, post_g, w_pe, w_pg)


def _dup_heads(x):
    pair_w = 2 * HEAD_DIM
    lane = lax.broadcasted_iota(jnp.int32, (x.shape[0], pair_w), 1)
    out = []
    for c in range(x.shape[1] // pair_w):
        both = x[:, c * pair_w:(c + 1) * pair_w]
        swapped = pltpu.roll(both, HEAD_DIM, 1)
        out += [jnp.where(lane < HEAD_DIM, both, swapped), jnp.where(lane < HEAD_DIM, swapped, both)]
    return jnp.concatenate(out, axis=1)


def _l1_prompt_kernel(sinks_ref, h_ref, g1_ref, kvg_ref, wkv_ref, win_ref, cos_ref, slo_ref, shi_ref,
                      og_ref, kc_ref, vc_ref, kx_scr, vx_scr, q_scr, z_scr):
    l = pl.program_id(1)
    tm = h_ref.shape[1]
    width = z_scr.shape[1]
    kvw = kx_scr.shape[1]
    kv_cols = kc_ref.shape[2]
    pair_w = 2 * HEAD_DIM
    n_pairs = width // pair_w
    pairs_per_kv = Q_PER_KV // 2

    @pl.when(l == 0)
    def _():
        kx_scr[0:WINDOW, :] = jnp.zeros((WINDOW, kvw), kx_scr.dtype)
        vx_scr[0:WINDOW, :] = jnp.zeros((WINDOW, kvw), vx_scr.dtype)

    @pl.when(l > 0)
    def _():
        kx_scr[0:WINDOW, :] = kx_scr[tm:tm + WINDOW, :]
        vx_scr[0:WINDOW, :] = vx_scr[tm:tm + WINDOW, :]

    h0 = _rms(h_ref[0])
    xn = (h0 * g1_ref[...]).astype(BF16)
    hk = (h0 * kvg_ref[...]).astype(BF16)
    cos, slo, shi = cos_ref[...], slo_ref[...], shi_ref[...]
    tile = lambda t, n: jnp.concatenate([t] * n, axis=1)

    q = _rope(_dot(xn, win_ref[:, 0:width]), tile(cos, n_pairs), tile(slo, n_pairs), tile(shi, n_pairs))
    q_scr[...] = (q * (ATTN_SCALE * LOG2_E)).astype(q_scr.dtype)
    z_scr[...] = _silu(_dot(xn, win_ref[:, width:2 * width]))

    kv = _dot(hk, wkv_ref[...])
    reps = kv_cols // pair_w
    k_rot = _rope(kv[:, :kv_cols], tile(cos, reps), tile(slo, reps), tile(shi, reps))
    v_new = kv[:, kv_cols:]
    kx_scr[WINDOW:WINDOW + tm, :] = _dup_heads(k_rot).astype(kx_scr.dtype)
    vx_scr[WINDOW:WINDOW + tm, :] = _dup_heads(v_new).astype(vx_scr.dtype)

    @pl.when(l == pl.num_programs(1) - 1)
    def _():
        kc_ref[0] = k_rot[tm - WINDOW:, :]
        vc_ref[0] = v_new[tm - WINDOW:, :]

    qi = lax.broadcasted_iota(jnp.int32, (WINDOW, 2 * WINDOW), 0)
    ki = lax.broadcasted_iota(jnp.int32, (WINDOW, 2 * WINDOW), 1)
    band = (ki > qi) & (ki <= qi + WINDOW)
    first_head = lax.broadcasted_iota(jnp.int32, (WINDOW, pair_w), 1) < HEAD_DIM

    def group_body(g, carry):
        kcol = pl.multiple_of(g * pair_w, pair_w)
        cols = [pl.multiple_of((g * pairs_per_kv + p) * pair_w, pair_w) for p in range(pairs_per_kv)]
        for c in range(tm // WINDOW):
            rows = slice(c * WINDOW, (c + 1) * WINDOW)
            keys = slice(c * WINDOW, (c + 2) * WINDOW)
            k_bd = _block_diag2(kx_scr[keys, pl.ds(kcol, pair_w)])
            v_bd = _block_diag2(vx_scr[keys, pl.ds(kcol, pair_w)])
            q_all = jnp.concatenate([q_scr[rows, pl.ds(col, pair_w)] for col in cols], axis=0)
            s_all = _dot_nt(q_all, k_bd)
            mask = band & (ki >= jnp.where(l == 0, WINDOW, 0)) if c == 0 else band
            probs, inv = [], []
            for p in range(pairs_per_kv):
                pair_probs, maxes, sums, pair_sinks = [], [], [], []
                for hh in range(2):
                    sink = sinks_ref[(g * pairs_per_kv + p) * 2 + hh] * LOG2_E
                    s = s_all[p * WINDOW:(p + 1) * WINDOW, hh * 2 * WINDOW:(hh + 1) * 2 * WINDOW]
                    s = jnp.where(mask, s, MASKED)
                    mx = jnp.maximum(jnp.max(s, axis=-1, keepdims=True), sink)
                    e = jnp.exp2(s - mx)
                    pair_probs.append(e.astype(BF16))
                    maxes.append(mx)
                    sums.append(jnp.sum(e, axis=-1, keepdims=True))
                    pair_sinks.append(sink)
                probs.append(jnp.concatenate(pair_probs, axis=1))
                mx2 = jnp.where(first_head, maxes[0], maxes[1])
                sum2 = jnp.where(first_head, sums[0], sums[1])
                sink2 = jnp.where(first_head, pair_sinks[0], pair_sinks[1])
                inv.append(1.0 / (sum2 + jnp.exp2(sink2 - mx2)))
            o_all = _dot(jnp.concatenate(probs, axis=0), v_bd)
            for p, col in enumerate(cols):
                o2 = o_all[p * WINDOW:(p + 1) * WINDOW] * inv[p]
                og_ref[0, rows, pl.ds(col, pair_w)] = (o2 * z_scr[rows, pl.ds(col, pair_w)]).astype(og_ref.dtype)
        return carry

    lax.fori_loop(0, n_pairs // pairs_per_kv, group_body, 0, unroll=2)


def _l1_prompt(h, g1, kvg, w_kv, w_in, sinks, tables, tm=L1_TILE):
    bsz, seq, dm = h.shape
    width = w_in.shape[1] // 2
    kv_cols = w_kv.shape[1] // 2
    kvw = 2 * kv_cols
    pair_w = 2 * HEAD_DIM
    const = lambda *shape: pl.BlockSpec(shape, lambda b, l: (0,) * len(shape))
    table = pl.BlockSpec((tm, pair_w), lambda b, l: (l, 0))
    weight = lambda *shape: pl.BlockSpec(shape, lambda b, l: (0, 0), pipeline_mode=pl.Buffered(1))
    return pl.pallas_call(
        _l1_prompt_kernel,
        grid=(bsz, seq // tm),
        in_specs=[
            pl.BlockSpec(memory_space=pltpu.SMEM),
            pl.BlockSpec((1, tm, dm), lambda b, l: (b, l, 0)),
            const(1, dm), const(1, dm), weight(dm, 2 * kv_cols), weight(dm, 2 * width),
            table, table, table,
        ],
        out_specs=[
            pl.BlockSpec((1, tm, width), lambda b, l: (b, l, 0)),
            pl.BlockSpec((1, WINDOW, kv_cols), lambda b, l: (b, 0, 0)),
            pl.BlockSpec((1, WINDOW, kv_cols), lambda b, l: (b, 0, 0)),
        ],
        out_shape=[
            jax.ShapeDtypeStruct((bsz, seq, width), BF16),
            jax.ShapeDtypeStruct((bsz, WINDOW, kv_cols), F32),
            jax.ShapeDtypeStruct((bsz, WINDOW, kv_cols), F32),
        ],
        scratch_shapes=[
            pltpu.VMEM((WINDOW + tm, kvw), BF16),
            pltpu.VMEM((WINDOW + tm, kvw), BF16),
            pltpu.VMEM((tm, width), BF16),
            pltpu.VMEM((tm, width), F32),
        ],
        compiler_params=pltpu.CompilerParams(
            dimension_semantics=("arbitrary", "arbitrary"), vmem_limit_bytes=VMEM_LIMIT_BYTES),
        name="l1_prompt",
    )(sinks, h, g1, kvg, w_kv, w_in, *tables)


def _norm_matmul_kernel(x_ref, g_ref, w_ref, o_ref):
    xn = (_rms(x_ref[...]) * g_ref[...]).astype(BF16)
    o_ref[...] = _dot(xn, w_ref[...])


def _norm_matmul(x, g, w, tn=2048):
    t, dm = x.shape
    n = w.shape[1]
    tn = min(tn, n)
    return pl.pallas_call(
        _norm_matmul_kernel,
        grid=(n // tn,),
        in_specs=[
            pl.BlockSpec((t, dm), lambda j: (0, 0)),
            pl.BlockSpec((1, dm), lambda j: (0, 0)),
            pl.BlockSpec((dm, tn), lambda j: (0, j)),
        ],
        out_specs=pl.BlockSpec((t, tn), lambda j: (0, j)),
        out_shape=jax.ShapeDtypeStruct((t, n), F32),
        compiler_params=pltpu.CompilerParams(
            dimension_semantics=("arbitrary",), vmem_limit_bytes=VMEM_LIMIT_BYTES),
        name="norm_matmul",
    )(x, g, w)


def _l0_sample_kernel(u_ref, lbl_ref, on_ref, s0_ref, s1_ref, og_ref):
    bt = u_ref.shape[0]
    width = og_ref.shape[1]
    d = HGRN_HEAD_DIM
    lb_all = _lower_bound(lbl_ref[...])
    terms = 4
    src = lax.broadcasted_iota(jnp.int32, (terms * bt, bt * d), 0) % bt
    dst = lax.broadcasted_iota(jnp.int32, (terms * bt, bt * d), 1) // d
    spread = jnp.where(src == dst, 1.0, 0.0).astype(BF16)

    for h in range(width // d):
        cols = slice(h * d, (h + 1) * d)
        lb = lb_all[:, cols]
        q = _silu(u_ref[:, h * d:(h + 1) * d])
        f = lb + (1.0 - lb) * _sigmoid(u_ref[:, width + h * d:width + (h + 1) * d])
        v = u_ref[:, 2 * width + h * d:2 * width + (h + 1) * d]
        z = _silu(u_ref[:, 3 * width + h * d:3 * width + (h + 1) * d])
        f_hi = f.astype(BF16).astype(F32)
        f_mid = (f - f_hi).astype(BF16).astype(F32)
        f_lo = (f - f_hi - f_mid).astype(BF16).astype(F32)
        f_terms = jnp.concatenate([f_hi, f_mid, f_lo, jnp.zeros_like(f)], axis=0)
        f_wide = _dot(f_terms.T.astype(BF16), spread)
        q_b = q.astype(BF16)
        outs = []
        for j in range(bt):
            v_row = v[j:j + 1, :]
            s_new = v_row + f_wide[:, j * d:(j + 1) * d] * (s0_ref[j, h] - v_row)
            s1_ref[j, h] = s_new
            outs.append(_dot(q_b[j:j + 1, :], s_new.astype(BF16)))
        o = jnp.concatenate(outs, axis=0)
        og_ref[:, cols] = _rms(o) * on_ref[:, cols] * z


def _l0_sample(u, lb_logits, onorm, state, bt=SAMPLE_TILE):
    n, heads, d, _ = state.shape
    width = heads * d
    return pl.pallas_call(
        _l0_sample_kernel,
        grid=(n // bt,),
        in_specs=[
            pl.BlockSpec((bt, 4 * width), lambda i: (i, 0)),
            pl.BlockSpec(lb_logits.shape, lambda i: (0, 0)),
            pl.BlockSpec((1, width), lambda i: (0, 0)),
            pl.BlockSpec((bt, heads, d, d), lambda i: (i, 0, 0, 0)),
        ],
        out_specs=[
            pl.BlockSpec((bt, heads, d, d), lambda i: (i, 0, 0, 0)),
            pl.BlockSpec((bt, width), lambda i: (i, 0)),
        ],
        out_shape=[
            jax.ShapeDtypeStruct(state.shape, F32),
            jax.ShapeDtypeStruct((n, width), F32),
        ],
        compiler_params=pltpu.CompilerParams(
            dimension_semantics=("arbitrary",), vmem_limit_bytes=VMEM_LIMIT_BYTES),
        name="l0_sample",
    )(u, lb_logits, onorm, state)


def _l1_sample_front_kernel(h_ref, g1_ref, kvg_ref, win_ref, wkv_ref, cos_ref, slo_ref, shi_ref,
                            q_ref, z_ref, k_ref, v_ref):
    width = q_ref.shape[1]
    kvw = k_ref.shape[1]
    h0 = _rms(h_ref[...])
    xn = (h0 * g1_ref[...]).astype(BF16)
    hk = (h0 * kvg_ref[...]).astype(BF16)
    cos, slo, shi = cos_ref[...], slo_ref[...], shi_ref[...]
    q = _rope(_dot(xn, win_ref[:, 0:width]), cos, slo, shi)
    q_ref[...] = q * ATTN_SCALE
    z_ref[...] = _silu(_dot(xn, win_ref[:, width:2 * width]))
    kv = _dot(hk, wkv_ref[...])
    k_ref[...] = _rope(kv[:, :kvw], cos[:, :kvw], slo[:, :kvw], shi[:, :kvw])
    v_ref[...] = kv[:, kvw:]


def _l1_sample_front(h, g1, kvg, w_in, w_kv, tables):
    t, dm = h.shape
    width = w_in.shape[1] // 2
    kvw = w_kv.shape[1] // 2
    full = lambda a: pl.BlockSpec(a.shape, lambda i: (0,) * a.ndim)
    args = (h, g1, kvg, w_in, w_kv, *tables)
    return pl.pallas_call(
        _l1_sample_front_kernel,
        grid=(1,),
        in_specs=[full(a) for a in args],
        out_specs=[pl.BlockSpec((t, width), lambda i: (0, 0)), pl.BlockSpec((t, width), lambda i: (0, 0)),
                   pl.BlockSpec((t, kvw), lambda i: (0, 0)), pl.BlockSpec((t, kvw), lambda i: (0, 0))],
        out_shape=[jax.ShapeDtypeStruct((t, width), F32), jax.ShapeDtypeStruct((t, width), F32),
                   jax.ShapeDtypeStruct((t, kvw), F32), jax.ShapeDtypeStruct((t, kvw), F32)],
        compiler_params=pltpu.CompilerParams(
            dimension_semantics=("arbitrary",), vmem_limit_bytes=VMEM_LIMIT_BYTES),
        name="l1_sample_front",
    )(*args)


def _l1_sample_attn_kernel(q_ref, z_ref, kn_ref, vn_ref, ckt_ref, cvt_ref, sink_ref, og_ref):
    bt, kv_heads, hd, wb = ckt_ref.shape
    n_q = q_ref.shape[1]
    kvw = kv_heads * hd
    valid = lax.broadcasted_iota(jnp.int32, (n_q, wb), 1) > wb - WINDOW
    head_kv = lax.broadcasted_iota(jnp.int32, (n_q, kvw), 0) // Q_PER_KV
    own_lanes = lax.broadcasted_iota(jnp.int32, (n_q, kvw), 1) // hd == head_kv
    row_kv = lax.broadcasted_iota(jnp.int32, (n_q, hd), 0) // Q_PER_KV
    sink = sink_ref[...]
    scored = []
    for j in range(bt):
        q = q_ref[j]
        q_bd = jnp.where(own_lanes, jnp.concatenate([q] * kv_heads, axis=1), 0.0)
        scored.append((q_bd, _dot(q_bd.astype(BF16), ckt_ref[j].reshape(kvw, wb).astype(BF16))))
    weighted = []
    for j, (q_bd, s) in enumerate(scored):
        s = jnp.where(valid, s, MASKED)
        s_new = jnp.sum(q_bd * kn_ref[j:j + 1, :], axis=-1, keepdims=True)
        mx = jnp.maximum(jnp.maximum(jnp.max(s, axis=-1, keepdims=True), s_new), sink)
        e = jnp.exp(s - mx)
        e_new = jnp.exp(s_new - mx)
        weighted.append((e.astype(BF16), e_new, jnp.sum(e, axis=-1, keepdims=True) + e_new + jnp.exp(sink - mx)))
    for j, (e, e_new, den) in enumerate(weighted):
        o_all = _dot_nt(e, cvt_ref[j].reshape(kvw, wb).astype(BF16)) + e_new * vn_ref[j:j + 1, :]
        o = jnp.zeros((n_q, hd), F32)
        for g in range(kv_heads):
            o = o + jnp.where(row_kv == g, o_all[:, g * hd:(g + 1) * hd], 0.0)
        og_ref[j] = o / den * z_ref[j]


def _l1_sample_attn(q3, z3, k_new, v_new, cache_kt, cache_vt, sinks_col, bt=SAMPLE_TILE):
    n, heads, hd = q3.shape
    kv_heads, wb = cache_kt.shape[1], cache_kt.shape[3]
    qspec = pl.BlockSpec((bt, heads, hd), lambda i: (i, 0, 0))
    nspec = pl.BlockSpec((bt, kv_heads * hd), lambda i: (i, 0))
    cspec = pl.BlockSpec((bt, kv_heads, hd, wb), lambda i: (i, 0, 0, 0))
    return pl.pallas_call(
        _l1_sample_attn_kernel,
        grid=(n // bt,),
        in_specs=[qspec, qspec, nspec, nspec, cspec, cspec, pl.BlockSpec((heads, 1), lambda i: (0, 0))],
        out_specs=qspec,
        out_shape=jax.ShapeDtypeStruct((n, heads, hd), F32),
        compiler_params=pltpu.CompilerParams(
            dimension_semantics=("arbitrary",), vmem_limit_bytes=VMEM_LIMIT_BYTES),
        name="l1_sample_attn",
    )(q3, z3, k_new, v_new, cache_kt, cache_vt, sinks_col)


def kernel(x_prompt, x_sample, p_prompt, p_sample, state_hgrn, cache_k, cache_v, pre_norm_g, post_norm_g,
           w_in_a, lb_logits, onorm_a, w_out_a, kv_norm_g, w_kv, w_in_b, sinks, w_out_b, w_pe, w_pg):
    bsz, seq, dm = x_prompt.shape
    n_s = x_sample.shape[0]
    assert x_sample.shape[1] == 1 and w_in_a.shape[0] == 1 and w_in_b.shape[0] == 1
    assert seq % L0_TILE == 0 and seq % L1_TILE == 0 and n_s % SAMPLE_TILE == 0
    kv_heads = w_kv.shape[1] // (2 * HEAD_DIM)
    n_q = w_in_b.shape[2] // (2 * HEAD_DIM)
    assert n_q == kv_heads * Q_PER_KV

    row = lambda a: a.reshape(1, -1)
    bf = lambda a: a.astype(BF16)
    w_in_a0, w_out_a0 = bf(w_in_a[0]), bf(w_out_a[0])
    w_in_b0, w_out_b0 = bf(w_in_b[0]), bf(w_out_b[0])
    w_pe_b, w_pg_b = bf(w_pe), bf(w_pg)
    w_kv_b = bf(w_kv)

    pp = p_prompt.reshape(p_prompt.shape[0], bsz * seq, -1)
    ps = p_sample.reshape(p_sample.shape[0], n_s, -1)

    og0, st_p = _l0_prompt(x_prompt, row(pre_norm_g[0]), w_in_a0, lb_logits, row(onorm_a[0]))
    h1 = _epilogue(og0.reshape(bsz * seq, -1), x_prompt.reshape(bsz * seq, dm), pp, 0,
                   w_out_a0, row(post_norm_g[0]), w_pe_b[0], w_pg_b[0], tm=EPILOGUE_TILE)
    tables_p = _rope_tables(jnp.arange(seq), 2)
    og1, k_p, v_p = _l1_prompt(h1.reshape(bsz, seq, dm), row(pre_norm_g[1]), row(kv_norm_g), w_kv_b, w_in_b0,
                               sinks[0], tables_p)
    y_p = _epilogue(og1.reshape(bsz * seq, -1), h1, pp, 1,
                    w_out_b0, row(post_norm_g[1]), w_pe_b[1], w_pg_b[1], tm=EPILOGUE_TILE)

    xs = x_sample.reshape(n_s, dm)
    u0 = _norm_matmul(xs, row(pre_norm_g[0]), w_in_a0)
    st_s, og0s = _l0_sample(u0, lb_logits, row(onorm_a[0]), state_hgrn[0])
    h1s = _epilogue(og0s, xs, ps, 0,
                    w_out_a0, row(post_norm_g[0]), w_pe_b[0], w_pg_b[0], tm=n_s)
    tables_s = _rope_tables(PAST_LEN + jnp.arange(1), n_q)
    q_s, z_s, k_s, v_s = _l1_sample_front(h1s, row(pre_norm_g[1]), row(kv_norm_g), w_in_b0, w_kv_b, tables_s)
    og1s = _l1_sample_attn(q_s.reshape(n_s, n_q, HEAD_DIM), z_s.reshape(n_s, n_q, HEAD_DIM), k_s, v_s,
                           cache_k.transpose(0, 2, 3, 1), cache_v.transpose(0, 2, 3, 1),
                           sinks[0].reshape(n_q, 1))
    y_s = _epilogue(og1s.reshape(n_s, -1), h1s, ps, 1,
                    w_out_b0, row(post_norm_g[1]), w_pe_b[1], w_pg_b[1], tm=n_s)

    return (y_p.reshape(bsz, seq, dm), y_s.reshape(n_s, 1, dm),
            st_p[None], st_s[None],
            k_p.reshape(bsz, WINDOW, kv_heads, HEAD_DIM), v_p.reshape(bsz, WINDOW, kv_heads, HEAD_DIM),
            k_s.reshape(n_s, 1, kv_heads, HEAD_DIM), v_s.reshape(n_s, 1, kv_heads, HEAD_DIM))
```

```python
import jax
import jax.numpy as jnp
from jax import lax
from jax.experimental import pallas as pl
from jax.experimental.pallas import tpu as pltpu

F32 = jnp.float32
BF16 = jnp.bfloat16

EPS = 1e-6
HGRN_HEAD_DIM = 128
HEAD_DIM = 64
Q_PER_KV = 8
WINDOW = 128
ROT_DIM = 16
ROPE_THETA = 500000.0
PAST_LEN = 16384
ATTN_SCALE = HEAD_DIM ** -0.5
LOG2_E = 1.4426950408889634

CHUNK = 128
LEVEL_HALVES = (64, 32, 16, 8, 4, 2, 1)
BF16_ROWS = 16
FAST_TAIL = 32
FAST_LOG2_BUDGET = 127.0 - 7.0 - 1.0
PROJ_COLS = 512
L0_TILE = 512
L1_TILE = 1024
EPILOGUE_TILE = 1024
SAMPLE_TILE = 8
MASKED = -1e30
V7X_VMEM_BYTES = 64 * 1024 * 1024
VMEM_LIMIT_BYTES = V7X_VMEM_BYTES - 4 * 1024 * 1024


def _sigmoid(x):
    return 1.0 / (1.0 + jnp.exp(-x))


def _silu(x):
    return x * _sigmoid(x)


def _rms(x):
    return x * lax.rsqrt(jnp.mean(x * x, axis=-1, keepdims=True) + EPS)


def _dot(a, b):
    return jnp.dot(a, b, preferred_element_type=F32)


def _dot_nt(a, b):
    return lax.dot_general(a, b, (((1,), (1,)), ((), ())), preferred_element_type=F32)


def _dot_tn(a, b):
    return lax.dot_general(a, b, (((0,), (0,)), ((), ())), preferred_element_type=F32)


def _block_diag2(a):
    c = a.shape[1] // 2
    lane = lax.broadcasted_iota(jnp.int32, a.shape, 1)
    zero = jnp.zeros_like(a)
    return jnp.concatenate([jnp.where(lane < c, a, zero), jnp.where(lane >= c, a, zero)], axis=0)


def _rope(x, cos, sin_lo, sin_hi):
    n = x.shape[-1]
    half = ROT_DIM // 2
    return x * cos + pltpu.roll(x, n - half, 1) * sin_lo + pltpu.roll(x, half, 1) * sin_hi


def _rope_tables(pos, reps):
    half = ROT_DIM // 2
    inv = ROPE_THETA ** (-jnp.arange(0, ROT_DIM, 2, dtype=F32) / ROT_DIM)
    ang = pos.astype(F32)[:, None] * inv[None, :]
    cos, sin = jnp.cos(ang), jnp.sin(ang)
    n = pos.shape[0]
    pad = jnp.zeros((n, HEAD_DIM - ROT_DIM), F32)
    zero = jnp.zeros((n, half), F32)
    c = jnp.concatenate([cos, cos, pad + 1.0], axis=1)
    s_lo = jnp.concatenate([-sin, zero, pad], axis=1)
    s_hi = jnp.concatenate([zero, sin, pad], axis=1)
    return tuple(jnp.tile(t, (1, reps)) for t in (c, s_lo, s_hi))


def _lower_bound(lbl):
    m = jnp.max(lbl, axis=0, keepdims=True)
    e = jnp.exp(lbl - m)
    return e[0:1] / jnp.sum(e, axis=0, keepdims=True)


def _level_ref(b, half, row):
    if half >= 8:
        return _block_rows(b, half - 1, 2 * half)
    n = b.shape[0]
    up = lambda x, d: pltpu.roll(x, n - d, 0)
    down = lambda x, d: pltpu.roll(x, d, 0)
    if half == 1:
        return jnp.where(row % 2 == 1, down(b, 1), b)
    y1 = jnp.where(row % 2 == 0, up(b, 1), b)
    if half == 2:
        return jnp.where(row % 4 >= 2, down(y1, 2), y1)
    y2 = jnp.where(row % 4 < 2, up(y1, 2), y1)
    return jnp.where(row % 8 >= 4, down(y2, 4), y2)


def _block_rows(b, offset, size):
    pieces = [jnp.broadcast_to(b[s + offset:s + offset + 1, :], (size, b.shape[1]))
              for s in range(0, b.shape[0], size)]
    return pieces[0] if len(pieces) == 1 else jnp.concatenate(pieces, axis=0)


def _block_diag2_aligned(a):
    c = a.shape[1] // 2
    zero = jnp.zeros((a.shape[0], c), a.dtype)
    return jnp.concatenate([jnp.concatenate([a[:, :c], zero], axis=1),
                            jnp.concatenate([zero, a[:, c:]], axis=1)], axis=0)


def _score_masks(c, w, halves, tail):
    d = w // 2
    row = lax.broadcasted_iota(jnp.int32, (c, w), 0)
    col_tok = lax.broadcasted_iota(jnp.int32, (c, w), 1) % d
    one = lambda cond: jnp.where(cond, 1.0, 0.0).astype(BF16)
    masks = {"diag": one(row == col_tok)}
    for half in halves:
        masks[half] = one(row // (2 * half) == col_tok // (2 * half))
    if tail is not None:
        masks["tail"] = one((row // tail == col_tok // tail) & (row > col_tok))
    return masks


def _chunk_scores(q, k, q_bf, k_bf, b, halves, tail, masks):
    c, w = q.shape
    d = w // 2
    qk = q * k
    diag = jnp.concatenate(
        [jnp.broadcast_to(jnp.sum(qk[:, :d], axis=-1, keepdims=True), (c, d)),
         jnp.broadcast_to(jnp.sum(qk[:, d:], axis=-1, keepdims=True), (c, d))], axis=1)
    a = diag.astype(BF16) * masks["diag"]
    for half in halves:
        if half >= BF16_ROWS:
            delta = b - _block_rows(b, half - 1, 2 * half)
            zero = jnp.zeros((half, w), BF16)
            qe, ke = [], []
            for s in range(0, c, 2 * half):
                lo, up = slice(s, s + half), slice(s + half, s + 2 * half)
                ke += [k_bf[lo] * jnp.exp2(-delta[lo]).astype(BF16), zero]
                qe += [zero, q_bf[up] * jnp.exp2(delta[up]).astype(BF16)]
            qe, ke = jnp.concatenate(qe, axis=0), jnp.concatenate(ke, axis=0)
        else:
            row = lax.broadcasted_iota(jnp.int32, (c, w), 0)
            upper = row % (2 * half) >= half
            ref = _level_ref(b, half, row)
            e = jnp.exp2(jnp.where(upper, b - ref, ref - b))
            qe = jnp.where(upper, q * e, 0.0).astype(BF16)
            ke = jnp.where(upper, 0.0, k * e).astype(BF16)
        p = _dot_nt(qe, _block_diag2_aligned(ke)).astype(BF16)
        a = a + (p if 2 * half == c else p * masks[half])
    if tail is not None:
        delta = b - _block_rows(b, tail // 2 - 1, tail)
        qe = q_bf * jnp.exp2(delta).astype(BF16)
        ke = k_bf * jnp.exp2(-delta).astype(BF16)
        a = a + _dot_nt(qe, _block_diag2_aligned(ke)).astype(BF16) * masks["tail"]
    return a


def _chunk_apply(q_bf, k_bf, b, v, a, st_a, st_b):
    c, w = b.shape
    d = w // 2
    st_bd = _block_diag2_aligned(jnp.concatenate([st_a, st_b], axis=1).astype(BF16))
    o = _dot_nt(q_bf * jnp.exp2(b).astype(BF16), st_bd) + _dot(a, _block_diag2_aligned(v))
    b_last = b[c - 1:c, :]
    k_hat = k_bf * jnp.exp2(b_last - b).astype(BF16)
    ds = _dot_tn(v, k_hat)
    decay = jnp.exp2(b_last)
    st_a = st_a * decay[:, :d] + ds[:d, :d]
    st_b = st_b * decay[:, d:] + ds[d:, d:]
    return o, st_a, st_b


def _l0_prompt_kernel(budget_ref, x_ref, pg_ref, w_ref, lbl_ref, on_ref, og_ref, st_ref,
                      s_scr, q_scr, k_scr, b_scr, v_scr, z_scr):
    l = pl.program_id(1)
    tm = x_ref.shape[1]
    width = q_scr.shape[1]
    d = HGRN_HEAD_DIM
    n_heads = width // d

    @pl.when(l == 0)
    def _():
        s_scr[...] = jnp.zeros_like(s_scr)

    xn = (_rms(x_ref[0]) * pg_ref[...]).astype(BF16)
    ti = lax.broadcasted_iota(jnp.int32, (CHUNK, CHUNK), 0)
    tj = lax.broadcasted_iota(jnp.int32, (CHUNK, CHUNK), 1)
    tri = jnp.where(tj <= ti, 1.0, 0.0).astype(BF16)
    tri2 = jnp.concatenate([tri, tri], axis=1)
    worst = jnp.zeros((1, PROJ_COLS), F32)
    for j in range(width // PROJ_COLS):
        cols = slice(j * PROJ_COLS, (j + 1) * PROJ_COLS)
        part = lambda i: w_ref[:, i * width + j * PROJ_COLS:i * width + (j + 1) * PROJ_COLS]
        lb = _lower_bound(lbl_ref[:, cols])
        q_scr[:, cols] = _silu(_dot(xn, part(0)))
        f = lb + (1.0 - lb) * _sigmoid(_dot(xn, part(1)))
        k_scr[:, cols] = 1.0 - f
        v_scr[:, cols] = _dot(xn, part(2)).astype(BF16)
        z_scr[:, cols] = _silu(_dot(xn, part(3))) * on_ref[:, cols]
        g = jnp.log2(f)
        g_hi = g.astype(BF16)
        r1 = g - g_hi.astype(F32)
        g_mid = r1.astype(BF16)
        g_lo = (r1 - g_mid.astype(F32)).astype(BF16)
        for c in range(tm // CHUNK):
            rows = slice(c * CHUNK, (c + 1) * CHUNK)
            b = _dot(tri2, jnp.concatenate([g_hi[rows], g_mid[rows]], axis=0)) + _dot(tri, g_lo[rows])
            b_scr[rows, cols] = b
            for s in range(0, CHUNK, FAST_TAIL):
                mid = b[s + FAST_TAIL // 2 - 1:s + FAST_TAIL // 2, :]
                worst = jnp.maximum(
                    worst, jnp.maximum(b[s:s + 1, :] - mid, mid - b[s + FAST_TAIL - 1:s + FAST_TAIL, :]))
    fast = 2.0 * jnp.max(worst) <= budget_ref[0]

    def scan(halves, tail, group):
        masks = _score_masks(CHUNK, 2 * d, halves, tail)
        chunks = [slice(c * CHUNK, (c + 1) * CHUNK) for c in range(tm // CHUNK)]

        def group_body(gi, carry):
            pairs = [gi * group + p for p in range(group)]
            cols = [pl.multiple_of(pi * 2 * d, 2 * d) for pi in pairs]
            load = lambda ref, rows, col: ref[rows, pl.ds(col, 2 * d)]
            states = [(s_scr[2 * pi], s_scr[2 * pi + 1]) for pi in pairs]
            operands = {}
            for n, rows in enumerate(chunks):
                for p, col in enumerate(cols):
                    q, k, b = load(q_scr, rows, col), load(k_scr, rows, col), load(b_scr, rows, col)
                    q_bf, k_bf = q.astype(BF16), k.astype(BF16)
                    operands[p, n] = (q_bf, k_bf, b, _chunk_scores(q, k, q_bf, k_bf, b, halves, tail, masks))
            for n, rows in enumerate(chunks):
                for p, col in enumerate(cols):
                    q_bf, k_bf, b, a = operands[p, n]
                    o, st_a, st_b = _chunk_apply(q_bf, k_bf, b, load(v_scr, rows, col), a, *states[p])
                    states[p] = (st_a, st_b)
                    on = jnp.concatenate([_rms(o[:, :d]), _rms(o[:, d:])], axis=1)
                    og_ref[0, rows, pl.ds(col, 2 * d)] = (on * load(z_scr, rows, col)).astype(og_ref.dtype)
            for pi, (st_a, st_b) in zip(pairs, states):
                s_scr[2 * pi] = st_a
                s_scr[2 * pi + 1] = st_b
            return carry

        lax.fori_loop(0, n_heads // 2 // group, group_body, 0)

    @pl.when(fast)
    def _():
        scan(tuple(h for h in LEVEL_HALVES if h >= FAST_TAIL), FAST_TAIL, group=4)

    @pl.when(jnp.logical_not(fast))
    def _():
        scan(LEVEL_HALVES, None, group=1)

    @pl.when(l == pl.num_programs(1) - 1)
    def _():
        for h in range(n_heads):
            st_ref[0, h] = s_scr[h].T


def _l0_prompt(x, pre_g, w_in, lb_logits, onorm, tm=L0_TILE):
    q_cols = w_in[:, :w_in.shape[1] // 4].astype(F32)
    q_bound = (x.shape[-1] ** 0.5) * jnp.max(jnp.abs(pre_g)) * jnp.sqrt(jnp.max(jnp.sum(q_cols * q_cols, axis=0)))
    budget = (FAST_LOG2_BUDGET - 1.0 - jnp.log2(q_bound)).astype(F32).reshape(1)
    bsz, seq, dm = x.shape
    width = w_in.shape[1] // 4
    n_heads = width // HGRN_HEAD_DIM
    d = HGRN_HEAD_DIM
    const = lambda *shape: pl.BlockSpec(shape, lambda b, l: (0,) * len(shape))
    return pl.pallas_call(
        _l0_prompt_kernel,
        grid=(bsz, seq // tm),
        in_specs=[
            pl.BlockSpec(memory_space=pltpu.SMEM),
            pl.BlockSpec((1, tm, dm), lambda b, l: (b, l, 0)),
            const(1, dm),
            pl.BlockSpec((dm, 4 * width), lambda b, l: (0, 0), pipeline_mode=pl.Buffered(1)),
            const(*lb_logits.shape),
            const(1, width),
        ],
        out_specs=[
            pl.BlockSpec((1, tm, width), lambda b, l: (b, l, 0)),
            pl.BlockSpec((1, n_heads, d, d), lambda b, l: (b, 0, 0, 0)),
        ],
        out_shape=[
            jax.ShapeDtypeStruct((bsz, seq, width), BF16),
            jax.ShapeDtypeStruct((bsz, n_heads, d, d), F32),
        ],
        scratch_shapes=[
            pltpu.VMEM((n_heads, d, d), F32),
            pltpu.VMEM((tm, width), F32),
            pltpu.VMEM((tm, width), F32),
            pltpu.VMEM((tm, width), F32),
            pltpu.VMEM((tm, width), BF16),
            pltpu.VMEM((tm, width), F32),
        ],
        compiler_params=pltpu.CompilerParams(
            dimension_semantics=("arbitrary", "arbitrary"), vmem_limit_bytes=VMEM_LIMIT_BYTES),
        name="l0_prompt",
    )(budget, x, pre_g, w_in, lb_logits, onorm)


def _epilogue_kernel(og_ref, h_ref, p_ref, wo_ref, pg_ref, wpe_ref, wpg_ref, out_ref, wo_scr, wpe_scr, wpg_scr):
    @pl.when(pl.program_id(0) == 0)
    def _():
        wo_scr[...] = wo_ref[...].astype(BF16)
        wpe_scr[...] = wpe_ref[...].astype(BF16)
        wpg_scr[...] = wpg_ref[...].astype(BF16)

    mix = _dot(og_ref[...].astype(BF16), wo_scr[...])
    hn = h_ref[...] + _rms(mix) * pg_ref[...]
    pe = _dot(p_ref[...].astype(BF16), wpe_scr[...])
    gate = _sigmoid(_dot(hn.astype(BF16), wpg_scr[...]))
    out_ref[...] = hn + pe * gate


def _epilogue(og, h, p_all, layer, w_out, post_g, w_pe, w_pg, tm):
    t, dm = h.shape
    width = og.shape[1]
    ple = p_all.shape[2]
    tm = min(tm, t)
    weight = lambda slab, *shape: pl.BlockSpec((None,) + shape, lambda i: (slab, 0, 0), pipeline_mode=pl.Buffered(1))
    return pl.pallas_call(
        _epilogue_kernel,
        grid=(t // tm,),
        in_specs=[
            pl.BlockSpec((tm, width), lambda i: (i, 0)),
            pl.BlockSpec((tm, dm), lambda i: (i, 0)),
            pl.BlockSpec((None, tm, ple), lambda i: (layer, i, 0)),
            weight(0, width, dm), pl.BlockSpec((1, dm), lambda i: (0, 0)), weight(layer, ple, dm), weight(layer, dm, dm),
        ],
        out_specs=pl.BlockSpec((tm, dm), lambda i: (i, 0)),
        out_shape=jax.ShapeDtypeStruct((t, dm), F32),
        scratch_shapes=[pltpu.VMEM((width, dm), BF16), pltpu.VMEM((ple, dm), BF16), pltpu.VMEM((dm, dm), BF16)],
        compiler_params=pltpu.CompilerParams(
            dimension_semantics=("arbitrary",), vmem_limit_bytes=VMEM_LIMIT_BYTES),
        name="epilogue",
    )(og, h, p_all, w_out, post_g, w_pe, w_pg)


def _dup_heads(x):
    pair_w = 2 * HEAD_DIM
    lane = lax.broadcasted_iota(jnp.int32, (x.shape[0], pair_w), 1)
    out = []
    for c in range(x.shape[1] // pair_w):
        both = x[:, c * pair_w:(c + 1) * pair_w]
        swapped = pltpu.roll(both, HEAD_DIM, 1)
        out += [jnp.where(lane < HEAD_DIM, both, swapped), jnp.where(lane < HEAD_DIM, swapped, both)]
    return jnp.concatenate(out, axis=1)


def _l1_prompt_kernel(sinks_ref, h_ref, g1_ref, kvg_ref, wkv_ref, win_ref, cos_ref, slo_ref, shi_ref,
                      og_ref, kc_ref, vc_ref, kx_scr, vx_scr, q_scr, z_scr):
    l = pl.program_id(1)
    tm = h_ref.shape[1]
    width = z_scr.shape[1]
    kvw = kx_scr.shape[1]
    kv_cols = kc_ref.shape[2]
    pair_w = 2 * HEAD_DIM
    n_pairs = width // pair_w
    pairs_per_kv = Q_PER_KV // 2

    @pl.when(l == 0)
    def _():
        kx_scr[0:WINDOW, :] = jnp.zeros((WINDOW, kvw), kx_scr.dtype)
        vx_scr[0:WINDOW, :] = jnp.zeros((WINDOW, kvw), vx_scr.dtype)

    @pl.when(l > 0)
    def _():
        kx_scr[0:WINDOW, :] = kx_scr[tm:tm + WINDOW, :]
        vx_scr[0:WINDOW, :] = vx_scr[tm:tm + WINDOW, :]

    h0 = _rms(h_ref[0])
    xn = (h0 * g1_ref[...]).astype(BF16)
    hk = (h0 * kvg_ref[...]).astype(BF16)
    cos, slo, shi = cos_ref[...], slo_ref[...], shi_ref[...]
    tile = lambda t, n: jnp.concatenate([t] * n, axis=1)

    q = _rope(_dot(xn, win_ref[:, 0:width]), tile(cos, n_pairs), tile(slo, n_pairs), tile(shi, n_pairs))
    q_scr[...] = (q * (ATTN_SCALE * LOG2_E)).astype(q_scr.dtype)
    z_scr[...] = _silu(_dot(xn, win_ref[:, width:2 * width]))

    kv = _dot(hk, wkv_ref[...])
    reps = kv_cols // pair_w
    k_rot = _rope(kv[:, :kv_cols], tile(cos, reps), tile(slo, reps), tile(shi, reps))
    v_new = kv[:, kv_cols:]
    kx_scr[WINDOW:WINDOW + tm, :] = _dup_heads(k_rot).astype(kx_scr.dtype)
    vx_scr[WINDOW:WINDOW + tm, :] = _dup_heads(v_new).astype(vx_scr.dtype)

    @pl.when(l == pl.num_programs(1) - 1)
    def _():
        kc_ref[0] = k_rot[tm - WINDOW:, :]
        vc_ref[0] = v_new[tm - WINDOW:, :]

    qi = lax.broadcasted_iota(jnp.int32, (WINDOW, 2 * WINDOW), 0)
    ki = lax.broadcasted_iota(jnp.int32, (WINDOW, 2 * WINDOW), 1)
    band = (ki > qi) & (ki <= qi + WINDOW)
    first_head = lax.broadcasted_iota(jnp.int32, (WINDOW, pair_w), 1) < HEAD_DIM

    def group_body(g, carry):
        kcol = pl.multiple_of(g * pair_w, pair_w)
        cols = [pl.multiple_of((g * pairs_per_kv + p) * pair_w, pair_w) for p in range(pairs_per_kv)]
        for c in range(tm // WINDOW):
            rows = slice(c * WINDOW, (c + 1) * WINDOW)
            keys = slice(c * WINDOW, (c + 2) * WINDOW)
            k_bd = _block_diag2(kx_scr[keys, pl.ds(kcol, pair_w)])
            v_bd = _block_diag2(vx_scr[keys, pl.ds(kcol, pair_w)])
            q_all = jnp.concatenate([q_scr[rows, pl.ds(col, pair_w)] for col in cols], axis=0)
            s_all = _dot_nt(q_all, k_bd)
            mask = band & (ki >= jnp.where(l == 0, WINDOW, 0)) if c == 0 else band
            probs, inv = [], []
            for p in range(pairs_per_kv):
                pair_probs, maxes, sums, pair_sinks = [], [], [], []
                for hh in range(2):
                    sink = sinks_ref[(g * pairs_per_kv + p) * 2 + hh] * LOG2_E
                    s = s_all[p * WINDOW:(p + 1) * WINDOW, hh * 2 * WINDOW:(hh + 1) * 2 * WINDOW]
                    s = jnp.where(mask, s, MASKED)
                    mx = jnp.maximum(jnp.max(s, axis=-1, keepdims=True), sink)
                    e = jnp.exp2(s - mx)
                    pair_probs.append(e.astype(BF16))
                    maxes.append(mx)
                    sums.append(jnp.sum(e, axis=-1, keepdims=True))
                    pair_sinks.append(sink)
                probs.append(jnp.concatenate(pair_probs, axis=1))
                mx2 = jnp.where(first_head, maxes[0], maxes[1])
                sum2 = jnp.where(first_head, sums[0], sums[1])
                sink2 = jnp.where(first_head, pair_sinks[0], pair_sinks[1])
                inv.append(1.0 / (sum2 + jnp.exp2(sink2 - mx2)))
            o_all = _dot(jnp.concatenate(probs, axis=0), v_bd)
            for p, col in enumerate(cols):
                o2 = o_all[p * WINDOW:(p + 1) * WINDOW] * inv[p]
                og_ref[0, rows, pl.ds(col, pair_w)] = (o2 * z_scr[rows, pl.ds(col, pair_w)]).astype(og_ref.dtype)
        return carry

    lax.fori_loop(0, n_pairs // pairs_per_kv, group_body, 0, unroll=2)


def _l1_prompt(h, g1, kvg, w_kv, w_in, sinks, tables, tm=L1_TILE):
    bsz, seq, dm = h.shape
    width = w_in.shape[1] // 2
    kv_cols = w_kv.shape[1] // 2
    kvw = 2 * kv_cols
    pair_w = 2 * HEAD_DIM
    const = lambda *shape: pl.BlockSpec(shape, lambda b, l: (0,) * len(shape))
    table = pl.BlockSpec((tm, pair_w), lambda b, l: (l, 0))
    weight = lambda *shape: pl.BlockSpec(shape, lambda b, l: (0, 0), pipeline_mode=pl.Buffered(1))
    return pl.pallas_call(
        _l1_prompt_kernel,
        grid=(bsz, seq // tm),
        in_specs=[
            pl.BlockSpec(memory_space=pltpu.SMEM),
            pl.BlockSpec((1, tm, dm), lambda b, l: (b, l, 0)),
            const(1, dm), const(1, dm), weight(dm, 2 * kv_cols), weight(dm, 2 * width),
            table, table, table,
        ],
        out_specs=[
            pl.BlockSpec((1, tm, width), lambda b, l: (b, l, 0)),
            pl.BlockSpec((1, WINDOW, kv_cols), lambda b, l: (b, 0, 0)),
            pl.BlockSpec((1, WINDOW, kv_cols), lambda b, l: (b, 0, 0)),
        ],
        out_shape=[
            jax.ShapeDtypeStruct((bsz, seq, width), BF16),
            jax.ShapeDtypeStruct((bsz, WINDOW, kv_cols), F32),
            jax.ShapeDtypeStruct((bsz, WINDOW, kv_cols), F32),
        ],
        scratch_shapes=[
            pltpu.VMEM((WINDOW + tm, kvw), BF16),
            pltpu.VMEM((WINDOW + tm, kvw), BF16),
            pltpu.VMEM((tm, width), BF16),
            pltpu.VMEM((tm, width), F32),
        ],
        compiler_params=pltpu.CompilerParams(
            dimension_semantics=("arbitrary", "arbitrary"), vmem_limit_bytes=VMEM_LIMIT_BYTES),
        name="l1_prompt",
    )(sinks, h, g1, kvg, w_kv, w_in, *tables)


def _norm_matmul_kernel(x_ref, g_ref, w_ref, o_ref):
    xn = (_rms(x_ref[...]) * g_ref[...]).astype(BF16)
    o_ref[...] = _dot(xn, w_ref[...])


def _norm_matmul(x, g, w, tn=2048):
    t, dm = x.shape
    n = w.shape[1]
    tn = min(tn, n)
    return pl.pallas_call(
        _norm_matmul_kernel,
        grid=(n // tn,),
        in_specs=[
            pl.BlockSpec((t, dm), lambda j: (0, 0)),
            pl.BlockSpec((1, dm), lambda j: (0, 0)),
            pl.BlockSpec((dm, tn), lambda j: (0, j)),
        ],
        out_specs=pl.BlockSpec((t, tn), lambda j: (0, j)),
        out_shape=jax.ShapeDtypeStruct((t, n), F32),
        compiler_params=pltpu.CompilerParams(
            dimension_semantics=("arbitrary",), vmem_limit_bytes=VMEM_LIMIT_BYTES),
        name="norm_matmul",
    )(x, g, w)


def _l0_sample_kernel(u_ref, lbl_ref, on_ref, s0_ref, s1_ref, og_ref):
    bt = u_ref.shape[0]
    width = og_ref.shape[1]
    d = HGRN_HEAD_DIM
    lb_all = _lower_bound(lbl_ref[...])
    terms = 4
    src = lax.broadcasted_iota(jnp.int32, (terms * bt, bt * d), 0) % bt
    dst = lax.broadcasted_iota(jnp.int32, (terms * bt, bt * d), 1) // d
    spread = jnp.where(src == dst, 1.0, 0.0).astype(BF16)

    for h in range(width // d):
        cols = slice(h * d, (h + 1) * d)
        lb = lb_all[:, cols]
        q = _silu(u_ref[:, h * d:(h + 1) * d])
        f = lb + (1.0 - lb) * _sigmoid(u_ref[:, width + h * d:width + (h + 1) * d])
        v = u_ref[:, 2 * width + h * d:2 * width + (h + 1) * d]
        z = _silu(u_ref[:, 3 * width + h * d:3 * width + (h + 1) * d])
        f_hi = f.astype(BF16).astype(F32)
        f_mid = (f - f_hi).astype(BF16).astype(F32)
        f_lo = (f - f_hi - f_mid).astype(BF16).astype(F32)
        f_terms = jnp.concatenate([f_hi, f_mid, f_lo, jnp.zeros_like(f)], axis=0)
        f_wide = _dot(f_terms.T.astype(BF16), spread)
        q_b = q.astype(BF16)
        outs = []
        for j in range(bt):
            v_row = v[j:j + 1, :]
            s_new = v_row + f_wide[:, j * d:(j + 1) * d] * (s0_ref[j, h] - v_row)
            s1_ref[j, h] = s_new
            outs.append(_dot(q_b[j:j + 1, :], s_new.astype(BF16)))
        o = jnp.concatenate(outs, axis=0)
        og_ref[:, cols] = _rms(o) * on_ref[:, cols] * z


def _l0_sample(u, lb_logits, onorm, state, bt=SAMPLE_TILE):
    n, heads, d, _ = state.shape
    width = heads * d
    return pl.pallas_call(
        _l0_sample_kernel,
        grid=(n // bt,),
        in_specs=[
            pl.BlockSpec((bt, 4 * width), lambda i: (i, 0)),
            pl.BlockSpec(lb_logits.shape, lambda i: (0, 0)),
            pl.BlockSpec((1, width), lambda i: (0, 0)),
            pl.BlockSpec((bt, heads, d, d), lambda i: (i, 0, 0, 0)),
        ],
        out_specs=[
            pl.BlockSpec((bt, heads, d, d), lambda i: (i, 0, 0, 0)),
            pl.BlockSpec((bt, width), lambda i: (i, 0)),
        ],
        out_shape=[
            jax.ShapeDtypeStruct(state.shape, F32),
            jax.ShapeDtypeStruct((n, width), F32),
        ],
        compiler_params=pltpu.CompilerParams(
            dimension_semantics=("arbitrary",), vmem_limit_bytes=VMEM_LIMIT_BYTES),
        name="l0_sample",
    )(u, lb_logits, onorm, state)


def _l1_sample_front_kernel(h_ref, g1_ref, kvg_ref, win_ref, wkv_ref, cos_ref, slo_ref, shi_ref,
                            q_ref, z_ref, k_ref, v_ref):
    width = q_ref.shape[1]
    kvw = k_ref.shape[1]
    h0 = _rms(h_ref[...])
    xn = (h0 * g1_ref[...]).astype(BF16)
    hk = (h0 * kvg_ref[...]).astype(BF16)
    cos, slo, shi = cos_ref[...], slo_ref[...], shi_ref[...]
    q = _rope(_dot(xn, win_ref[:, 0:width]), cos, slo, shi)
    q_ref[...] = q * ATTN_SCALE
    z_ref[...] = _silu(_dot(xn, win_ref[:, width:2 * width]))
    kv = _dot(hk, wkv_ref[...])
    k_ref[...] = _rope(kv[:, :kvw], cos[:, :kvw], slo[:, :kvw], shi[:, :kvw])
    v_ref[...] = kv[:, kvw:]


def _l1_sample_front(h, g1, kvg, w_in, w_kv, tables):
    t, dm = h.shape
    width = w_in.shape[1] // 2
    kvw = w_kv.shape[1] // 2
    full = lambda a: pl.BlockSpec(a.shape, lambda i: (0,) * a.ndim)
    args = (h, g1, kvg, w_in, w_kv, *tables)
    return pl.pallas_call(
        _l1_sample_front_kernel,
        grid=(1,),
        in_specs=[full(a) for a in args],
        out_specs=[pl.BlockSpec((t, width), lambda i: (0, 0)), pl.BlockSpec((t, width), lambda i: (0, 0)),
                   pl.BlockSpec((t, kvw), lambda i: (0, 0)), pl.BlockSpec((t, kvw), lambda i: (0, 0))],
        out_shape=[jax.ShapeDtypeStruct((t, width), F32), jax.ShapeDtypeStruct((t, width), F32),
                   jax.ShapeDtypeStruct((t, kvw), F32), jax.ShapeDtypeStruct((t, kvw), F32)],
        compiler_params=pltpu.CompilerParams(
            dimension_semantics=("arbitrary",), vmem_limit_bytes=VMEM_LIMIT_BYTES),
        name="l1_sample_front",
    )(*args)


def _l1_sample_attn_kernel(q_ref, z_ref, kn_ref, vn_ref, ckt_ref, cvt_ref, sink_ref, og_ref):
    bt, kv_heads, hd, wb = ckt_ref.shape
    n_q = q_ref.shape[1]
    kvw = kv_heads * hd
    valid = lax.broadcasted_iota(jnp.int32, (n_q, wb), 1) > wb - WINDOW
    head_kv = lax.broadcasted_iota(jnp.int32, (n_q, kvw), 0) // Q_PER_KV
    own_lanes = lax.broadcasted_iota(jnp.int32, (n_q, kvw), 1) // hd == head_kv
    row_kv = lax.broadcasted_iota(jnp.int32, (n_q, hd), 0) // Q_PER_KV
    sink = sink_ref[...]
    scored = []
    for j in range(bt):
        q = q_ref[j]
        q_bd = jnp.where(own_lanes, jnp.concatenate([q] * kv_heads, axis=1), 0.0)
        scored.append((q_bd, _dot(q_bd.astype(BF16), ckt_ref[j].reshape(kvw, wb).astype(BF16))))
    weighted = []
    for j, (q_bd, s) in enumerate(scored):
        s = jnp.where(valid, s, MASKED)
        s_new = jnp.sum(q_bd * kn_ref[j:j + 1, :], axis=-1, keepdims=True)
        mx = jnp.maximum(jnp.maximum(jnp.max(s, axis=-1, keepdims=True), s_new), sink)
        e = jnp.exp(s - mx)
        e_new = jnp.exp(s_new - mx)
        weighted.append((e.astype(BF16), e_new, jnp.sum(e, axis=-1, keepdims=True) + e_new + jnp.exp(sink - mx)))
    for j, (e, e_new, den) in enumerate(weighted):
        o_all = _dot_nt(e, cvt_ref[j].reshape(kvw, wb).astype(BF16)) + e_new * vn_ref[j:j + 1, :]
        o = jnp.zeros((n_q, hd), F32)
        for g in range(kv_heads):
            o = o + jnp.where(row_kv == g, o_all[:, g * hd:(g + 1) * hd], 0.0)
        og_ref[j] = o / den * z_ref[j]


def _l1_sample_attn(q3, z3, k_new, v_new, cache_kt, cache_vt, sinks_col, bt=SAMPLE_TILE):
    n, heads, hd = q3.shape
    kv_heads, wb = cache_kt.shape[1], cache_kt.shape[3]
    qspec = pl.BlockSpec((bt, heads, hd), lambda i: (i, 0, 0))
    nspec = pl.BlockSpec((bt, kv_heads * hd), lambda i: (i, 0))
    cspec = pl.BlockSpec((bt, kv_heads, hd, wb), lambda i: (i, 0, 0, 0))
    return pl.pallas_call(
        _l1_sample_attn_kernel,
        grid=(n // bt,),
        in_specs=[qspec, qspec, nspec, nspec, cspec, cspec, pl.BlockSpec((heads, 1), lambda i: (0, 0))],
        out_specs=qspec,
        out_shape=jax.ShapeDtypeStruct((n, heads, hd), F32),
        compiler_params=pltpu.CompilerParams(
            dimension_semantics=("arbitrary",), vmem_limit_bytes=VMEM_LIMIT_BYTES),
        name="l1_sample_attn",
    )(q3, z3, k_new, v_new, cache_kt, cache_vt, sinks_col)


def kernel(x_prompt, x_sample, p_prompt, p_sample, state_hgrn, cache_k, cache_v, pre_norm_g, post_norm_g,
           w_in_a, lb_logits, onorm_a, w_out_a, kv_norm_g, w_kv, w_in_b, sinks, w_out_b, w_pe, w_pg):
    bsz, seq, dm = x_prompt.shape
    n_s = x_sample.shape[0]
    assert x_sample.shape[1] == 1 and w_in_a.shape[0] == 1 and w_in_b.shape[0] == 1
    assert seq % L0_TILE == 0 and seq % L1_TILE == 0 and n_s % SAMPLE_TILE == 0
    kv_heads = w_kv.shape[1] // (2 * HEAD_DIM)
    n_q = w_in_b.shape[2] // (2 * HEAD_DIM)
    assert n_q == kv_heads * Q_PER_KV

    row = lambda a: a.reshape(1, -1)
    bf = lambda a: a.astype(BF16)
    w_in_a0, w_in_b0 = bf(w_in_a[0]), bf(w_in_b[0])
    w_kv_b = bf(w_kv)

    pp = p_prompt.reshape(p_prompt.shape[0], bsz * seq, -1)
    ps = p_sample.reshape(p_sample.shape[0], n_s, -1)

    og0, st_p = _l0_prompt(x_prompt, row(pre_norm_g[0]), w_in_a0, lb_logits, row(onorm_a[0]))
    h1 = _epilogue(og0.reshape(bsz * seq, -1), x_prompt.reshape(bsz * seq, dm), pp, 0,
                   w_out_a, row(post_norm_g[0]), w_pe, w_pg, tm=EPILOGUE_TILE)
    tables_p = _rope_tables(jnp.arange(seq), 2)
    og1, k_p, v_p = _l1_prompt(h1.reshape(bsz, seq, dm), row(pre_norm_g[1]), row(kv_norm_g), w_kv_b, w_in_b0,
                               sinks[0], tables_p)
    y_p = _epilogue(og1.reshape(bsz * seq, -1), h1, pp, 1,
                    w_out_b, row(post_norm_g[1]), w_pe, w_pg, tm=EPILOGUE_TILE)

    xs = x_sample.reshape(n_s, dm)
    u0 = _norm_matmul(xs, row(pre_norm_g[0]), w_in_a0)
    st_s, og0s = _l0_sample(u0, lb_logits, row(onorm_a[0]), state_hgrn[0])
    h1s = _epilogue(og0s, xs, ps, 0,
                    w_out_a, row(post_norm_g[0]), w_pe, w_pg, tm=n_s)
    tables_s = _rope_tables(PAST_LEN + jnp.arange(1), n_q)
    q_s, z_s, k_s, v_s = _l1_sample_front(h1s, row(pre_norm_g[1]), row(kv_norm_g), w_in_b0, w_kv_b, tables_s)
    og1s = _l1_sample_attn(q_s.reshape(n_s, n_q, HEAD_DIM), z_s.reshape(n_s, n_q, HEAD_DIM), k_s, v_s,
                           cache_k.transpose(0, 2, 3, 1), cache_v.transpose(0, 2, 3, 1),
                           sinks[0].reshape(n_q, 1))
    y_s = _epilogue(og1s.reshape(n_s, -1), h1s, ps, 1,
                    w_out_b, row(post_norm_g[1]), w_pe, w_pg, tm=n_s)

    return (y_p.reshape(bsz, seq, dm), y_s.reshape(n_s, 1, dm),
            st_p[None], st_s[None],
            k_p.reshape(bsz, WINDOW, kv_heads, HEAD_DIM), v_p.reshape(bsz, WINDOW, kv_heads, HEAD_DIM),
            k_s.reshape(n_s, 1, kv_heads, HEAD_DIM), v_s.reshape(n_s, 1, kv_heads, HEAD_DIM))
```

```python
import jax
import jax.numpy as jnp
from jax import lax
from jax.experimental import pallas as pl
from jax.experimental.pallas import tpu as pltpu

F32 = jnp.float32
BF16 = jnp.bfloat16

EPS = 1e-6
HGRN_HEAD_DIM = 128
HEAD_DIM = 64
Q_PER_KV = 8
WINDOW = 128
ROT_DIM = 16
ROPE_THETA = 500000.0
PAST_LEN = 16384
ATTN_SCALE = HEAD_DIM ** -0.5
LOG2_E = 1.4426950408889634

CHUNK = 128
LEVEL_HALVES = (64, 32, 16, 8, 4, 2, 1)
BF16_ROWS = 16
FAST_TAIL = 32
FAST_LOG2_BUDGET = 127.0 - 7.0 - 1.0
PROJ_COLS = 512
L0_TILE = 512
L1_TILE = 1024
EPILOGUE_TILE = 1024
SAMPLE_TILE = 8
MASKED = -1e30
V7X_VMEM_BYTES = 64 * 1024 * 1024
VMEM_LIMIT_BYTES = V7X_VMEM_BYTES - 4 * 1024 * 1024


def _sigmoid(x):
    return 1.0 / (1.0 + jnp.exp(-x))


def _silu(x):
    return x * _sigmoid(x)


def _rms(x):
    return x * lax.rsqrt(jnp.mean(x * x, axis=-1, keepdims=True) + EPS)


def _dot(a, b):
    return jnp.dot(a, b, preferred_element_type=F32)


def _dot_nt(a, b):
    return lax.dot_general(a, b, (((1,), (1,)), ((), ())), preferred_element_type=F32)


def _dot_tn(a, b):
    return lax.dot_general(a, b, (((0,), (0,)), ((), ())), preferred_element_type=F32)


def _block_diag2(a):
    c = a.shape[1] // 2
    lane = lax.broadcasted_iota(jnp.int32, a.shape, 1)
    zero = jnp.zeros_like(a)
    return jnp.concatenate([jnp.where(lane < c, a, zero), jnp.where(lane >= c, a, zero)], axis=0)


def _rope(x, cos, sin_lo, sin_hi):
    n = x.shape[-1]
    half = ROT_DIM // 2
    return x * cos + pltpu.roll(x, n - half, 1) * sin_lo + pltpu.roll(x, half, 1) * sin_hi


def _rope_tables(pos, reps):
    half = ROT_DIM // 2
    inv = ROPE_THETA ** (-jnp.arange(0, ROT_DIM, 2, dtype=F32) / ROT_DIM)
    ang = pos.astype(F32)[:, None] * inv[None, :]
    cos, sin = jnp.cos(ang), jnp.sin(ang)
    n = pos.shape[0]
    pad = jnp.zeros((n, HEAD_DIM - ROT_DIM), F32)
    zero = jnp.zeros((n, half), F32)
    c = jnp.concatenate([cos, cos, pad + 1.0], axis=1)
    s_lo = jnp.concatenate([-sin, zero, pad], axis=1)
    s_hi = jnp.concatenate([zero, sin, pad], axis=1)
    return tuple(jnp.tile(t, (1, reps)) for t in (c, s_lo, s_hi))


def _lower_bound(lbl):
    m = jnp.max(lbl, axis=0, keepdims=True)
    e = jnp.exp(lbl - m)
    return e[0:1] / jnp.sum(e, axis=0, keepdims=True)


def _level_ref(b, half, row):
    if half >= 8:
        return _block_rows(b, half - 1, 2 * half)
    n = b.shape[0]
    up = lambda x, d: pltpu.roll(x, n - d, 0)
    down = lambda x, d: pltpu.roll(x, d, 0)
    if half == 1:
        return jnp.where(row % 2 == 1, down(b, 1), b)
    y1 = jnp.where(row % 2 == 0, up(b, 1), b)
    if half == 2:
        return jnp.where(row % 4 >= 2, down(y1, 2), y1)
    y2 = jnp.where(row % 4 < 2, up(y1, 2), y1)
    return jnp.where(row % 8 >= 4, down(y2, 4), y2)


def _block_rows(b, offset, size):
    pieces = [jnp.broadcast_to(b[s + offset:s + offset + 1, :], (size, b.shape[1]))
              for s in range(0, b.shape[0], size)]
    return pieces[0] if len(pieces) == 1 else jnp.concatenate(pieces, axis=0)


def _block_diag2_aligned(a):
    c = a.shape[1] // 2
    zero = jnp.zeros((a.shape[0], c), a.dtype)
    return jnp.concatenate([jnp.concatenate([a[:, :c], zero], axis=1),
                            jnp.concatenate([zero, a[:, c:]], axis=1)], axis=0)


def _score_masks(c, w, halves, tail):
    d = w // 2
    row = lax.broadcasted_iota(jnp.int32, (c, w), 0)
    col_tok = lax.broadcasted_iota(jnp.int32, (c, w), 1) % d
    one = lambda cond: jnp.where(cond, 1.0, 0.0).astype(BF16)
    masks = {"diag": one(row == col_tok)}
    for half in halves:
        masks[half] = one(row // (2 * half) == col_tok // (2 * half))
    if tail is not None:
        masks["tail"] = one((row // tail == col_tok // tail) & (row > col_tok))
    return masks


def _chunk_scores(q, k, q_bf, k_bf, b, halves, tail, masks):
    c, w = q.shape
    d = w // 2
    qk = q * k
    diag = jnp.concatenate(
        [jnp.broadcast_to(jnp.sum(qk[:, :d], axis=-1, keepdims=True), (c, d)),
         jnp.broadcast_to(jnp.sum(qk[:, d:], axis=-1, keepdims=True), (c, d))], axis=1)
    a = diag.astype(BF16) * masks["diag"]
    for half in halves:
        if half >= BF16_ROWS:
            delta = b - _block_rows(b, half - 1, 2 * half)
            zero = jnp.zeros((half, w), BF16)
            qe, ke = [], []
            for s in range(0, c, 2 * half):
                lo, up = slice(s, s + half), slice(s + half, s + 2 * half)
                ke += [k_bf[lo] * jnp.exp2(-delta[lo]).astype(BF16), zero]
                qe += [zero, q_bf[up] * jnp.exp2(delta[up]).astype(BF16)]
            qe, ke = jnp.concatenate(qe, axis=0), jnp.concatenate(ke, axis=0)
        else:
            row = lax.broadcasted_iota(jnp.int32, (c, w), 0)
            upper = row % (2 * half) >= half
            ref = _level_ref(b, half, row)
            e = jnp.exp2(jnp.where(upper, b - ref, ref - b))
            qe = jnp.where(upper, q * e, 0.0).astype(BF16)
            ke = jnp.where(upper, 0.0, k * e).astype(BF16)
        p = _dot_nt(qe, _block_diag2_aligned(ke)).astype(BF16)
        a = a + (p if 2 * half == c else p * masks[half])
    if tail is not None:
        delta = b - _block_rows(b, tail // 2 - 1, tail)
        qe = q_bf * jnp.exp2(delta).astype(BF16)
        ke = k_bf * jnp.exp2(-delta).astype(BF16)
        a = a + _dot_nt(qe, _block_diag2_aligned(ke)).astype(BF16) * masks["tail"]
    return a


def _chunk_apply(q_bf, k_bf, b, v, a, st_a, st_b):
    c, w = b.shape
    d = w // 2
    st_bd = _block_diag2_aligned(jnp.concatenate([st_a, st_b], axis=1).astype(BF16))
    o = _dot_nt(q_bf * jnp.exp2(b).astype(BF16), st_bd) + _dot(a, _block_diag2_aligned(v))
    b_last = b[c - 1:c, :]
    k_hat = k_bf * jnp.exp2(b_last - b).astype(BF16)
    ds = _dot_tn(v, k_hat)
    decay = jnp.exp2(b_last)
    st_a = st_a * decay[:, :d] + ds[:d, :d]
    st_b = st_b * decay[:, d:] + ds[d:, d:]
    return o, st_a, st_b


def _l0_prompt_kernel(budget_ref, x_ref, pg_ref, w_ref, lbl_ref, on_ref, og_ref, st_ref,
                      s_scr, q_scr, k_scr, b_scr, v_scr, z_scr):
    l = pl.program_id(1)
    tm = x_ref.shape[1]
    width = q_scr.shape[1]
    d = HGRN_HEAD_DIM
    n_heads = width // d

    @pl.when(l == 0)
    def _():
        s_scr[...] = jnp.zeros_like(s_scr)

    xn = (_rms(x_ref[0]) * pg_ref[...]).astype(BF16)
    ti = lax.broadcasted_iota(jnp.int32, (CHUNK, CHUNK), 0)
    tj = lax.broadcasted_iota(jnp.int32, (CHUNK, CHUNK), 1)
    tri = jnp.where(tj <= ti, 1.0, 0.0).astype(BF16)
    tri2 = jnp.concatenate([tri, tri], axis=1)
    worst = jnp.zeros((1, PROJ_COLS), F32)
    for j in range(width // PROJ_COLS):
        cols = slice(j * PROJ_COLS, (j + 1) * PROJ_COLS)
        part = lambda i: w_ref[:, i * width + j * PROJ_COLS:i * width + (j + 1) * PROJ_COLS]
        lb = _lower_bound(lbl_ref[:, cols])
        q_scr[:, cols] = _silu(_dot(xn, part(0)))
        f = lb + (1.0 - lb) * _sigmoid(_dot(xn, part(1)))
        k_scr[:, cols] = 1.0 - f
        v_scr[:, cols] = _dot(xn, part(2)).astype(BF16)
        z_scr[:, cols] = _silu(_dot(xn, part(3))) * on_ref[:, cols]
        g = jnp.log2(f)
        g_hi = g.astype(BF16)
        r1 = g - g_hi.astype(F32)
        g_mid = r1.astype(BF16)
        g_lo = (r1 - g_mid.astype(F32)).astype(BF16)
        for c in range(tm // CHUNK):
            rows = slice(c * CHUNK, (c + 1) * CHUNK)
            b = _dot(tri2, jnp.concatenate([g_hi[rows], g_mid[rows]], axis=0)) + _dot(tri, g_lo[rows])
            b_scr[rows, cols] = b
            for s in range(0, CHUNK, FAST_TAIL):
                mid = b[s + FAST_TAIL // 2 - 1:s + FAST_TAIL // 2, :]
                worst = jnp.maximum(
                    worst, jnp.maximum(b[s:s + 1, :] - mid, mid - b[s + FAST_TAIL - 1:s + FAST_TAIL, :]))
    fast = 2.0 * jnp.max(worst) <= budget_ref[0]

    def scan(halves, tail, group, static):
        masks = _score_masks(CHUNK, 2 * d, halves, tail)
        chunks = [slice(c * CHUNK, (c + 1) * CHUNK) for c in range(tm // CHUNK)]

        def group_body(gi, carry):
            pairs = [gi * group + p for p in range(group)]
            cols = [pi * 2 * d if isinstance(pi, int) else pl.multiple_of(pi * 2 * d, 2 * d) for pi in pairs]
            load = lambda ref, rows, col: ref[rows, pl.ds(col, 2 * d)]
            states = [(s_scr[2 * pi], s_scr[2 * pi + 1]) for pi in pairs]
            operands = {}
            for n, rows in enumerate(chunks):
                for p, col in enumerate(cols):
                    q, k, b = load(q_scr, rows, col), load(k_scr, rows, col), load(b_scr, rows, col)
                    q_bf, k_bf = q.astype(BF16), k.astype(BF16)
                    operands[p, n] = (q_bf, k_bf, b, _chunk_scores(q, k, q_bf, k_bf, b, halves, tail, masks))
            for n, rows in enumerate(chunks):
                for p, col in enumerate(cols):
                    q_bf, k_bf, b, a = operands[p, n]
                    o, st_a, st_b = _chunk_apply(q_bf, k_bf, b, load(v_scr, rows, col), a, *states[p])
                    states[p] = (st_a, st_b)
                    on = jnp.concatenate([_rms(o[:, :d]), _rms(o[:, d:])], axis=1)
                    og_ref[0, rows, pl.ds(col, 2 * d)] = (on * load(z_scr, rows, col)).astype(og_ref.dtype)
            for pi, (st_a, st_b) in zip(pairs, states):
                s_scr[2 * pi] = st_a
                s_scr[2 * pi + 1] = st_b
            return carry

        n_groups = n_heads // 2 // group
        if static:
            for gi in range(n_groups):
                group_body(gi, 0)
        else:
            lax.fori_loop(0, n_groups, group_body, 0)

    @pl.when(fast)
    def _():
        scan(tuple(h for h in LEVEL_HALVES if h >= FAST_TAIL), FAST_TAIL, group=4, static=True)

    @pl.when(jnp.logical_not(fast))
    def _():
        scan(LEVEL_HALVES, None, group=1, static=False)

    @pl.when(l == pl.num_programs(1) - 1)
    def _():
        for h in range(n_heads):
            st_ref[0, h] = s_scr[h].T


def _l0_prompt(x, pre_g, w_in, lb_logits, onorm, tm=L0_TILE):
    q_cols = w_in[:, :w_in.shape[1] // 4].astype(F32)
    q_bound = (x.shape[-1] ** 0.5) * jnp.max(jnp.abs(pre_g)) * jnp.sqrt(jnp.max(jnp.sum(q_cols * q_cols, axis=0)))
    budget = (FAST_LOG2_BUDGET - 1.0 - jnp.log2(q_bound)).astype(F32).reshape(1)
    bsz, seq, dm = x.shape
    width = w_in.shape[1] // 4
    n_heads = width // HGRN_HEAD_DIM
    d = HGRN_HEAD_DIM
    const = lambda *shape: pl.BlockSpec(shape, lambda b, l: (0,) * len(shape))
    return pl.pallas_call(
        _l0_prompt_kernel,
        grid=(bsz, seq // tm),
        in_specs=[
            pl.BlockSpec(memory_space=pltpu.SMEM),
            pl.BlockSpec((1, tm, dm), lambda b, l: (b, l, 0)),
            const(1, dm),
            pl.BlockSpec((dm, 4 * width), lambda b, l: (0, 0), pipeline_mode=pl.Buffered(1)),
            const(*lb_logits.shape),
            const(1, width),
        ],
        out_specs=[
            pl.BlockSpec((1, tm, width), lambda b, l: (b, l, 0)),
            pl.BlockSpec((1, n_heads, d, d), lambda b, l: (b, 0, 0, 0)),
        ],
        out_shape=[
            jax.ShapeDtypeStruct((bsz, seq, width), BF16),
            jax.ShapeDtypeStruct((bsz, n_heads, d, d), F32),
        ],
        scratch_shapes=[
            pltpu.VMEM((n_heads, d, d), F32),
            pltpu.VMEM((tm, width), F32),
            pltpu.VMEM((tm, width), F32),
            pltpu.VMEM((tm, width), F32),
            pltpu.VMEM((tm, width), BF16),
            pltpu.VMEM((tm, width), F32),
        ],
        compiler_params=pltpu.CompilerParams(
            dimension_semantics=("arbitrary", "arbitrary"), vmem_limit_bytes=VMEM_LIMIT_BYTES),
        name="l0_prompt",
    )(budget, x, pre_g, w_in, lb_logits, onorm)


def _epilogue_kernel(og_ref, h_ref, p_ref, wo_ref, pg_ref, wpe_ref, wpg_ref, out_ref, wo_scr, wpe_scr, wpg_scr):
    @pl.when(pl.program_id(0) == 0)
    def _():
        wo_scr[...] = wo_ref[...].astype(BF16)
        wpe_scr[...] = wpe_ref[...].astype(BF16)
        wpg_scr[...] = wpg_ref[...].astype(BF16)

    mix = _dot(og_ref[...].astype(BF16), wo_scr[...])
    hn = h_ref[...] + _rms(mix) * pg_ref[...]
    pe = _dot(p_ref[...].astype(BF16), wpe_scr[...])
    gate = _sigmoid(_dot(hn.astype(BF16), wpg_scr[...]))
    out_ref[...] = hn + pe * gate


def _epilogue(og, h, p_all, layer, w_out, post_g, w_pe, w_pg, tm):
    t, dm = h.shape
    width = og.shape[1]
    ple = p_all.shape[2]
    tm = min(tm, t)
    weight = lambda slab, *shape: pl.BlockSpec((None,) + shape, lambda i: (slab, 0, 0), pipeline_mode=pl.Buffered(1))
    return pl.pallas_call(
        _epilogue_kernel,
        grid=(t // tm,),
        in_specs=[
            pl.BlockSpec((tm, width), lambda i: (i, 0)),
            pl.BlockSpec((tm, dm), lambda i: (i, 0)),
            pl.BlockSpec((None, tm, ple), lambda i: (layer, i, 0)),
            weight(0, width, dm), pl.BlockSpec((1, dm), lambda i: (0, 0)), weight(layer, ple, dm), weight(layer, dm, dm),
        ],
        out_specs=pl.BlockSpec((tm, dm), lambda i: (i, 0)),
        out_shape=jax.ShapeDtypeStruct((t, dm), F32),
        scratch_shapes=[pltpu.VMEM((width, dm), BF16), pltpu.VMEM((ple, dm), BF16), pltpu.VMEM((dm, dm), BF16)],
        compiler_params=pltpu.CompilerParams(
            dimension_semantics=("arbitrary",), vmem_limit_bytes=VMEM_LIMIT_BYTES),
        name="epilogue",
    )(og, h, p_all, w_out, post_g, w_pe, w_pg)


def _dup_heads(x):
    pair_w = 2 * HEAD_DIM
    lane = lax.broadcasted_iota(jnp.int32, (x.shape[0], pair_w), 1)
    out = []
    for c in range(x.shape[1] // pair_w):
        both = x[:, c * pair_w:(c + 1) * pair_w]
        swapped = pltpu.roll(both, HEAD_DIM, 1)
        out += [jnp.where(lane < HEAD_DIM, both, swapped), jnp.where(lane < HEAD_DIM, swapped, both)]
    return jnp.concatenate(out, axis=1)


def _l1_prompt_kernel(sinks_ref, h_ref, g1_ref, kvg_ref, wkv_ref, win_ref, cos_ref, slo_ref, shi_ref,
                      og_ref, kc_ref, vc_ref, kx_scr, vx_scr, q_scr, z_scr):
    l = pl.program_id(1)
    tm = h_ref.shape[1]
    width = z_scr.shape[1]
    kvw = kx_scr.shape[1]
    kv_cols = kc_ref.shape[2]
    pair_w = 2 * HEAD_DIM
    n_pairs = width // pair_w
    pairs_per_kv = Q_PER_KV // 2

    @pl.when(l == 0)
    def _():
        kx_scr[0:WINDOW, :] = jnp.zeros((WINDOW, kvw), kx_scr.dtype)
        vx_scr[0:WINDOW, :] = jnp.zeros((WINDOW, kvw), vx_scr.dtype)

    @pl.when(l > 0)
    def _():
        kx_scr[0:WINDOW, :] = kx_scr[tm:tm + WINDOW, :]
        vx_scr[0:WINDOW, :] = vx_scr[tm:tm + WINDOW, :]

    h0 = _rms(h_ref[0])
    xn = (h0 * g1_ref[...]).astype(BF16)
    hk = (h0 * kvg_ref[...]).astype(BF16)
    cos, slo, shi = cos_ref[...], slo_ref[...], shi_ref[...]
    tile = lambda t, n: jnp.concatenate([t] * n, axis=1)

    q = _rope(_dot(xn, win_ref[:, 0:width]), tile(cos, n_pairs), tile(slo, n_pairs), tile(shi, n_pairs))
    q_scr[...] = (q * (ATTN_SCALE * LOG2_E)).astype(q_scr.dtype)
    z_scr[...] = _silu(_dot(xn, win_ref[:, width:2 * width]))

    kv = _dot(hk, wkv_ref[...])
    reps = kv_cols // pair_w
    k_rot = _rope(kv[:, :kv_cols], tile(cos, reps), tile(slo, reps), tile(shi, reps))
    v_new = kv[:, kv_cols:]
    kx_scr[WINDOW:WINDOW + tm, :] = _dup_heads(k_rot).astype(kx_scr.dtype)
    vx_scr[WINDOW:WINDOW + tm, :] = _dup_heads(v_new).astype(vx_scr.dtype)

    @pl.when(l == pl.num_programs(1) - 1)
    def _():
        kc_ref[0] = k_rot[tm - WINDOW:, :]
        vc_ref[0] = v_new[tm - WINDOW:, :]

    qi = lax.broadcasted_iota(jnp.int32, (WINDOW, 2 * WINDOW), 0)
    ki = lax.broadcasted_iota(jnp.int32, (WINDOW, 2 * WINDOW), 1)
    band = (ki > qi) & (ki <= qi + WINDOW)
    first_head = lax.broadcasted_iota(jnp.int32, (WINDOW, pair_w), 1) < HEAD_DIM

    def group_body(g, carry):
        kcol = pl.multiple_of(g * pair_w, pair_w)
        cols = [pl.multiple_of((g * pairs_per_kv + p) * pair_w, pair_w) for p in range(pairs_per_kv)]
        for c in range(tm // WINDOW):
            rows = slice(c * WINDOW, (c + 1) * WINDOW)
            keys = slice(c * WINDOW, (c + 2) * WINDOW)
            k_bd = _block_diag2(kx_scr[keys, pl.ds(kcol, pair_w)])
            v_bd = _block_diag2(vx_scr[keys, pl.ds(kcol, pair_w)])
            q_all = jnp.concatenate([q_scr[rows, pl.ds(col, pair_w)] for col in cols], axis=0)
            s_all = _dot_nt(q_all, k_bd)
            mask = band & (ki >= jnp.where(l == 0, WINDOW, 0)) if c == 0 else band
            probs, inv = [], []
            for p in range(pairs_per_kv):
                pair_probs, maxes, sums, pair_sinks = [], [], [], []
                for hh in range(2):
                    sink = sinks_ref[(g * pairs_per_kv + p) * 2 + hh] * LOG2_E
                    s = s_all[p * WINDOW:(p + 1) * WINDOW, hh * 2 * WINDOW:(hh + 1) * 2 * WINDOW]
                    s = jnp.where(mask, s, MASKED)
                    mx = jnp.max(s, axis=-1, keepdims=True)
                    e = jnp.exp2(s - mx)
                    pair_probs.append(e.astype(BF16))
                    maxes.append(mx)
                    sums.append(jnp.sum(e, axis=-1, keepdims=True))
                    pair_sinks.append(sink)
                probs.append(jnp.concatenate(pair_probs, axis=1))
                mx2 = jnp.where(first_head, maxes[0], maxes[1])
                sum2 = jnp.where(first_head, sums[0], sums[1])
                sink2 = jnp.where(first_head, pair_sinks[0], pair_sinks[1])
                inv.append(1.0 / (sum2 + jnp.exp2(sink2 - mx2)))
            o_all = _dot(jnp.concatenate(probs, axis=0), v_bd)
            for p, col in enumerate(cols):
                o2 = o_all[p * WINDOW:(p + 1) * WINDOW] * inv[p]
                og_ref[0, rows, pl.ds(col, pair_w)] = (o2 * z_scr[rows, pl.ds(col, pair_w)]).astype(og_ref.dtype)
        return carry

    lax.fori_loop(0, n_pairs // pairs_per_kv, group_body, 0, unroll=2)


def _l1_prompt(h, g1, kvg, w_kv, w_in, sinks, tables, tm=L1_TILE):
    bsz, seq, dm = h.shape
    width = w_in.shape[1] // 2
    kv_cols = w_kv.shape[1] // 2
    kvw = 2 * kv_cols
    pair_w = 2 * HEAD_DIM
    const = lambda *shape: pl.BlockSpec(shape, lambda b, l: (0,) * len(shape))
    table = pl.BlockSpec((tm, pair_w), lambda b, l: (l, 0))
    weight = lambda *shape: pl.BlockSpec(shape, lambda b, l: (0, 0), pipeline_mode=pl.Buffered(1))
    return pl.pallas_call(
        _l1_prompt_kernel,
        grid=(bsz, seq // tm),
        in_specs=[
            pl.BlockSpec(memory_space=pltpu.SMEM),
            pl.BlockSpec((1, tm, dm), lambda b, l: (b, l, 0)),
            const(1, dm), const(1, dm), weight(dm, 2 * kv_cols), weight(dm, 2 * width),
            table, table, table,
        ],
        out_specs=[
            pl.BlockSpec((1, tm, width), lambda b, l: (b, l, 0)),
            pl.BlockSpec((1, WINDOW, kv_cols), lambda b, l: (b, 0, 0)),
            pl.BlockSpec((1, WINDOW, kv_cols), lambda b, l: (b, 0, 0)),
        ],
        out_shape=[
            jax.ShapeDtypeStruct((bsz, seq, width), BF16),
            jax.ShapeDtypeStruct((bsz, WINDOW, kv_cols), F32),
            jax.ShapeDtypeStruct((bsz, WINDOW, kv_cols), F32),
        ],
        scratch_shapes=[
            pltpu.VMEM((WINDOW + tm, kvw), BF16),
            pltpu.VMEM((WINDOW + tm, kvw), BF16),
            pltpu.VMEM((tm, width), BF16),
            pltpu.VMEM((tm, width), F32),
        ],
        compiler_params=pltpu.CompilerParams(
            dimension_semantics=("arbitrary", "arbitrary"), vmem_limit_bytes=VMEM_LIMIT_BYTES),
        name="l1_prompt",
    )(sinks, h, g1, kvg, w_kv, w_in, *tables)


def _norm_matmul_kernel(x_ref, g_ref, w_ref, o_ref):
    xn = (_rms(x_ref[...]) * g_ref[...]).astype(BF16)
    o_ref[...] = _dot(xn, w_ref[...])


def _norm_matmul(x, g, w, tn=2048):
    t, dm = x.shape
    n = w.shape[1]
    tn = min(tn, n)
    return pl.pallas_call(
        _norm_matmul_kernel,
        grid=(n // tn,),
        in_specs=[
            pl.BlockSpec((t, dm), lambda j: (0, 0)),
            pl.BlockSpec((1, dm), lambda j: (0, 0)),
            pl.BlockSpec((dm, tn), lambda j: (0, j)),
        ],
        out_specs=pl.BlockSpec((t, tn), lambda j: (0, j)),
        out_shape=jax.ShapeDtypeStruct((t, n), F32),
        compiler_params=pltpu.CompilerParams(
            dimension_semantics=("arbitrary",), vmem_limit_bytes=VMEM_LIMIT_BYTES),
        name="norm_matmul",
    )(x, g, w)


def _l0_sample_kernel(u_ref, lbl_ref, on_ref, s0_ref, s1_ref, og_ref):
    bt = u_ref.shape[0]
    width = og_ref.shape[1]
    d = HGRN_HEAD_DIM
    lb_all = _lower_bound(lbl_ref[...])
    terms = 4
    src = lax.broadcasted_iota(jnp.int32, (terms * bt, bt * d), 0) % bt
    dst = lax.broadcasted_iota(jnp.int32, (terms * bt, bt * d), 1) // d
    spread = jnp.where(src == dst, 1.0, 0.0).astype(BF16)

    for h in range(width // d):
        cols = slice(h * d, (h + 1) * d)
        lb = lb_all[:, cols]
        q = _silu(u_ref[:, h * d:(h + 1) * d])
        f = lb + (1.0 - lb) * _sigmoid(u_ref[:, width + h * d:width + (h + 1) * d])
        v = u_ref[:, 2 * width + h * d:2 * width + (h + 1) * d]
        z = _silu(u_ref[:, 3 * width + h * d:3 * width + (h + 1) * d])
        f_hi = f.astype(BF16).astype(F32)
        f_mid = (f - f_hi).astype(BF16).astype(F32)
        f_lo = (f - f_hi - f_mid).astype(BF16).astype(F32)
        f_terms = jnp.concatenate([f_hi, f_mid, f_lo, jnp.zeros_like(f)], axis=0)
        f_wide = _dot(f_terms.T.astype(BF16), spread)
        q_b = q.astype(BF16)
        outs = []
        for j in range(bt):
            v_row = v[j:j + 1, :]
            s_new = v_row + f_wide[:, j * d:(j + 1) * d] * (s0_ref[j, h] - v_row)
            s1_ref[j, h] = s_new
            outs.append(_dot(q_b[j:j + 1, :], s_new.astype(BF16)))
        o = jnp.concatenate(outs, axis=0)
        og_ref[:, cols] = _rms(o) * on_ref[:, cols] * z


def _l0_sample(u, lb_logits, onorm, state, bt=SAMPLE_TILE):
    n, heads, d, _ = state.shape
    width = heads * d
    return pl.pallas_call(
        _l0_sample_kernel,
        grid=(n // bt,),
        in_specs=[
            pl.BlockSpec((bt, 4 * width), lambda i: (i, 0)),
            pl.BlockSpec(lb_logits.shape, lambda i: (0, 0)),
            pl.BlockSpec((1, width), lambda i: (0, 0)),
            pl.BlockSpec((bt, heads, d, d), lambda i: (i, 0, 0, 0)),
        ],
        out_specs=[
            pl.BlockSpec((bt, heads, d, d), lambda i: (i, 0, 0, 0)),
            pl.BlockSpec((bt, width), lambda i: (i, 0)),
        ],
        out_shape=[
            jax.ShapeDtypeStruct(state.shape, F32),
            jax.ShapeDtypeStruct((n, width), F32),
        ],
        compiler_params=pltpu.CompilerParams(
            dimension_semantics=("arbitrary",), vmem_limit_bytes=VMEM_LIMIT_BYTES),
        name="l0_sample",
    )(u, lb_logits, onorm, state)


def _l1_sample_front_kernel(h_ref, g1_ref, kvg_ref, win_ref, wkv_ref, cos_ref, slo_ref, shi_ref,
                            q_ref, z_ref, k_ref, v_ref):
    width = q_ref.shape[1]
    kvw = k_ref.shape[1]
    h0 = _rms(h_ref[...])
    xn = (h0 * g1_ref[...]).astype(BF16)
    hk = (h0 * kvg_ref[...]).astype(BF16)
    cos, slo, shi = cos_ref[...], slo_ref[...], shi_ref[...]
    q = _rope(_dot(xn, win_ref[:, 0:width]), cos, slo, shi)
    q_ref[...] = q * ATTN_SCALE
    z_ref[...] = _silu(_dot(xn, win_ref[:, width:2 * width]))
    kv = _dot(hk, wkv_ref[...])
    k_ref[...] = _rope(kv[:, :kvw], cos[:, :kvw], slo[:, :kvw], shi[:, :kvw])
    v_ref[...] = kv[:, kvw:]


def _l1_sample_front(h, g1, kvg, w_in, w_kv, tables):
    t, dm = h.shape
    width = w_in.shape[1] // 2
    kvw = w_kv.shape[1] // 2
    full = lambda a: pl.BlockSpec(a.shape, lambda i: (0,) * a.ndim)
    args = (h, g1, kvg, w_in, w_kv, *tables)
    return pl.pallas_call(
        _l1_sample_front_kernel,
        grid=(1,),
        in_specs=[full(a) for a in args],
        out_specs=[pl.BlockSpec((t, width), lambda i: (0, 0)), pl.BlockSpec((t, width), lambda i: (0, 0)),
                   pl.BlockSpec((t, kvw), lambda i: (0, 0)), pl.BlockSpec((t, kvw), lambda i: (0, 0))],
        out_shape=[jax.ShapeDtypeStruct((t, width), F32), jax.ShapeDtypeStruct((t, width), F32),
                   jax.ShapeDtypeStruct((t, kvw), F32), jax.ShapeDtypeStruct((t, kvw), F32)],
        compiler_params=pltpu.CompilerParams(
            dimension_semantics=("arbitrary",), vmem_limit_bytes=VMEM_LIMIT_BYTES),
        name="l1_sample_front",
    )(*args)


def _l1_sample_attn_kernel(q_ref, z_ref, kn_ref, vn_ref, ckt_ref, cvt_ref, sink_ref, og_ref):
    bt, kv_heads, hd, wb = ckt_ref.shape
    n_q = q_ref.shape[1]
    kvw = kv_heads * hd
    valid = lax.broadcasted_iota(jnp.int32, (n_q, wb), 1) > wb - WINDOW
    head_kv = lax.broadcasted_iota(jnp.int32, (n_q, kvw), 0) // Q_PER_KV
    own_lanes = lax.broadcasted_iota(jnp.int32, (n_q, kvw), 1) // hd == head_kv
    row_kv = lax.broadcasted_iota(jnp.int32, (n_q, hd), 0) // Q_PER_KV
    sink = sink_ref[...]
    scored = []
    for j in range(bt):
        q = q_ref[j]
        q_bd = jnp.where(own_lanes, jnp.concatenate([q] * kv_heads, axis=1), 0.0)
        scored.append((q_bd, _dot(q_bd.astype(BF16), ckt_ref[j].reshape(kvw, wb).astype(BF16))))
    weighted = []
    for j, (q_bd, s) in enumerate(scored):
        s = jnp.where(valid, s, MASKED)
        s_new = jnp.sum(q_bd * kn_ref[j:j + 1, :], axis=-1, keepdims=True)
        mx = jnp.maximum(jnp.maximum(jnp.max(s, axis=-1, keepdims=True), s_new), sink)
        e = jnp.exp(s - mx)
        e_new = jnp.exp(s_new - mx)
        weighted.append((e.astype(BF16), e_new, jnp.sum(e, axis=-1, keepdims=True) + e_new + jnp.exp(sink - mx)))
    for j, (e, e_new, den) in enumerate(weighted):
        o_all = _dot_nt(e, cvt_ref[j].reshape(kvw, wb).astype(BF16)) + e_new * vn_ref[j:j + 1, :]
        o = jnp.zeros((n_q, hd), F32)
        for g in range(kv_heads):
            o = o + jnp.where(row_kv == g, o_all[:, g * hd:(g + 1) * hd], 0.0)
        og_ref[j] = o / den * z_ref[j]


def _l1_sample_attn(q3, z3, k_new, v_new, cache_kt, cache_vt, sinks_col, bt=SAMPLE_TILE):
    n, heads, hd = q3.shape
    kv_heads, wb = cache_kt.shape[1], cache_kt.shape[3]
    qspec = pl.BlockSpec((bt, heads, hd), lambda i: (i, 0, 0))
    nspec = pl.BlockSpec((bt, kv_heads * hd), lambda i: (i, 0))
    cspec = pl.BlockSpec((bt, kv_heads, hd, wb), lambda i: (i, 0, 0, 0))
    return pl.pallas_call(
        _l1_sample_attn_kernel,
        grid=(n // bt,),
        in_specs=[qspec, qspec, nspec, nspec, cspec, cspec, pl.BlockSpec((heads, 1), lambda i: (0, 0))],
        out_specs=qspec,
        out_shape=jax.ShapeDtypeStruct((n, heads, hd), F32),
        compiler_params=pltpu.CompilerParams(
            dimension_semantics=("arbitrary",), vmem_limit_bytes=VMEM_LIMIT_BYTES),
        name="l1_sample_attn",
    )(q3, z3, k_new, v_new, cache_kt, cache_vt, sinks_col)


def kernel(x_prompt, x_sample, p_prompt, p_sample, state_hgrn, cache_k, cache_v, pre_norm_g, post_norm_g,
           w_in_a, lb_logits, onorm_a, w_out_a, kv_norm_g, w_kv, w_in_b, sinks, w_out_b, w_pe, w_pg):
    bsz, seq, dm = x_prompt.shape
    n_s = x_sample.shape[0]
    assert x_sample.shape[1] == 1 and w_in_a.shape[0] == 1 and w_in_b.shape[0] == 1
    assert seq % L0_TILE == 0 and seq % L1_TILE == 0 and n_s % SAMPLE_TILE == 0
    kv_heads = w_kv.shape[1] // (2 * HEAD_DIM)
    n_q = w_in_b.shape[2] // (2 * HEAD_DIM)
    assert n_q == kv_heads * Q_PER_KV

    row = lambda a: a.reshape(1, -1)
    bf = lambda a: a.astype(BF16)
    w_in_a0, w_in_b0 = bf(w_in_a[0]), bf(w_in_b[0])
    w_kv_b = bf(w_kv)

    pp = p_prompt.reshape(p_prompt.shape[0], bsz * seq, -1)
    ps = p_sample.reshape(p_sample.shape[0], n_s, -1)

    og0, st_p = _l0_prompt(x_prompt, row(pre_norm_g[0]), w_in_a0, lb_logits, row(onorm_a[0]))
    h1 = _epilogue(og0.reshape(bsz * seq, -1), x_prompt.reshape(bsz * seq, dm), pp, 0,
                   w_out_a, row(post_norm_g[0]), w_pe, w_pg, tm=EPILOGUE_TILE)
    tables_p = _rope_tables(jnp.arange(seq), 2)
    og1, k_p, v_p = _l1_prompt(h1.reshape(bsz, seq, dm), row(pre_norm_g[1]), row(kv_norm_g), w_kv_b, w_in_b0,
                               sinks[0], tables_p)
    y_p = _epilogue(og1.reshape(bsz * seq, -1), h1, pp, 1,
                    w_out_b, row(post_norm_g[1]), w_pe, w_pg, tm=EPILOGUE_TILE)

    xs = x_sample.reshape(n_s, dm)
    u0 = _norm_matmul(xs, row(pre_norm_g[0]), w_in_a0)
    st_s, og0s = _l0_sample(u0, lb_logits, row(onorm_a[0]), state_hgrn[0])
    h1s = _epilogue(og0s, xs, ps, 0,
                    w_out_a, row(post_norm_g[0]), w_pe, w_pg, tm=n_s)
    tables_s = _rope_tables(PAST_LEN + jnp.arange(1), n_q)
    q_s, z_s, k_s, v_s = _l1_sample_front(h1s, row(pre_norm_g[1]), row(kv_norm_g), w_in_b0, w_kv_b, tables_s)
    og1s = _l1_sample_attn(q_s.reshape(n_s, n_q, HEAD_DIM), z_s.reshape(n_s, n_q, HEAD_DIM), k_s, v_s,
                           cache_k.transpose(0, 2, 3, 1), cache_v.transpose(0, 2, 3, 1),
                           sinks[0].reshape(n_q, 1))
    y_s = _epilogue(og1s.reshape(n_s, -1), h1s, ps, 1,
                    w_out_b, row(post_norm_g[1]), w_pe, w_pg, tm=n_s)

    return (y_p.reshape(bsz, seq, dm), y_s.reshape(n_s, 1, dm),
            st_p[None], st_s[None],
            k_p.reshape(bsz, WINDOW, kv_heads, HEAD_DIM), v_p.reshape(bsz, WINDOW, kv_heads, HEAD_DIM),
            k_s.reshape(n_s, 1, kv_heads, HEAD_DIM), v_s.reshape(n_s, 1, kv_heads, HEAD_DIM))
```
